```python
import math
import jax
import jax.numpy as jnp
from jax import lax
import numpy as np

D_MODEL = 2048
BATCH = 2
SEQ = 4096
DEPTH = 4

GRID_W = 64
CTX_LEN = 256
ROPE_THETA = 10000.0
NORM_EPS = 1e-6
NEG_INF = -1e30
Q_BLOCK = 128

MLA_H = 4
MLA_Q_LORA = 512
MLA_KV_LORA = 256
MLA_NOPE = 128
MLA_ROPE = 64
MLA_V = 128
SWA_H = 4
SWA_KV_H = 2
SWA_HD = 128
SWA_WINDOW = 128
SWA_BLOCK = 128
NA_H = 4
NA_HD = 128
NA_KH = 8
NA_KW = 16
DIFF_H = 4
DIFF_DK = 64
DIFF_DV = 128
N_BRANCH = 4
BRANCH_W = 512
D_FF = 5632
N_EXPERTS = 8
TOP_K = 2
D_FF_EXPERT = 5632
N_DENSE = (DEPTH + 1) // 2
N_MOE = DEPTH // 2

IN_SIZES = (
    MLA_Q_LORA, MLA_KV_LORA, MLA_ROPE,
    SWA_H * SWA_HD, SWA_KV_H * SWA_HD, SWA_KV_H * SWA_HD,
    NA_H * NA_HD, NA_H * NA_HD, NA_H * NA_HD,
    DIFF_H * 2 * DIFF_DK, DIFF_H * 2 * DIFF_DK, DIFF_H * DIFF_DV,
    N_BRANCH * D_MODEL,
)
IN_WIDTH = sum(IN_SIZES)

kernel_name = "hybrid_gated_branch_flow_backbone"


def rmsnorm(x, g):
    xf = x.astype(jnp.float32)
    y = xf * lax.rsqrt(jnp.mean(xf * xf, axis=-1, keepdims=True) + NORM_EPS)
    return (y * g.astype(jnp.float32)).astype(x.dtype)


def modulate(x, shift, scale):
    return x * (1 + scale) + shift


def split_in(u):
    parts, start = [], 0
    for width in IN_SIZES:
        parts.append(u[..., start:start + width])
        start += width
    return parts


def heads(t, n_h, d):
    return t.reshape(t.shape[0], t.shape[1], n_h, d)


def cat_seq(a, b):
    return jnp.concatenate([a, b], axis=1)


def rope_tables(n, dim):
    pos = jnp.arange(n, dtype=jnp.int32)
    row = (pos // GRID_W).astype(jnp.float32)
    col = (pos % GRID_W).astype(jnp.float32)
    quarter = dim // 4
    inv_freq = ROPE_THETA ** (-jnp.arange(quarter, dtype=jnp.float32) / quarter)
    ang_r = row[:, None] * inv_freq
    ang_c = col[:, None] * inv_freq
    ang = jnp.concatenate([ang_r, ang_r, ang_c, ang_c], axis=-1)
    return jnp.cos(ang), jnp.sin(ang)


def axial_rope(x, cos, sin):
    half = x.shape[-1] // 2
    quarter = half // 2

    def rot(y):
        return jnp.concatenate([-y[..., quarter:], y[..., :quarter]], axis=-1)

    x_rot = jnp.concatenate([rot(x[..., :half]), rot(x[..., half:])], axis=-1)
    return x * cos[:, None, :].astype(x.dtype) + x_rot * sin[:, None, :].astype(x.dtype)


def _identity(p):
    return p


def dense_block_attention(q, k, v, combine):
    bsz, n_q, n_g, dk = q.shape
    n_blk = n_q // Q_BLOCK
    scale = dk ** -0.5
    q_blocks = q.reshape(bsz, n_blk, Q_BLOCK, n_g, dk).transpose(1, 0, 2, 3, 4)

    def one_block(qb):
        s = jnp.einsum("bqgd,bngd->bgqn", qb, k, preferred_element_type=jnp.float32) * scale
        w = combine(jax.nn.softmax(s, axis=-1)).astype(v.dtype)
        return jnp.einsum("bhqn,bnhd->bqhd", w, v)

    o = lax.map(one_block, q_blocks)
    return o.transpose(1, 0, 2, 3, 4).reshape(bsz, n_q, v.shape[2], v.shape[3])


def sink_probs(s, sink):
    m = jnp.maximum(jnp.max(s, axis=-1, keepdims=True), sink)
    e = jnp.exp(s - m)
    return e / (jnp.sum(e, axis=-1, keepdims=True) + jnp.exp(sink - m))


def mla_queries(cq, q_norm_g, w_uq, rope):
    bsz, n, _ = cq.shape
    q = (rmsnorm(cq, q_norm_g) @ w_uq).reshape(bsz, n, MLA_H, MLA_NOPE + MLA_ROPE)
    if rope is None:
        return q
    return jnp.concatenate([q[..., :MLA_NOPE], axial_rope(q[..., MLA_NOPE:], *rope)], axis=-1)


def mla_keys_values(ckv, k_rope, kv_norm_g, w_ukv, rope):
    bsz, n, _ = ckv.shape
    kv = (rmsnorm(ckv, kv_norm_g) @ w_ukv).reshape(bsz, n, MLA_H, MLA_NOPE + MLA_V)
    kr = k_rope[:, :, None, :]
    if rope is not None:
        kr = axial_rope(kr, *rope)
    k = jnp.concatenate([kv[..., :MLA_NOPE], jnp.broadcast_to(kr, (bsz, n, MLA_H, MLA_ROPE))], axis=-1)
    return k, kv[..., MLA_NOPE:]


def window_gqa_attention(q, k, v, k_ctx, v_ctx, sink):
    bsz, n, n_h, d = q.shape
    n_kv = k.shape[2]
    grp = n_h // n_kv
    blk = SWA_BLOCK
    n_blk = n // blk

    def bands(t):
        tp = jnp.pad(t, ((0, 0), (blk, blk), (0, 0), (0, 0))).reshape(bsz, n_blk + 2, blk, n_kv, d)
        return jnp.concatenate([tp[:, :-2], tp[:, 1:-1], tp[:, 2:]], axis=2)

    kb, vb = bands(k), bands(v)
    qb = q.reshape(bsz, n_blk, blk, n_kv, grp, d)
    scale = d ** -0.5
    s_loc = jnp.einsum("bnqkgd,bnjkd->bnkgqj", qb, kb, preferred_element_type=jnp.float32) * scale
    s_ctx = jnp.einsum("bnqkgd,bckd->bnkgqc", qb, k_ctx, preferred_element_type=jnp.float32) * scale
    qi = jnp.arange(blk)[:, None]
    jj = jnp.arange(3 * blk)[None, :]
    key_pos = jnp.arange(n_blk)[:, None, None] * blk - blk + jj[None]
    valid = (jnp.abs(jj - blk - qi)[None] <= SWA_WINDOW) & (key_pos >= 0) & (key_pos < n)
    s_loc = jnp.where(valid[None, :, None, None], s_loc, NEG_INF)
    sink_b = sink.astype(jnp.float32).reshape(n_kv, grp)[None, None, :, :, None, None]
    p = sink_probs(jnp.concatenate([s_ctx, s_loc], axis=-1), sink_b).astype(v.dtype)
    n_ctx = k_ctx.shape[1]
    o = (jnp.einsum("bnkgqc,bckd->bnqkgd", p[..., :n_ctx], v_ctx)
         + jnp.einsum("bnkgqj,bnjkd->bnqkgd", p[..., n_ctx:], vb))
    return o.reshape(bsz, n, n_h, d)


def context_sink_attention(q, k, v, sink):
    bsz, n, n_h, d = q.shape
    n_kv = k.shape[2]
    grp = n_h // n_kv
    qg = q.reshape(bsz, n, n_kv, grp, d)
    s = jnp.einsum("bqkgd,bckd->bkgqc", qg, k, preferred_element_type=jnp.float32) * d ** -0.5
    p = sink_probs(s, sink.astype(jnp.float32).reshape(n_kv, grp)[None, :, :, None, None])
    o = jnp.einsum("bkgqc,bckd->bqkgd", p.astype(v.dtype), v)
    return o.reshape(bsz, n, n_h, d)


def neighbourhood_attention(q, k, v, k_ctx, v_ctx, rpb):
    bsz, n, n_h, d = q.shape
    rows = n // GRID_W
    kh = min(NA_KH, rows)
    kw = NA_KW
    r = jnp.arange(rows)
    row_idx = jnp.clip(r - kh // 2, 0, rows - kh)[:, None] + jnp.arange(kh)[None, :]
    col = jnp.arange(GRID_W)
    col_start = jnp.clip(col - kw // 2, 0, GRID_W - kw)
    col_valid = (col[None, :] >= col_start[:, None]) & (col[None, :] < col_start[:, None] + kw)
    dr = row_idx - r[:, None] + (NA_KH - 1)
    dc = jnp.clip(col[None, :] - col[:, None], -(kw - 1), kw - 1) + (kw - 1)
    bias = rpb[:, dr][..., dc]
    bias = bias.transpose(1, 0, 3, 2, 4).reshape(rows, n_h, GRID_W, kh * GRID_W).astype(jnp.float32)
    valid = jnp.broadcast_to(col_valid[:, None, :], (GRID_W, kh, GRID_W)).reshape(GRID_W, kh * GRID_W)
    qg = q.reshape(bsz, rows, GRID_W, n_h, d)
    kg = k.reshape(bsz, rows, GRID_W, n_h, d)[:, row_idx].reshape(bsz, rows, kh * GRID_W, n_h, d)
    vg = v.reshape(bsz, rows, GRID_W, n_h, d)[:, row_idx].reshape(bsz, rows, kh * GRID_W, n_h, d)
    scale = d ** -0.5
    s_loc = jnp.einsum("brqhd,brkhd->brhqk", qg, kg, preferred_element_type=jnp.float32) * scale + bias[None]
    s_loc = jnp.where(valid, s_loc, NEG_INF)
    s_ctx = jnp.einsum("brqhd,bchd->brhqc", qg, k_ctx, preferred_element_type=jnp.float32) * scale
    p = jax.nn.softmax(jnp.concatenate([s_ctx, s_loc], axis=-1), axis=-1).astype(v.dtype)
    n_ctx = k_ctx.shape[1]
    o = (jnp.einsum("brhqc,bchd->brqhd", p[..., :n_ctx], v_ctx)
         + jnp.einsum("brhqk,brkhd->brqhd", p[..., n_ctx:], vg))
    return o.reshape(bsz, n, n_h, d)


def diff_qk(t, rope):
    bsz, n, _ = t.shape
    t = t.reshape(bsz, n, 2 * DIFF_H, DIFF_DK)
    return t if rope is None else axial_rope(t, *rope)


def merge_branches(outs, gate_raw, w_branch, w_out):
    bsz, n = gate_raw.shape[:2]
    gates = jax.nn.sigmoid(gate_raw.reshape(bsz, n, N_BRANCH, D_MODEL))
    merged = None
    for m, o in enumerate(outs):
        y = gates[:, :, m] * (o.reshape(bsz, n, BRANCH_W) @ w_branch[m])
        merged = y if merged is None else merged + y
    return merged @ w_out


def token_mixer(u_lat, u_ctx, p, ropes, lam_init, need_ctx):
    L = split_in(u_lat)
    X = split_in(u_ctx)
    lat_out, ctx_out = [], []

    k_c, v_c = mla_keys_values(X[1], X[2], p["mla_kv_norm"], p["mla_w_ukv"], None)
    k_l, v_l = mla_keys_values(L[1], L[2], p["mla_kv_norm"], p["mla_w_ukv"], ropes[MLA_ROPE])
    q_l = mla_queries(L[0], p["mla_q_norm"], p["mla_w_uq"], ropes[MLA_ROPE])
    lat_out.append(dense_block_attention(q_l, cat_seq(k_c, k_l), cat_seq(v_c, v_l), _identity))
    if need_ctx:
        q_c = mla_queries(X[0], p["mla_q_norm"], p["mla_w_uq"], None)
        ctx_out.append(dense_block_attention(q_c, k_c, v_c, _identity))

    rope_b = ropes[SWA_HD]
    q_l = axial_rope(heads(L[3], SWA_H, SWA_HD), *rope_b)
    k_l = axial_rope(heads(L[4], SWA_KV_H, SWA_HD), *rope_b)
    v_l = heads(L[5], SWA_KV_H, SWA_HD)
    k_c = heads(X[4], SWA_KV_H, SWA_HD)
    v_c = heads(X[5], SWA_KV_H, SWA_HD)
    lat_out.append(window_gqa_attention(q_l, k_l, v_l, k_c, v_c, p["swa_sink"]))
    if need_ctx:
        ctx_out.append(context_sink_attention(heads(X[3], SWA_H, SWA_HD), k_c, v_c, p["swa_sink"]))

    k_c = heads(X[7], NA_H, NA_HD)
    v_c = heads(X[8], NA_H, NA_HD)
    lat_out.append(neighbourhood_attention(heads(L[6], NA_H, NA_HD), heads(L[7], NA_H, NA_HD),
                                           heads(L[8], NA_H, NA_HD), k_c, v_c, p["na_rpb"]))
    if need_ctx:
        ctx_out.append(dense_block_attention(heads(X[6], NA_H, NA_HD), k_c, v_c, _identity))

    lp = p["diff_lambda"].astype(jnp.float32)
    lam = jnp.exp(jnp.sum(lp[0] * lp[1])) - jnp.exp(jnp.sum(lp[2] * lp[3])) + lam_init

    def diff_combine(pr):
        pr = pr.reshape(pr.shape[0], DIFF_H, 2, pr.shape[2], pr.shape[3])
        return pr[:, :, 0] - lam * pr[:, :, 1]

    def diff_post(o):
        return rmsnorm(o, p["diff_subln_g"]) * (1.0 - lam_init)

    k_c = diff_qk(X[10], None)
    v_c = heads(X[11], DIFF_H, DIFF_DV)
    k_l = diff_qk(L[10], ropes[DIFF_DK])
    v_l = heads(L[11], DIFF_H, DIFF_DV)
    lat_out.append(diff_post(dense_block_attention(diff_qk(L[9], ropes[DIFF_DK]), cat_seq(k_c, k_l),
                                                   cat_seq(v_c, v_l), diff_combine)))
    if need_ctx:
        ctx_out.append(diff_post(dense_block_attention(diff_qk(X[9], None), k_c, v_c, diff_combine)))

    y_lat = merge_branches(lat_out, L[12], p["w_branch"], p["w_out"])
    y_ctx = merge_branches(ctx_out, X[12], p["w_branch"], p["w_out"]) if need_ctx else None
    return y_lat, y_ctx


def swiglu(x, w1, w3, w2):
    return (jax.nn.silu(x @ w1) * (x @ w3)) @ w2


def moe_swiglu(x, router, w1, w3, w2):
    logits = (x @ router).astype(jnp.float32)
    top_v, top_i = lax.top_k(logits, TOP_K)
    top_g = jax.nn.softmax(top_v, axis=-1)
    gate = jnp.sum(jax.nn.one_hot(top_i, N_EXPERTS, dtype=jnp.float32) * top_g[..., None], axis=-2)
    out = None
    for e in range(N_EXPERTS):
        y = gate[..., e:e + 1].astype(x.dtype) * swiglu(x, w1[e], w3[e], w2[e])
        out = y if out is None else out + y
    return out


def setup_inputs(seed: int = 0) -> dict:
    key = jax.random.key(seed)
    keys = iter(jax.random.split(key, 32))

    def nrm(shape, scale):
        return jax.random.normal(next(keys), shape, jnp.float32) * scale

    L, D = DEPTH, D_MODEL
    return {
        "x": nrm((BATCH, SEQ, D), 1.0),
        "c": nrm((BATCH, D), 1.0),
        "ctx": nrm((BATCH, CTX_LEN, D), 1.0),
        "c_ctx": nrm((D,), 1.0),
        "mod_w": nrm((L, D, 6 * D), 0.5 * D ** -0.5),
        "mod_b": nrm((L, 6 * D), 0.02),
        "norm1_g": 1.0 + nrm((L, D), 0.05),
        "norm2_g": 1.0 + nrm((L, D), 0.05),
        "w_in": nrm((L, D, IN_WIDTH), D ** -0.5),
        "mla_q_norm": 1.0 + nrm((L, MLA_Q_LORA), 0.05),
        "mla_kv_norm": 1.0 + nrm((L, MLA_KV_LORA), 0.05),
        "mla_w_uq": nrm((L, MLA_Q_LORA, MLA_H * (MLA_NOPE + MLA_ROPE)), MLA_Q_LORA ** -0.5),
        "mla_w_ukv": nrm((L, MLA_KV_LORA, MLA_H * (MLA_NOPE + MLA_V)), MLA_KV_LORA ** -0.5),
        "swa_sink": nrm((L, SWA_H), 0.5),
        "na_rpb": nrm((L, NA_H, 2 * NA_KH - 1, 2 * NA_KW - 1), 0.1),
        "diff_lambda": nrm((L, 4, DIFF_DK), 0.1),
        "diff_subln_g": 1.0 + nrm((L, DIFF_DV), 0.05),
        "w_branch": nrm((L, N_BRANCH, BRANCH_W, D), BRANCH_W ** -0.5),
        "w_out": nrm((L, D, D), D ** -0.5),
        "ffn_w1": nrm((N_DENSE, D, D_FF), D ** -0.5),
        "ffn_w3": nrm((N_DENSE, D, D_FF), D ** -0.5),
        "ffn_w2": nrm((N_DENSE, D_FF, D), D_FF ** -0.5),
        "moe_router": nrm((N_MOE, D, N_EXPERTS), D ** -0.5),
        "moe_w1": nrm((N_MOE, N_EXPERTS, D, D_FF_EXPERT), D ** -0.5),
        "moe_w3": nrm((N_MOE, N_EXPERTS, D, D_FF_EXPERT), D ** -0.5),
        "moe_w2": nrm((N_MOE, N_EXPERTS, D_FF_EXPERT, D), D_FF_EXPERT ** -0.5),
        "final_norm_g": 1.0 + nrm((D,), 0.05),
    }


def reference(x, c, ctx, c_ctx, mod_w, mod_b, norm1_g, norm2_g, w_in, mla_q_norm, mla_kv_norm,
              mla_w_uq, mla_w_ukv, swa_sink, na_rpb, diff_lambda, diff_subln_g, w_branch, w_out,
              ffn_w1, ffn_w3, ffn_w2, moe_router, moe_w1, moe_w3, moe_w2, final_norm_g):
    n_lat = x.shape[1]
    ropes = {dim: rope_tables(n_lat, dim) for dim in (MLA_ROPE, SWA_HD, DIFF_DK)}
    h, hc = x, ctx
    for l in range(DEPTH):
        need_ctx = l < DEPTH - 1
        lam_init = 0.8 - 0.6 * math.exp(-0.3 * l)
        mod = jax.nn.silu(c) @ mod_w[l] + mod_b[l]
        mod_c = jax.nn.silu(c_ctx) @ mod_w[l] + mod_b[l]
        sh1, sc1, g1, sh2, sc2, g2 = jnp.split(mod[:, None, :], 6, axis=-1)
        csh1, csc1, cg1, csh2, csc2, cg2 = jnp.split(mod_c, 6, axis=-1)

        u_lat = modulate(rmsnorm(h, norm1_g[l]), sh1, sc1) @ w_in[l]
        u_ctx = modulate(rmsnorm(hc, norm1_g[l]), csh1, csc1) @ w_in[l]
        p = {
            "mla_q_norm": mla_q_norm[l], "mla_kv_norm": mla_kv_norm[l],
            "mla_w_uq": mla_w_uq[l], "mla_w_ukv": mla_w_ukv[l],
            "swa_sink": swa_sink[l], "na_rpb": na_rpb[l],
            "diff_lambda": diff_lambda[l], "diff_subln_g": diff_subln_g[l],
            "w_branch": w_branch[l], "w_out": w_out[l],
        }
        y_lat, y_ctx = token_mixer(u_lat, u_ctx, p, ropes, lam_init, need_ctx)
        h = h + g1 * y_lat
        if need_ctx:
            hc = hc + cg1 * y_ctx

        xn = modulate(rmsnorm(h, norm2_g[l]), sh2, sc2)
        if need_ctx:
            xcn = modulate(rmsnorm(hc, norm2_g[l]), csh2, csc2)
        if l % 2 == 0:
            i = l // 2
            h = h + g2 * swiglu(xn, ffn_w1[i], ffn_w3[i], ffn_w2[i])
            if need_ctx:
                hc = hc + cg2 * swiglu(xcn, ffn_w1[i], ffn_w3[i], ffn_w2[i])
        else:
            i = l // 2
            h = h + g2 * moe_swiglu(xn, moe_router[i], moe_w1[i], moe_w3[i], moe_w2[i])
            if need_ctx:
                hc = hc + cg2 * moe_swiglu(xcn, moe_router[i], moe_w1[i], moe_w3[i], moe_w2[i])
    return rmsnorm(h, final_norm_g)
```

```python
import functools
import math

import jax
import jax.numpy as jnp
import numpy as np
from jax import lax
from jax.experimental import pallas as pl
from jax.experimental.pallas import tpu as pltpu

F32 = jnp.float32
BF16 = jnp.bfloat16

GRID_W = 64
ROPE_THETA = 10000.0
NORM_EPS = 1e-6
NEG_INF = -1e30

MLA_H, MLA_NOPE, MLA_ROPE, MLA_V = 4, 128, 64, 128
SWA_H, SWA_KV_H, SWA_HD, SWA_WINDOW = 4, 2, 128, 128
NA_H, NA_HD, NA_KH, NA_KW = 4, 128, 8, 16
DIFF_H, DIFF_DK, DIFF_DV = 4, 64, 128
N_BRANCH, BRANCH_W = 4, 512
N_EXPERTS, TOP_K = 8, 2
LANES = 128

VMEM_LIMIT = 56 * 1024 * 1024


def _params(n_axes):
    return pltpu.CompilerParams(dimension_semantics=("arbitrary",) * n_axes, vmem_limit_bytes=VMEM_LIMIT)


def _tile(n, pref, mult):
    if n <= pref:
        return n
    t = (pref // mult) * mult
    while t > mult and n % t:
        t -= mult
    assert n % t == 0, (n, pref, mult)
    return t


def _dot(a, b):
    return jnp.dot(a, b, preferred_element_type=F32)


def _dot_nt(a, b):
    return lax.dot_general(a, b, (((1,), (1,)), ((), ())), preferred_element_type=F32)


def _silu(x):
    return x * (1.0 / (1.0 + jnp.exp(-x)))


def _sigmoid(x):
    return 1.0 / (1.0 + jnp.exp(-x))


def _mod_kernel(c_ref, w_ref, b_ref, o_ref):
    a = _silu(c_ref[...]).astype(BF16)
    o_ref[...] = _dot(a, w_ref[...].astype(BF16)) + b_ref[...]


def _mod_all(c8, mod_w, mod_b):
    n_l, d, n = mod_w.shape
    tn = _tile(n, 1024, LANES)
    return pl.pallas_call(
        _mod_kernel,
        grid=(n_l, n // tn),
        in_specs=[
            pl.BlockSpec((8, d), lambda l, j: (0, 0)),
            pl.BlockSpec((None, d, tn), lambda l, j: (l, 0, j)),
            pl.BlockSpec((None, 1, tn), lambda l, j: (l, 0, j)),
        ],
        out_specs=pl.BlockSpec((None, 8, tn), lambda l, j: (l, 0, j)),
        out_shape=jax.ShapeDtypeStruct((n_l, 8, n), F32),
        compiler_params=_params(2),
        name="mod_all",
    )(c8, mod_w, mod_b.reshape(n_l, 1, n))


def _group_of_tile(i, tm, n_lat, seq, n_batch):
    row0 = i * tm
    return jnp.where(row0 < n_lat, row0 // seq, n_batch)


def _norm_kernel(h_ref, g_ref, sh_ref, sc_ref, *rest, tm, n_lat, seq, n_batch, route):
    grp = _group_of_tile(pl.program_id(0), tm, n_lat, seq, n_batch)
    x = h_ref[...]
    y = x * lax.rsqrt(jnp.mean(x * x, axis=-1, keepdims=True) + NORM_EPS) * g_ref[...]
    sh = sh_ref[pl.ds(grp, 1), :]
    sc = sc_ref[pl.ds(grp, 1), :]
    xn = y * (1.0 + sc) + sh
    if not route:
        (o_ref,) = rest
        o_ref[...] = xn.astype(BF16)
        return
    r_ref, o_ref, tg_ref, ti_ref = rest
    o_ref[...] = xn
    logits = jnp.dot(xn, r_ref[...], preferred_element_type=F32, precision=lax.Precision.HIGHEST)
    lane = lax.broadcasted_iota(jnp.int32, logits.shape, 1).astype(F32)
    logits = jnp.where(lane < N_EXPERTS, logits, -jnp.inf)
    m1 = jnp.max(logits, axis=-1, keepdims=True)
    i1 = jnp.min(jnp.where(logits == m1, lane, float(LANES)), axis=-1, keepdims=True)
    rest_l = jnp.where(lane == i1, -jnp.inf, logits)
    m2 = jnp.max(rest_l, axis=-1, keepdims=True)
    i2 = jnp.min(jnp.where(rest_l == m2, lane, float(LANES)), axis=-1, keepdims=True)
    e2 = jnp.exp(m2 - m1)
    den = 1.0 + e2
    tg_ref[...] = jnp.where(lane == 0.0, 1.0 / den, jnp.where(lane == 1.0, e2 / den, 0.0))
    ti_ref[...] = jnp.where(lane == 0.0, i1, jnp.where(lane == 1.0, i2, 0.0)).astype(jnp.int32)


def _norm_mod(h, g, mod_l, shift_chunk, scale_chunk, dims, router=None):
    n_lat, seq, n_batch = dims
    t, d = h.shape
    tm = _tile(math.gcd(seq, t - n_lat), 256, 8)
    route = router is not None
    kern = functools.partial(_norm_kernel, tm=tm, n_lat=n_lat, seq=seq, n_batch=n_batch, route=route)
    in_specs = [
        pl.BlockSpec((tm, d), lambda i: (i, 0)),
        pl.BlockSpec((1, d), lambda i: (0, 0)),
        pl.BlockSpec((8, d), lambda i: (0, shift_chunk)),
        pl.BlockSpec((8, d), lambda i: (0, scale_chunk)),
    ]
    args = [h, g.reshape(1, d), mod_l, mod_l]
    if route:
        in_specs.append(pl.BlockSpec((d, LANES), lambda i: (0, 0)))
        args.append(router)
        out_specs = [pl.BlockSpec((tm, d), lambda i: (i, 0)),
                     pl.BlockSpec((tm, LANES), lambda i: (i, 0)),
                     pl.BlockSpec((tm, LANES), lambda i: (i, 0))]
        out_shape = [jax.ShapeDtypeStruct((t, d), F32),
                     jax.ShapeDtypeStruct((t, LANES), F32),
                     jax.ShapeDtypeStruct((t, LANES), jnp.int32)]
    else:
        out_specs = pl.BlockSpec((tm, d), lambda i: (i, 0))
        out_shape = jax.ShapeDtypeStruct((t, d), BF16)
    return pl.pallas_call(
        kern, grid=(t // tm,), in_specs=in_specs, out_specs=out_specs, out_shape=out_shape,
        compiler_params=_params(1), name="norm_route" if route else "norm_mod",
    )(*args)


def _final_norm_kernel(h_ref, g_ref, o_ref):
    x = h_ref[...]
    o_ref[...] = x * lax.rsqrt(jnp.mean(x * x, axis=-1, keepdims=True) + NORM_EPS) * g_ref[...]


def _final_norm(h, g, n_rows):
    d = h.shape[1]
    tm = _tile(n_rows, 256, 8)
    return pl.pallas_call(
        _final_norm_kernel, grid=(n_rows // tm,),
        in_specs=[pl.BlockSpec((tm, d), lambda i: (i, 0)), pl.BlockSpec((1, d), lambda i: (0, 0))],
        out_specs=pl.BlockSpec((tm, d), lambda i: (i, 0)),
        out_shape=jax.ShapeDtypeStruct((n_rows, d), F32),
        compiler_params=_params(1), name="final_norm",
    )(h, g.reshape(1, d))


def _cast_weights(first, w_refs, wb_refs):
    @pl.when(first)
    def _():
        for w_ref, wb_ref in zip(w_refs, wb_refs):
            wb_ref[...] = w_ref[...].astype(BF16)


def _proj_kernel(x_ref, w_ref, o_ref):
    o_ref[...] = _dot(x_ref[...], w_ref[...]).astype(o_ref.dtype)


def _project_in(xn, w_l, layer):
    t, d = xn.shape
    n = w_l.shape[2]
    tm = _tile(t, 1024, 8) if t % 1024 == 0 else _tile(t, 512, 8)
    tn = _tile(n, 1024, LANES)
    return pl.pallas_call(
        _proj_kernel, grid=(n // tn, t // tm),
        in_specs=[pl.BlockSpec((tm, d), lambda j, i: (i, 0)),
                  pl.BlockSpec((None, d, tn), lambda j, i: (layer, 0, j))],
        out_specs=pl.BlockSpec((tm, tn), lambda j, i: (i, j)),
        out_shape=jax.ShapeDtypeStruct((t, n), BF16),
        compiler_params=_params(2), name="project_in",
    )(xn, w_l)


def _proj_res_kernel(x_ref, w_ref, h_ref, gate_ref, o_ref, wb_ref, *, tm, n_lat, seq, n_batch):
    i = pl.program_id(1)
    _cast_weights(i == 0, [w_ref], [wb_ref])
    grp = _group_of_tile(i, tm, n_lat, seq, n_batch)
    gate = gate_ref[pl.ds(grp, 1), :]
    o_ref[...] = h_ref[...] + gate * _dot(x_ref[...], wb_ref[...])


def _project_residual(x, w_l, layer, h, mod_l, gate_chunk, dims):
    n_lat, seq, n_batch = dims
    t, k = x.shape
    n = w_l.shape[2]
    tm = _tile(math.gcd(seq, t - n_lat), 512, 8)
    tn = _tile(n, 512, LANES)
    kern = functools.partial(_proj_res_kernel, tm=tm, n_lat=n_lat, seq=seq, n_batch=n_batch)
    gate_blk = gate_chunk * (n // tn)
    return pl.pallas_call(
        kern, grid=(n // tn, t // tm),
        in_specs=[pl.BlockSpec((tm, k), lambda j, i: (i, 0)),
                  pl.BlockSpec((None, k, tn), lambda j, i: (layer, 0, j)),
                  pl.BlockSpec((tm, tn), lambda j, i: (i, j)),
                  pl.BlockSpec((8, tn), lambda j, i: (0, gate_blk + j))],
        out_specs=pl.BlockSpec((tm, tn), lambda j, i: (i, j)),
        out_shape=jax.ShapeDtypeStruct(h.shape, F32),
        scratch_shapes=[pltpu.VMEM((k, tn), BF16)],
        input_output_aliases={2: 0},
        compiler_params=_params(2), name="project_residual",
    )(x, w_l, h, mod_l)


def _swiglu_up_kernel(x_ref, w1_ref, w3_ref, o_ref, w1b_ref, w3b_ref):
    _cast_weights(pl.program_id(1) == 0, [w1_ref, w3_ref], [w1b_ref, w3b_ref])
    x = x_ref[...]
    a = _dot(x, w1b_ref[...])
    b = _dot(x, w3b_ref[...])
    o_ref[...] = (_silu(a) * b).astype(BF16)


def _swiglu_up(xn, w1, w3, idx):
    t, d = xn.shape
    f = w1.shape[2]
    tm = _tile(t, 1024, 8) if t % 1024 == 0 else _tile(t, 512, 8)
    tn = _tile(f, 512, LANES)
    return pl.pallas_call(
        _swiglu_up_kernel, grid=(f // tn, t // tm),
        in_specs=[pl.BlockSpec((tm, d), lambda j, i: (i, 0)),
                  pl.BlockSpec((None, d, tn), lambda j, i: (idx, 0, j)),
                  pl.BlockSpec((None, d, tn), lambda j, i: (idx, 0, j))],
        out_specs=pl.BlockSpec((tm, tn), lambda j, i: (i, j)),
        out_shape=jax.ShapeDtypeStruct((t, f), BF16),
        scratch_shapes=[pltpu.VMEM((d, tn), BF16), pltpu.VMEM((d, tn), BF16)],
        compiler_params=_params(2), name="swiglu_up",
    )(xn, w1, w3)


def _merge_kernel(o_ref, g0_ref, g1_ref, g2_ref, g3_ref, w_ref, out_ref, wb_ref):
    _cast_weights(pl.program_id(1) == 0, [w_ref], [wb_ref])
    acc = None
    for m, g_ref in enumerate((g0_ref, g1_ref, g2_ref, g3_ref)):
        y = _sigmoid(g_ref[...].astype(F32)) * _dot(o_ref[:, m * BRANCH_W:(m + 1) * BRANCH_W], wb_ref[m])
        acc = y if acc is None else acc + y
    out_ref[...] = acc.astype(BF16)


def _merge(o, u, w_branch, layer):
    t = o.shape[0]
    d = w_branch.shape[3]
    tm = _tile(t, 512, 8)
    tn = _tile(d, 512, LANES)
    nj = d // tn

    def gate_spec(m):
        return pl.BlockSpec((tm, tn), lambda j, i: (i, m * nj + j))

    return pl.pallas_call(
        _merge_kernel, grid=(nj, t // tm),
        in_specs=[pl.BlockSpec((tm, N_BRANCH * BRANCH_W), lambda j, i: (i, 0)),
                  gate_spec(0), gate_spec(1), gate_spec(2), gate_spec(3),
                  pl.BlockSpec((None, N_BRANCH, BRANCH_W, tn), lambda j, i: (layer, 0, 0, j))],
        out_specs=pl.BlockSpec((tm, tn), lambda j, i: (i, j)),
        out_shape=jax.ShapeDtypeStruct((t, d), BF16),
        scratch_shapes=[pltpu.VMEM((N_BRANCH, BRANCH_W, tn), BF16)],
        compiler_params=_params(2), name="merge",
    )(o, u, u, u, u, w_branch)


def _rope_tables(n, dim):
    pos = jnp.arange(n, dtype=jnp.int32)
    row = (pos // GRID_W).astype(F32)
    col = (pos % GRID_W).astype(F32)
    quarter = dim // 4
    inv_freq = ROPE_THETA ** (-jnp.arange(quarter, dtype=F32) / quarter)
    ang_r = row[:, None] * inv_freq
    ang_c = col[:, None] * inv_freq
    ang = jnp.concatenate([ang_r, ang_r, ang_c, ang_c], axis=-1)
    reps = LANES // dim
    cos = jnp.tile(jnp.cos(ang), (1, reps))
    sin = jnp.tile(jnp.sin(ang), (1, reps))
    lane = jnp.arange(LANES)[None, :]
    first = (lane & quarter) == 0
    return cos, jnp.where(first, -sin, 0.0), jnp.where(first, 0.0, sin)


def _rope(x, cos, sin_a, sin_b, quarter):
    return (x * cos + pltpu.roll(x, LANES - quarter, 1) * sin_a + pltpu.roll(x, quarter, 1) * sin_b)


def _mla_prep_kernel(cq_ref, ckv_ref, kr_ref, qg_ref, kvg_ref, wq_ref, wkv_ref, cos_ref, sa_ref, sb_ref,
                     q_ref, k_ref, v_ref, *, n_lat_tiles):
    is_lat = pl.program_id(0) < n_lat_tiles
    cos = jnp.where(is_lat, cos_ref[...], 1.0)
    sa = jnp.where(is_lat, sa_ref[...], 0.0)
    sb = jnp.where(is_lat, sb_ref[...], 0.0)
    quarter = MLA_ROPE // 4

    def rms(x_ref, g_ref):
        x = x_ref[...].astype(F32)
        return (x * lax.rsqrt(jnp.mean(x * x, axis=-1, keepdims=True) + NORM_EPS) * g_ref[...]).astype(BF16)

    scale = (MLA_NOPE + MLA_ROPE) ** -0.5
    q = _dot(rms(cq_ref, qg_ref), wq_ref[...].astype(BF16)) * scale
    kv = _dot(rms(ckv_ref, kvg_ref), wkv_ref[...].astype(BF16))
    kr = _rope(kr_ref[...].astype(F32), cos, sa, sb, quarter).astype(BF16)
    for hd in range(MLA_H):
        b = hd * 2 * LANES
        q_ref[:, b:b + LANES] = q[:, b:b + LANES].astype(BF16)
        q_ref[:, b + LANES:b + 2 * LANES] = _rope(q[:, b + LANES:b + 2 * LANES], cos, sa, sb, quarter).astype(BF16)
        k_ref[:, b:b + LANES] = kv[:, b:b + LANES].astype(BF16)
        k_ref[:, b + LANES:b + 2 * LANES] = kr
        v_ref[:, hd * LANES:(hd + 1) * LANES] = kv[:, b + LANES:b + 2 * LANES].astype(BF16)


def _mla_prep(u, cols, q_norm, kv_norm, w_uq_p, w_ukv, rope64, dims):
    n_lat, seq, _ = dims
    t = u.shape[0]
    tm = _tile(math.gcd(seq, t - n_lat), 256, 16)
    n_lat_tiles = n_lat // tm
    seq_tiles = seq // tm
    qw = MLA_H * 2 * LANES

    def tab_spec():
        return pl.BlockSpec((tm, LANES), lambda i: (jnp.where(i < n_lat_tiles, i % seq_tiles, 0), 0))

    kern = functools.partial(_mla_prep_kernel, n_lat_tiles=n_lat_tiles)
    return pl.pallas_call(
        kern, grid=(t // tm,),
        in_specs=[pl.BlockSpec((tm, 512), lambda i: (i, cols["cq"] // 512)),
                  pl.BlockSpec((tm, 256), lambda i: (i, cols["ckv"] // 256)),
                  pl.BlockSpec((tm, LANES), lambda i: (i, cols["krope"] // LANES)),
                  pl.BlockSpec((1, 512), lambda i: (0, 0)),
                  pl.BlockSpec((1, 256), lambda i: (0, 0)),
                  pl.BlockSpec((512, qw), lambda i: (0, 0)),
                  pl.BlockSpec((256, qw), lambda i: (0, 0)),
                  tab_spec(), tab_spec(), tab_spec()],
        out_specs=[pl.BlockSpec((tm, qw), lambda i: (i, 0)),
                   pl.BlockSpec((tm, qw), lambda i: (i, 0)),
                   pl.BlockSpec((tm, MLA_H * MLA_V), lambda i: (i, 0))],
        out_shape=[jax.ShapeDtypeStruct((t, qw), BF16),
                   jax.ShapeDtypeStruct((t, qw), BF16),
                   jax.ShapeDtypeStruct((t, MLA_H * MLA_V), BF16)],
        compiler_params=_params(1), name="mla_prep",
    )(u, u, u, q_norm.reshape(1, -1), kv_norm.reshape(1, -1), w_uq_p, w_ukv, *rope64)


def _softmax_pv(scores, values, sink=None):
    m = None
    for s in scores:
        mi = jnp.max(s, axis=-1, keepdims=True)
        m = mi if m is None else jnp.maximum(m, mi)
    if sink is not None:
        m = jnp.maximum(m, sink)
    den = None
    acc = None
    for s, v in zip(scores, values):
        p = jnp.exp(s - m)
        li = jnp.sum(p, axis=-1, keepdims=True)
        den = li if den is None else den + li
        o = _dot(p.astype(BF16), v)
        acc = o if acc is None else acc + o
    if sink is not None:
        den = den + jnp.exp(sink - m)
    return acc / den


def _diff_lambda(lam_ref, lam_init):
    lp = lam_ref[...]
    return (jnp.exp(jnp.sum(lp[0:1, :] * lp[1:2, :], axis=-1, keepdims=True))
            - jnp.exp(jnp.sum(lp[2:3, :] * lp[3:4, :], axis=-1, keepdims=True)) + lam_init)


def _diff_finish(o1, o2, lam, g, lam_init):
    o = o1 - lam * o2
    y = o * lax.rsqrt(jnp.mean(o * o, axis=-1, keepdims=True) + NORM_EPS) * g
    return y * (1.0 - lam_init)


def _mla_attn_kernel(q_ref, kc_ref, kl_ref, vc_ref, vl_ref, oin_ref, o_ref):
    del oin_ref
    q = q_ref[...]
    s_c = _dot_nt(q, kc_ref[...])
    s_l = _dot_nt(q, kl_ref[...])
    o_ref[...] = _softmax_pv([s_c, s_l], [vc_ref[...], vl_ref[...]]).astype(BF16)


def _mla_attention(qa, ka, va, o, dims, branch):
    n_lat, seq, n_batch = dims
    n_ctx = (qa.shape[0] - n_lat) // n_batch
    tq = _tile(seq, 512, 16)
    nq = seq // tq
    lat_blocks = seq // n_ctx if seq % n_ctx == 0 else None
    assert lat_blocks is not None
    ctx_blk0 = n_lat // n_ctx
    kw = 2 * LANES
    return pl.pallas_call(
        _mla_attn_kernel, grid=(n_batch, MLA_H, nq),
        in_specs=[pl.BlockSpec((tq, kw), lambda b, h, i: (b * nq + i, h)),
                  pl.BlockSpec((n_ctx, kw), lambda b, h, i: (ctx_blk0 + b, h)),
                  pl.BlockSpec((seq, kw), lambda b, h, i: (b, h)),
                  pl.BlockSpec((n_ctx, LANES), lambda b, h, i: (ctx_blk0 + b, h)),
                  pl.BlockSpec((seq, LANES), lambda b, h, i: (b, h)),
                  pl.BlockSpec(memory_space=pl.ANY)],
        out_specs=pl.BlockSpec((tq, LANES), lambda b, h, i: (b * nq + i, branch * 4 + h)),
        out_shape=jax.ShapeDtypeStruct(o.shape, o.dtype),
        input_output_aliases={5: 0},
        compiler_params=_params(3), name="mla_attention",
    )(qa, ka, ka, va, va, o)


def _diff_attn_kernel(q_ref, kc_ref, kl_ref, vc_ref, vl_ref, cosq_ref, saq_ref, sbq_ref, cos_ref, sa_ref, sb_ref,
                      lam_ref, g_ref, oin_ref, o_ref, kr_ref, *, lam_init):
    del oin_ref
    quarter = DIFF_DK // 4

    @pl.when(pl.program_id(2) == 0)
    def _():
        kr_ref[...] = _rope(kl_ref[...].astype(F32), cos_ref[...], sa_ref[...], sb_ref[...], quarter).astype(BF16)

    scale = DIFF_DK ** -0.5
    q = _rope(q_ref[...].astype(F32) * scale, cosq_ref[...], saq_ref[...], sbq_ref[...], quarter)
    lane = lax.broadcasted_iota(jnp.int32, q.shape, 1)
    outs = []
    for j in range(2):
        qj = jnp.where((lane >= j * DIFF_DK) & (lane < (j + 1) * DIFF_DK), q, 0.0).astype(BF16)
        s_c = _dot_nt(qj, kc_ref[...])
        s_l = _dot_nt(qj, kr_ref[...])
        outs.append(_softmax_pv([s_c, s_l], [vc_ref[...], vl_ref[...]]))
    lam = _diff_lambda(lam_ref, lam_init)
    o_ref[...] = _diff_finish(outs[0], outs[1], lam, g_ref[...], lam_init).astype(BF16)


def _diff_attention(u, cols, rope64, diff_lambda_l, subln_g, lam_init, o, dims, branch):
    n_lat, seq, n_batch = dims
    n_ctx = (u.shape[0] - n_lat) // n_batch
    tq = _tile(seq, 512, 16)
    nq = seq // tq
    ctx_blk0 = n_lat // n_ctx
    qb, kb, vb = cols["diff_q"] // LANES, cols["diff_k"] // LANES, cols["diff_v"] // LANES
    kern = functools.partial(_diff_attn_kernel, lam_init=lam_init)

    def tab_q():
        return pl.BlockSpec((tq, LANES), lambda b, h, i: (i, 0))

    def tab_k():
        return pl.BlockSpec((seq, LANES), lambda b, h, i: (0, 0))

    return pl.pallas_call(
        kern, grid=(n_batch, DIFF_H, nq),
        in_specs=[pl.BlockSpec((tq, LANES), lambda b, h, i: (b * nq + i, qb + h)),
                  pl.BlockSpec((n_ctx, LANES), lambda b, h, i: (ctx_blk0 + b, kb + h)),
                  pl.BlockSpec((seq, LANES), lambda b, h, i: (b, kb + h)),
                  pl.BlockSpec((n_ctx, LANES), lambda b, h, i: (ctx_blk0 + b, vb + h)),
                  pl.BlockSpec((seq, LANES), lambda b, h, i: (b, vb + h)),
                  tab_q(), tab_q(), tab_q(), tab_k(), tab_k(), tab_k(),
                  pl.BlockSpec((4, DIFF_DK), lambda b, h, i: (0, 0)),
                  pl.BlockSpec((1, DIFF_DV), lambda b, h, i: (0, 0)),
                  pl.BlockSpec(memory_space=pl.ANY)],
        out_specs=pl.BlockSpec((tq, LANES), lambda b, h, i: (b * nq + i, branch * 4 + h)),
        out_shape=jax.ShapeDtypeStruct(o.shape, o.dtype),
        scratch_shapes=[pltpu.VMEM((seq, LANES), BF16)],
        input_output_aliases={13: 0},
        compiler_params=_params(3), name="diff_attention",
    )(u, u, u, u, u, *rope64, *rope64, diff_lambda_l, subln_g.reshape(1, -1), o)


def _local_attn_kernel(*refs, tq, nk, back, seq, use_rope, use_sink):
    it = iter(refs)
    q_ref, kl_ref, vl_ref, kc_ref, vc_ref, bias_ref = (next(it) for _ in range(6))
    if use_rope:
        cosq_ref, saq_ref, sbq_ref, cos_ref, sa_ref, sb_ref = (next(it) for _ in range(6))
    if use_sink:
        sink_ref = next(it)
    oin_ref, o_ref = next(it), next(it)
    del oin_ref
    i = pl.program_id(2)
    scale = LANES ** -0.5
    if use_rope:
        kr_ref = next(it)
        quarter = SWA_HD // 4

        @pl.when(i == 0)
        def _():
            kr_ref[...] = _rope(kl_ref[...].astype(F32), cos_ref[...], sa_ref[...], sb_ref[...],
                                quarter).astype(BF16)

        q = _rope(q_ref[...].astype(F32) * scale, cosq_ref[...], saq_ref[...], sbq_ref[...], quarter).astype(BF16)
        k_src = kr_ref
    else:
        q = (q_ref[...].astype(F32) * scale).astype(BF16)
        k_src = kl_ref
    ks = pl.multiple_of(jnp.clip(i * tq - back, 0, seq - nk), 64)
    s_l = _dot_nt(q, k_src[pl.ds(ks, nk), :]) + bias_ref[...]
    s_c = _dot_nt(q, kc_ref[...])
    sink = sink_ref[pl.program_id(1)] if use_sink else None
    o = _softmax_pv([s_c, s_l], [vc_ref[...], vl_ref[pl.ds(ks, nk), :]], sink=sink)
    o_ref[...] = o.astype(BF16)


def _local_attention(u, qcol, kcol, vcol, n_heads, n_kv, bias, nk, back, rope, sink, o, dims, branch, name):
    n_lat, seq, n_batch = dims
    n_ctx = (u.shape[0] - n_lat) // n_batch
    tq = bias.shape[2]
    nq = seq // tq
    grp = n_heads // n_kv
    ctx_blk0 = n_lat // n_ctx
    qb, kb, vb = qcol // LANES, kcol // LANES, vcol // LANES
    per_head_bias = bias.shape[0] > 1
    use_rope, use_sink = rope is not None, sink is not None
    kern = functools.partial(_local_attn_kernel, tq=tq, nk=nk, back=back, seq=seq,
                             use_rope=use_rope, use_sink=use_sink)
    in_specs = [pl.BlockSpec((tq, LANES), lambda b, h, i: (b * nq + i, qb + h)),
                pl.BlockSpec((seq, LANES), lambda b, h, i: (b, kb + h // grp)),
                pl.BlockSpec((seq, LANES), lambda b, h, i: (b, vb + h // grp)),
                pl.BlockSpec((n_ctx, LANES), lambda b, h, i: (ctx_blk0 + b, kb + h // grp)),
                pl.BlockSpec((n_ctx, LANES), lambda b, h, i: (ctx_blk0 + b, vb + h // grp)),
                pl.BlockSpec((None, None, tq, nk), lambda b, h, i: (h if per_head_bias else 0, i, 0, 0))]
    args = [u, u, u, u, u, bias]
    scratch = []
    if use_rope:
        in_specs += [pl.BlockSpec((tq, LANES), lambda b, h, i: (i, 0))] * 3
        in_specs += [pl.BlockSpec((seq, LANES), lambda b, h, i: (0, 0))] * 3
        args += [*rope, *rope]
        scratch.append(pltpu.VMEM((seq, LANES), BF16))
    if use_sink:
        in_specs.append(pl.BlockSpec(memory_space=pltpu.SMEM))
        args.append(sink)
    in_specs.append(pl.BlockSpec(memory_space=pl.ANY))
    args.append(o)
    return pl.pallas_call(
        kern, grid=(n_batch, n_heads, nq), in_specs=in_specs,
        out_specs=pl.BlockSpec((tq, LANES), lambda b, h, i: (b * nq + i, branch * 4 + h)),
        out_shape=jax.ShapeDtypeStruct(o.shape, o.dtype),
        scratch_shapes=scratch,
        input_output_aliases={len(args) - 1: 0},
        compiler_params=_params(3), name=name,
    )(*args)


def _local_tiling(seq, span):
    tq = _tile(seq, 512, 64)
    nk = min(seq, tq + 2 * span)
    return tq, nk


def _swa_bias(seq):
    tq, nk = _local_tiling(seq, SWA_WINDOW)
    nq = seq // tq
    tiles = []
    for i in range(nq):
        ks = min(max(i * tq - SWA_WINDOW, 0), seq - nk)
        qp = i * tq + np.arange(tq)[:, None]
        kp = ks + np.arange(nk)[None, :]
        tiles.append(np.where(np.abs(qp - kp) <= SWA_WINDOW, 0.0, NEG_INF))
    return jnp.asarray(np.stack(tiles)[None], F32), nk, SWA_WINDOW


def _na_bias(rpb, seq):
    rows = seq // GRID_W
    kh = min(NA_KH, rows)
    back_rows = kh // 2
    tq, nk = _local_tiling(seq, back_rows * GRID_W)
    nq, qr_n, kr_n = seq // tq, tq // GRID_W, nk // GRID_W
    col = np.arange(GRID_W)
    col_start = np.clip(col - NA_KW // 2, 0, GRID_W - NA_KW)
    col_valid = (col[None, :] >= col_start[:, None]) & (col[None, :] < col_start[:, None] + NA_KW)
    dc = np.clip(col[None, :] - col[:, None], -(NA_KW - 1), NA_KW - 1) + (NA_KW - 1)
    toe = jnp.where(col_valid[None, None], rpb[:, :, dc], NEG_INF)
    masked = jnp.full((rpb.shape[0], 1, GRID_W, GRID_W), NEG_INF, F32)
    toe = jnp.concatenate([toe.astype(F32), masked], axis=1)
    n_d = 2 * NA_KH - 1
    tiles = []
    for i in range(nq):
        ks_row = min(max(i * qr_n - back_rows, 0), rows - kr_n)
        qr = i * qr_n + np.arange(qr_n)[:, None]
        kr = ks_row + np.arange(kr_n)[None, :]
        start = np.clip(qr - kh // 2, 0, rows - kh)
        ok = (kr >= start) & (kr < start + kh)
        d_idx = np.where(ok, kr - qr + (NA_KH - 1), n_d)
        blk = toe[:, d_idx]
        tiles.append(blk.transpose(0, 1, 3, 2, 4).reshape(rpb.shape[0], tq, nk))
    return jnp.stack(tiles, axis=1), nk, back_rows * GRID_W


def _ctx_attn_kernel(u_ref, qa_ref, ka_ref, va_ref, sink_ref, lam_ref, g_ref, oin_ref, o_ref, *, cols, lam_init):
    del oin_ref

    def col(name, hd, width=LANES):
        c0 = cols[name] + hd * width
        return u_ref[:, c0:c0 + width]

    for hd in range(MLA_H):
        q = qa_ref[:, hd * 2 * LANES:(hd + 1) * 2 * LANES]
        k = ka_ref[:, hd * 2 * LANES:(hd + 1) * 2 * LANES]
        v = va_ref[:, hd * LANES:(hd + 1) * LANES]
        o_ref[:, hd * LANES:(hd + 1) * LANES] = _softmax_pv([_dot_nt(q, k)], [v]).astype(BF16)
    grp = SWA_H // SWA_KV_H
    scale = SWA_HD ** -0.5
    for hd in range(SWA_H):
        q = (col("swa_q", hd).astype(F32) * scale).astype(BF16)
        o = _softmax_pv([_dot_nt(q, col("swa_k", hd // grp))], [col("swa_v", hd // grp)], sink=sink_ref[hd])
        o_ref[:, BRANCH_W + hd * LANES:BRANCH_W + (hd + 1) * LANES] = o.astype(BF16)
    scale = NA_HD ** -0.5
    for hd in range(NA_H):
        q = (col("na_q", hd).astype(F32) * scale).astype(BF16)
        o = _softmax_pv([_dot_nt(q, col("na_k", hd))], [col("na_v", hd)])
        o_ref[:, 2 * BRANCH_W + hd * LANES:2 * BRANCH_W + (hd + 1) * LANES] = o.astype(BF16)
    scale = DIFF_DK ** -0.5
    lam = _diff_lambda(lam_ref, lam_init)
    for hd in range(DIFF_H):
        q = col("diff_q", hd).astype(F32) * scale
        k = col("diff_k", hd)
        v = col("diff_v", hd)
        lane = lax.broadcasted_iota(jnp.int32, q.shape, 1)
        outs = []
        for j in range(2):
            qj = jnp.where((lane >= j * DIFF_DK) & (lane < (j + 1) * DIFF_DK), q, 0.0).astype(BF16)
            outs.append(_softmax_pv([_dot_nt(qj, k)], [v]))
        o = _diff_finish(outs[0], outs[1], lam, g_ref[...], lam_init)
        o_ref[:, 3 * BRANCH_W + hd * LANES:3 * BRANCH_W + (hd + 1) * LANES] = o.astype(BF16)


def _ctx_attention(u, cols, qa, ka, va, sink, diff_lambda_l, subln_g, lam_init, o, dims):
    n_lat, seq, n_batch = dims
    n_ctx = (u.shape[0] - n_lat) // n_batch
    blk0 = n_lat // n_ctx
    kern = functools.partial(_ctx_attn_kernel, cols=cols, lam_init=lam_init)
    return pl.pallas_call(
        kern, grid=(n_batch,),
        in_specs=[pl.BlockSpec((n_ctx, u.shape[1]), lambda b: (blk0 + b, 0)),
                  pl.BlockSpec((n_ctx, qa.shape[1]), lambda b: (blk0 + b, 0)),
                  pl.BlockSpec((n_ctx, ka.shape[1]), lambda b: (blk0 + b, 0)),
                  pl.BlockSpec((n_ctx, va.shape[1]), lambda b: (blk0 + b, 0)),
                  pl.BlockSpec(memory_space=pltpu.SMEM),
                  pl.BlockSpec((4, DIFF_DK), lambda b: (0, 0)),
                  pl.BlockSpec((1, DIFF_DV), lambda b: (0, 0)),
                  pl.BlockSpec(memory_space=pl.ANY)],
        out_specs=pl.BlockSpec((n_ctx, o.shape[1]), lambda b: (blk0 + b, 0)),
        out_shape=jax.ShapeDtypeStruct(o.shape, o.dtype),
        input_output_aliases={7: 0},
        compiler_params=_params(1), name="ctx_attention",
    )(u, qa, ka, va, sink, diff_lambda_l, subln_g.reshape(1, -1), o)


MOE_TM = 512
GATHER_ROWS = 128


def _route_plan(top_i, tm):
    t = top_i.shape[0]
    n_assign = t * TOP_K
    n_tiles = (n_assign + N_EXPERTS * (tm - 1)) // tm
    e_flat = top_i.reshape(-1)
    onehot = (e_flat[:, None] == jnp.arange(N_EXPERTS)[None, :]).astype(jnp.int32)
    csum = jnp.cumsum(onehot, axis=0)
    rank = jnp.sum(onehot * csum, axis=1) - 1
    counts = csum[-1]
    padded = ((counts + tm - 1) // tm) * tm
    ends = jnp.cumsum(padded)
    offs = ends - padded
    pos = (jnp.sum(onehot * offs[None, :], axis=1) + rank).astype(jnp.int32)
    src = jnp.zeros((n_tiles * tm,), jnp.int32).at[pos].set(jnp.arange(n_assign, dtype=jnp.int32) // TOP_K)
    tile_start = jnp.arange(n_tiles, dtype=jnp.int32) * tm
    tile_expert = jnp.minimum(jnp.sum((tile_start[:, None] >= ends[None, :]).astype(jnp.int32), axis=1),
                              N_EXPERTS - 1).astype(jnp.int32)
    tile_valid = (tile_start < ends[-1]).astype(jnp.int32)
    return pos, src, tile_expert, tile_valid


def _row_copies(idx_ref, base, src_hbm, bufs, sems, slot, n_rows, idx_stride, start):
    def body(r, carry):
        for s, buf in enumerate(bufs):
            row = idx_ref[(base + r) * idx_stride + s]
            cp = pltpu.make_async_copy(src_hbm.at[pl.ds(row, 1), :], buf.at[slot, pl.ds(r, 1), :], sems.at[slot, s])
            if start:
                cp.start()
            else:
                cp.wait()
        return carry
    lax.fori_loop(0, n_rows, body, 0)


def _gather_kernel(src_ref, x_hbm, o_ref, buf, sems, *, rows):
    i = pl.program_id(0)
    n = pl.num_programs(0)
    slot = i % 2

    @pl.when(i == 0)
    def _():
        _row_copies(src_ref, 0, x_hbm, [buf], sems, 0, rows, 1, True)

    @pl.when(i + 1 < n)
    def _():
        _row_copies(src_ref, (i + 1) * rows, x_hbm, [buf], sems, 1 - slot, rows, 1, True)

    _row_copies(src_ref, i * rows, x_hbm, [buf], sems, slot, rows, 1, False)
    o_ref[...] = buf[slot].astype(BF16)


def _gather_rows(xn, src):
    d = xn.shape[1]
    n_rows = src.shape[0]
    rows = _tile(n_rows, GATHER_ROWS, 16)
    kern = functools.partial(_gather_kernel, rows=rows)
    return pl.pallas_call(
        kern,
        grid_spec=pltpu.PrefetchScalarGridSpec(
            num_scalar_prefetch=1, grid=(n_rows // rows,),
            in_specs=[pl.BlockSpec(memory_space=pl.ANY)],
            out_specs=pl.BlockSpec((rows, d), lambda i, s: (i, 0)),
            scratch_shapes=[pltpu.VMEM((2, rows, d), F32), pltpu.SemaphoreType.DMA((2, 1))]),
        out_shape=jax.ShapeDtypeStruct((n_rows, d), BF16),
        compiler_params=_params(1), name="moe_gather",
    )(src, xn)


def _expert_changed(te_ref, i):
    return jnp.logical_or(i == 0, te_ref[i] != te_ref[jnp.maximum(i - 1, 0)])


def _gmm_up_kernel(te_ref, tv_ref, x_ref, w1_ref, w3_ref, o_ref, w1b_ref, w3b_ref):
    i = pl.program_id(1)
    _cast_weights(_expert_changed(te_ref, i), [w1_ref, w3_ref], [w1b_ref, w3b_ref])

    @pl.when(tv_ref[i] == 1)
    def _():
        x = x_ref[...]
        o_ref[...] = (_silu(_dot(x, w1b_ref[...])) * _dot(x, w3b_ref[...])).astype(BF16)

    @pl.when(tv_ref[i] == 0)
    def _():
        o_ref[...] = jnp.zeros(o_ref.shape, BF16)


def _gmm_up(xs, w1, w3, idx, tile_expert, tile_valid, tm):
    r, d = xs.shape
    f = w1.shape[3]
    tn = _tile(f, 512, LANES)
    return pl.pallas_call(
        _gmm_up_kernel,
        grid_spec=pltpu.PrefetchScalarGridSpec(
            num_scalar_prefetch=2, grid=(f // tn, r // tm),
            in_specs=[pl.BlockSpec((tm, d), lambda j, i, te, tv: (i, 0)),
                      pl.BlockSpec((None, None, d, tn), lambda j, i, te, tv: (idx, te[i], 0, j)),
                      pl.BlockSpec((None, None, d, tn), lambda j, i, te, tv: (idx, te[i], 0, j))],
            out_specs=pl.BlockSpec((tm, tn), lambda j, i, te, tv: (i, j)),
            scratch_shapes=[pltpu.VMEM((d, tn), BF16), pltpu.VMEM((d, tn), BF16)]),
        out_shape=jax.ShapeDtypeStruct((r, f), BF16),
        compiler_params=_params(2), name="moe_up",
    )(tile_expert, tile_valid, xs, w1, w3)


def _gmm_down_kernel(te_ref, tv_ref, a_ref, w_ref, o_ref, wb_ref):
    i = pl.program_id(1)
    _cast_weights(_expert_changed(te_ref, i), [w_ref], [wb_ref])

    @pl.when(tv_ref[i] == 1)
    def _():
        o_ref[...] = _dot(a_ref[...], wb_ref[...])

    @pl.when(tv_ref[i] == 0)
    def _():
        o_ref[...] = jnp.zeros(o_ref.shape, F32)


def _gmm_down(a, w2, idx, tile_expert, tile_valid, tm):
    r, f = a.shape
    d = w2.shape[3]
    tn = _tile(d, 512, LANES)
    return pl.pallas_call(
        _gmm_down_kernel,
        grid_spec=pltpu.PrefetchScalarGridSpec(
            num_scalar_prefetch=2, grid=(d // tn, r // tm),
            in_specs=[pl.BlockSpec((tm, f), lambda j, i, te, tv: (i, 0)),
                      pl.BlockSpec((None, None, f, tn), lambda j, i, te, tv: (idx, te[i], 0, j))],
            out_specs=pl.BlockSpec((tm, tn), lambda j, i, te, tv: (i, j)),
            scratch_shapes=[pltpu.VMEM((f, tn), BF16)]),
        out_shape=jax.ShapeDtypeStruct((r, d), F32),
        compiler_params=_params(2), name="moe_down",
    )(tile_expert, tile_valid, a, w2)


def _combine_kernel(pos_ref, y_hbm, h_ref, tg_ref, gate_ref, o_ref, buf0, buf1, sems, *, rows, n_lat, seq, n_batch):
    i = pl.program_id(0)
    n = pl.num_programs(0)
    slot = i % 2
    bufs = [buf0, buf1]

    @pl.when(i == 0)
    def _():
        _row_copies(pos_ref, 0, y_hbm, bufs, sems, 0, rows, TOP_K, True)

    @pl.when(i + 1 < n)
    def _():
        _row_copies(pos_ref, (i + 1) * rows, y_hbm, bufs, sems, 1 - slot, rows, TOP_K, True)

    _row_copies(pos_ref, i * rows, y_hbm, bufs, sems, slot, rows, TOP_K, False)
    grp = _group_of_tile(i, rows, n_lat, seq, n_batch)
    gate = gate_ref[pl.ds(grp, 1), :]
    tg = tg_ref[...]
    mix = tg[:, 0:1] * buf0[slot] + tg[:, 1:2] * buf1[slot]
    o_ref[...] = h_ref[...] + gate * mix


def _combine(h, y, pos, top_g, mod_l, gate_chunk, dims):
    n_lat, seq, n_batch = dims
    t, d = h.shape
    rows = _tile(math.gcd(seq, t - n_lat), GATHER_ROWS, 8)
    kern = functools.partial(_combine_kernel, rows=rows, n_lat=n_lat, seq=seq, n_batch=n_batch)
    return pl.pallas_call(
        kern,
        grid_spec=pltpu.PrefetchScalarGridSpec(
            num_scalar_prefetch=1, grid=(t // rows,),
            in_specs=[pl.BlockSpec(memory_space=pl.ANY),
                      pl.BlockSpec((rows, d), lambda i, p: (i, 0)),
                      pl.BlockSpec((rows, LANES), lambda i, p: (i, 0)),
                      pl.BlockSpec((8, d), lambda i, p: (0, gate_chunk))],
            out_specs=pl.BlockSpec((rows, d), lambda i, p: (i, 0)),
            scratch_shapes=[pltpu.VMEM((2, rows, d), F32), pltpu.VMEM((2, rows, d), F32),
                            pltpu.SemaphoreType.DMA((2, 2))]),
        out_shape=jax.ShapeDtypeStruct(h.shape, F32),
        input_output_aliases={2: 0},
        compiler_params=_params(1), name="moe_combine",
    )(pos, y, h, top_g, mod_l)


def _in_columns(d_model):
    order = [("gates", N_BRANCH * d_model), ("cq", 512), ("ckv", 256), ("krope", LANES),
             ("swa_q", SWA_H * SWA_HD), ("swa_k", SWA_KV_H * SWA_HD), ("swa_v", SWA_KV_H * SWA_HD),
             ("na_q", NA_H * NA_HD), ("na_k", NA_H * NA_HD), ("na_v", NA_H * NA_HD),
             ("diff_q", DIFF_H * 2 * DIFF_DK), ("diff_k", DIFF_H * 2 * DIFF_DK), ("diff_v", DIFF_H * DIFF_DV)]
    cols, off = {}, 0
    for name, width in order:
        cols[name] = off
        off += width
    return cols, off


def _permute_w_in(w_in, d_model):
    sizes = [512, 256, MLA_ROPE, 512, 256, 256, 512, 512, 512, 512, 512, 512, N_BRANCH * d_model]
    names = ["cq", "ckv", "krope", "swa_q", "swa_k", "swa_v", "na_q", "na_k", "na_v",
             "diff_q", "diff_k", "diff_v", "gates"]
    starts = np.concatenate([[0], np.cumsum(sizes)])
    piece = {n: w_in[:, :, int(starts[i]):int(starts[i + 1])].astype(BF16) for i, n in enumerate(names)}
    piece["krope"] = jnp.pad(piece["krope"], ((0, 0), (0, 0), (0, LANES - MLA_ROPE)))
    cols, width = _in_columns(d_model)
    total = -(-width // 1024) * 1024
    parts = [piece[n] for n in cols]
    if total > width:
        parts.append(jnp.zeros(w_in.shape[:2] + (total - width,), BF16))
    return jnp.concatenate(parts, axis=2), cols


def kernel(x, c, ctx, c_ctx, mod_w, mod_b, norm1_g, norm2_g, w_in, mla_q_norm, mla_kv_norm, mla_w_uq, mla_w_ukv,
           swa_sink, na_rpb, diff_lambda, diff_subln_g, w_branch, w_out, ffn_w1, ffn_w3, ffn_w2, moe_router,
           moe_w1, moe_w3, moe_w2, final_norm_g):
    n_batch, seq, d = x.shape
    n_ctx = ctx.shape[1]
    depth = mod_w.shape[0]
    n_lat = n_batch * seq
    dims = (n_lat, seq, n_batch)

    h = jnp.concatenate([x.reshape(n_lat, d), ctx.reshape(n_batch * n_ctx, d)], axis=0)
    t = h.shape[0]
    c8 = jnp.concatenate([c, c_ctx[None, :], jnp.zeros((8 - n_batch - 1, d), F32)], axis=0)
    mod = _mod_all(c8, mod_w, mod_b)

    w_in_p, cols = _permute_w_in(w_in, d)
    w_uq = mla_w_uq.reshape(depth, -1, MLA_H, MLA_NOPE + MLA_ROPE)
    w_uq_p = jnp.pad(w_uq, ((0, 0), (0, 0), (0, 0), (0, 2 * LANES - MLA_NOPE - MLA_ROPE)))
    w_uq_p = w_uq_p.reshape(depth, -1, MLA_H * 2 * LANES)
    rope64 = _rope_tables(seq, MLA_ROPE)
    rope128 = _rope_tables(seq, SWA_HD)
    swa_bias, swa_nk, swa_back = _swa_bias(seq)
    router_p = jnp.pad(moe_router, ((0, 0), (0, 0), (0, LANES - N_EXPERTS)))

    for l in range(depth):
        need_ctx = l < depth - 1
        lam_init = 0.8 - 0.6 * math.exp(-0.3 * l)
        mod_l = mod[l]

        xn = _norm_mod(h, norm1_g[l], mod_l, 0, 1, dims)
        u = _project_in(xn, w_in_p, l)

        qa, ka, va = _mla_prep(u, cols, mla_q_norm[l], mla_kv_norm[l], w_uq_p[l], mla_w_ukv[l], rope64, dims)
        o = jnp.zeros((t, N_BRANCH * BRANCH_W), BF16)
        o = _mla_attention(qa, ka, va, o, dims, 0)
        o = _local_attention(u, cols["swa_q"], cols["swa_k"], cols["swa_v"], SWA_H, SWA_KV_H, swa_bias,
                             swa_nk, swa_back, rope128, swa_sink[l], o, dims, 1, "swa_attention")
        na_bias, na_nk, na_back = _na_bias(na_rpb[l], seq)
        o = _local_attention(u, cols["na_q"], cols["na_k"], cols["na_v"], NA_H, NA_H, na_bias,
                             na_nk, na_back, None, None, o, dims, 2, "na_attention")
        o = _diff_attention(u, cols, rope64, diff_lambda[l], diff_subln_g[l], lam_init, o, dims, 3)
        if need_ctx:
            o = _ctx_attention(u, cols, qa, ka, va, swa_sink[l], diff_lambda[l], diff_subln_g[l], lam_init, o, dims)

        merged = _merge(o, u, w_branch, l)
        h = _project_residual(merged, w_out, l, h, mod_l, 2, dims)

        i = l // 2
        if l % 2 == 0:
            xn = _norm_mod(h, norm2_g[l], mod_l, 3, 4, dims)
            a = _swiglu_up(xn, ffn_w1, ffn_w3, i)
            h = _project_residual(a, ffn_w2, i, h, mod_l, 5, dims)
        else:
            xn, top_g, top_i = _norm_mod(h, norm2_g[l], mod_l, 3, 4, dims, router=router_p[i])
            pos, src, tile_expert, tile_valid = _route_plan(top_i[:, :TOP_K], MOE_TM)
            xs = _gather_rows(xn, src)
            a = _gmm_up(xs, moe_w1, moe_w3, i, tile_expert, tile_valid, MOE_TM)
            y = _gmm_down(a, moe_w2, i, tile_expert, tile_valid, MOE_TM)
            h = _combine(h, y, pos, top_g, mod_l, 5, dims)

    out = _final_norm(h, final_norm_g, n_lat)
    return out.reshape(n_batch, seq, d)
```

```python
import functools
import math

import jax
import jax.numpy as jnp
import numpy as np
from jax import lax
from jax.experimental import pallas as pl
from jax.experimental.pallas import tpu as pltpu

F32 = jnp.float32
BF16 = jnp.bfloat16

GRID_W = 64
ROPE_THETA = 10000.0
NORM_EPS = 1e-6
NEG_INF = -1e30
LOG2E = math.log2(math.e)

MLA_H, MLA_NOPE, MLA_ROPE, MLA_V = 4, 128, 64, 128
SWA_H, SWA_KV_H, SWA_HD, SWA_WINDOW = 4, 2, 128, 128
NA_H, NA_HD, NA_KH, NA_KW = 4, 128, 8, 16
DIFF_H, DIFF_DK, DIFF_DV = 4, 64, 128
N_BRANCH, BRANCH_W = 4, 512
N_EXPERTS, TOP_K = 8, 2
LANES = 128

VMEM_LIMIT = 56 * 1024 * 1024


def _params(n_axes):
    return pltpu.CompilerParams(dimension_semantics=("arbitrary",) * n_axes, vmem_limit_bytes=VMEM_LIMIT)


def _tile(n, pref, mult):
    if n <= pref:
        return n
    t = (pref // mult) * mult
    while t > mult and n % t:
        t -= mult
    assert n % t == 0, (n, pref, mult)
    return t


def _dot(a, b):
    return jnp.dot(a, b, preferred_element_type=F32)


def _dot_nt(a, b):
    return lax.dot_general(a, b, (((1,), (1,)), ((), ())), preferred_element_type=F32)


def _silu(x):
    return x * (1.0 / (1.0 + jnp.exp(-x)))


def _sigmoid(x):
    return 1.0 / (1.0 + jnp.exp(-x))


def _mod_kernel(c_ref, w_ref, b_ref, o_ref):
    a = _silu(c_ref[...]).astype(BF16)
    o_ref[...] = _dot(a, w_ref[...].astype(BF16)) + b_ref[...]


def _mod_all(c8, mod_w, mod_b):
    n_l, d, n = mod_w.shape
    tn = _tile(n, 1024, LANES)
    return pl.pallas_call(
        _mod_kernel,
        grid=(n_l, n // tn),
        in_specs=[
            pl.BlockSpec((8, d), lambda l, j: (0, 0)),
            pl.BlockSpec((None, d, tn), lambda l, j: (l, 0, j)),
            pl.BlockSpec((None, 1, tn), lambda l, j: (l, 0, j)),
        ],
        out_specs=pl.BlockSpec((None, 8, tn), lambda l, j: (l, 0, j)),
        out_shape=jax.ShapeDtypeStruct((n_l, 8, n), F32),
        compiler_params=_params(2),
        name="mod_all",
    )(c8, mod_w, mod_b.reshape(n_l, 1, n))


def _group_of_tile(i, tm, n_lat, seq, n_batch):
    row0 = i * tm
    return jnp.where(row0 < n_lat, row0 // seq, n_batch)


def _norm_kernel(h_ref, g_ref, sh_ref, sc_ref, *rest, tm, n_lat, seq, n_batch, route):
    grp = _group_of_tile(pl.program_id(0), tm, n_lat, seq, n_batch)
    x = h_ref[...]
    y = x * lax.rsqrt(jnp.mean(x * x, axis=-1, keepdims=True) + NORM_EPS) * g_ref[...]
    sh = sh_ref[pl.ds(grp, 1), :]
    sc = sc_ref[pl.ds(grp, 1), :]
    xn = y * (1.0 + sc) + sh
    if not route:
        (o_ref,) = rest
        o_ref[...] = xn.astype(BF16)
        return
    r_ref, o_ref, tg_ref, ti_ref = rest
    bits = lax.bitcast_convert_type(xn.astype(BF16).astype(F32), jnp.uint32)
    half = bits.shape[1] // 2
    o_ref[...] = (bits[:, half:] & jnp.uint32(0xFFFF0000)) | (bits[:, :half] >> 16)
    logits = jnp.dot(xn, r_ref[...], preferred_element_type=F32, precision=lax.Precision.HIGHEST)
    lane = lax.broadcasted_iota(jnp.int32, logits.shape, 1).astype(F32)
    logits = jnp.where(lane < N_EXPERTS, logits, -jnp.inf)
    m1 = jnp.max(logits, axis=-1, keepdims=True)
    i1 = jnp.min(jnp.where(logits == m1, lane, float(LANES)), axis=-1, keepdims=True)
    rest_l = jnp.where(lane == i1, -jnp.inf, logits)
    m2 = jnp.max(rest_l, axis=-1, keepdims=True)
    i2 = jnp.min(jnp.where(rest_l == m2, lane, float(LANES)), axis=-1, keepdims=True)
    e2 = jnp.exp(m2 - m1)
    den = 1.0 + e2
    tg_ref[...] = jnp.where(lane == 0.0, 1.0 / den, jnp.where(lane == 1.0, e2 / den, 0.0))
    ti_ref[...] = jnp.where(lane == 0.0, i1, jnp.where(lane == 1.0, i2, 0.0)).astype(jnp.int32)


def _norm_mod(h, g, mod_l, shift_chunk, scale_chunk, dims, router=None):
    n_lat, seq, n_batch = dims
    t, d = h.shape
    tm = _tile(math.gcd(seq, t - n_lat), 256, 8)
    route = router is not None
    kern = functools.partial(_norm_kernel, tm=tm, n_lat=n_lat, seq=seq, n_batch=n_batch, route=route)
    in_specs = [
        pl.BlockSpec((tm, d), lambda i: (i, 0)),
        pl.BlockSpec((1, d), lambda i: (0, 0)),
        pl.BlockSpec((8, d), lambda i: (0, shift_chunk)),
        pl.BlockSpec((8, d), lambda i: (0, scale_chunk)),
    ]
    args = [h, g.reshape(1, d), mod_l, mod_l]
    if route:
        in_specs.append(pl.BlockSpec((d, LANES), lambda i: (0, 0)))
        args.append(router)
        out_specs = [pl.BlockSpec((tm, d // 2), lambda i: (i, 0)),
                     pl.BlockSpec((tm, LANES), lambda i: (i, 0)),
                     pl.BlockSpec((tm, LANES), lambda i: (i, 0))]
        out_shape = [jax.ShapeDtypeStruct((t, d // 2), jnp.uint32),
                     jax.ShapeDtypeStruct((t, LANES), F32),
                     jax.ShapeDtypeStruct((t, LANES), jnp.int32)]
    else:
        out_specs = pl.BlockSpec((tm, d), lambda i: (i, 0))
        out_shape = jax.ShapeDtypeStruct((t, d), BF16)
    return pl.pallas_call(
        kern, grid=(t // tm,), in_specs=in_specs, out_specs=out_specs, out_shape=out_shape,
        compiler_params=_params(1), name="norm_route" if route else "norm_mod",
    )(*args)


def _final_norm_kernel(h_ref, g_ref, o_ref):
    x = h_ref[...]
    o_ref[...] = x * lax.rsqrt(jnp.mean(x * x, axis=-1, keepdims=True) + NORM_EPS) * g_ref[...]


def _final_norm(h, g, n_rows):
    d = h.shape[1]
    tm = _tile(n_rows, 256, 8)
    return pl.pallas_call(
        _final_norm_kernel, grid=(n_rows // tm,),
        in_specs=[pl.BlockSpec((tm, d), lambda i: (i, 0)), pl.BlockSpec((1, d), lambda i: (0, 0))],
        out_specs=pl.BlockSpec((tm, d), lambda i: (i, 0)),
        out_shape=jax.ShapeDtypeStruct((n_rows, d), F32),
        compiler_params=_params(1), name="final_norm",
    )(h, g.reshape(1, d))


def _cast_weights(first, w_refs, wb_refs):
    @pl.when(first)
    def _():
        for w_ref, wb_ref in zip(w_refs, wb_refs):
            wb_ref[...] = w_ref[...].astype(BF16)


def _proj_kernel(x_ref, w_ref, o_ref):
    o_ref[...] = _dot(x_ref[...], w_ref[...]).astype(o_ref.dtype)


def _project_in(xn, w_l, layer):
    t, d = xn.shape
    n = w_l.shape[2]
    tm = _tile(t, 1024, 8) if t % 1024 == 0 else _tile(t, 512, 8)
    tn = _tile(n, 1024, LANES)
    return pl.pallas_call(
        _proj_kernel, grid=(n // tn, t // tm),
        in_specs=[pl.BlockSpec((tm, d), lambda j, i: (i, 0)),
                  pl.BlockSpec((None, d, tn), lambda j, i: (layer, 0, j))],
        out_specs=pl.BlockSpec((tm, tn), lambda j, i: (i, j)),
        out_shape=jax.ShapeDtypeStruct((t, n), BF16),
        compiler_params=_params(2), name="project_in",
    )(xn, w_l)


def _proj_res_kernel(x_ref, w_ref, h_ref, gate_ref, o_ref, wb_ref, *, tm, n_lat, seq, n_batch):
    i = pl.program_id(1)
    _cast_weights(i == 0, [w_ref], [wb_ref])
    grp = _group_of_tile(i, tm, n_lat, seq, n_batch)
    gate = gate_ref[pl.ds(grp, 1), :]
    o_ref[...] = h_ref[...] + gate * _dot(x_ref[...], wb_ref[...])


def _project_residual(x, w_l, layer, h, mod_l, gate_chunk, dims):
    n_lat, seq, n_batch = dims
    t, k = x.shape
    n = w_l.shape[2]
    tm = _tile(math.gcd(seq, t - n_lat), 512, 8)
    tn = _tile(n, 512, LANES)
    kern = functools.partial(_proj_res_kernel, tm=tm, n_lat=n_lat, seq=seq, n_batch=n_batch)
    gate_blk = gate_chunk * (n // tn)
    return pl.pallas_call(
        kern, grid=(n // tn, t // tm),
        in_specs=[pl.BlockSpec((tm, k), lambda j, i: (i, 0)),
                  pl.BlockSpec((None, k, tn), lambda j, i: (layer, 0, j)),
                  pl.BlockSpec((tm, tn), lambda j, i: (i, j)),
                  pl.BlockSpec((8, tn), lambda j, i: (0, gate_blk + j))],
        out_specs=pl.BlockSpec((tm, tn), lambda j, i: (i, j)),
        out_shape=jax.ShapeDtypeStruct(h.shape, F32),
        scratch_shapes=[pltpu.VMEM((k, tn), BF16)],
        input_output_aliases={2: 0},
        compiler_params=_params(2), name="project_residual",
    )(x, w_l, h, mod_l)


def _swiglu_up_kernel(x_ref, w1_ref, w3_ref, o_ref, w1b_ref, w3b_ref):
    _cast_weights(pl.program_id(1) == 0, [w1_ref, w3_ref], [w1b_ref, w3b_ref])
    x = x_ref[...]
    a = _dot(x, w1b_ref[...])
    b = _dot(x, w3b_ref[...])
    o_ref[...] = (_silu(a) * b).astype(BF16)


def _swiglu_up(xn, w1, w3, idx):
    t, d = xn.shape
    f = w1.shape[2]
    tm = _tile(t, 1024, 8) if t % 1024 == 0 else _tile(t, 512, 8)
    tn = _tile(f, 512, LANES)
    return pl.pallas_call(
        _swiglu_up_kernel, grid=(f // tn, t // tm),
        in_specs=[pl.BlockSpec((tm, d), lambda j, i: (i, 0)),
                  pl.BlockSpec((None, d, tn), lambda j, i: (idx, 0, j)),
                  pl.BlockSpec((None, d, tn), lambda j, i: (idx, 0, j))],
        out_specs=pl.BlockSpec((tm, tn), lambda j, i: (i, j)),
        out_shape=jax.ShapeDtypeStruct((t, f), BF16),
        scratch_shapes=[pltpu.VMEM((d, tn), BF16), pltpu.VMEM((d, tn), BF16)],
        compiler_params=_params(2), name="swiglu_up",
    )(xn, w1, w3)


def _merge_kernel(o_ref, g0_ref, g1_ref, g2_ref, g3_ref, w_ref, out_ref, wb_ref):
    _cast_weights(pl.program_id(1) == 0, [w_ref], [wb_ref])
    acc = None
    for m, g_ref in enumerate((g0_ref, g1_ref, g2_ref, g3_ref)):
        y = _sigmoid(g_ref[...].astype(F32)) * _dot(o_ref[:, m * BRANCH_W:(m + 1) * BRANCH_W], wb_ref[m])
        acc = y if acc is None else acc + y
    out_ref[...] = acc.astype(BF16)


def _merge(o, u, w_branch, layer):
    t = o.shape[0]
    d = w_branch.shape[3]
    tm = _tile(t, 512, 8)
    tn = _tile(d, 512, LANES)
    nj = d // tn

    def gate_spec(m):
        return pl.BlockSpec((tm, tn), lambda j, i: (i, m * nj + j))

    return pl.pallas_call(
        _merge_kernel, grid=(nj, t // tm),
        in_specs=[pl.BlockSpec((tm, N_BRANCH * BRANCH_W), lambda j, i: (i, 0)),
                  gate_spec(0), gate_spec(1), gate_spec(2), gate_spec(3),
                  pl.BlockSpec((None, N_BRANCH, BRANCH_W, tn), lambda j, i: (layer, 0, 0, j))],
        out_specs=pl.BlockSpec((tm, tn), lambda j, i: (i, j)),
        out_shape=jax.ShapeDtypeStruct((t, d), BF16),
        scratch_shapes=[pltpu.VMEM((N_BRANCH, BRANCH_W, tn), BF16)],
        compiler_params=_params(2), name="merge",
    )(o, u, u, u, u, w_branch)


def _rope_tables(n, dim):
    pos = jnp.arange(n, dtype=jnp.int32)
    row = (pos // GRID_W).astype(F32)
    col = (pos % GRID_W).astype(F32)
    quarter = dim // 4
    inv_freq = ROPE_THETA ** (-jnp.arange(quarter, dtype=F32) / quarter)
    ang_r = row[:, None] * inv_freq
    ang_c = col[:, None] * inv_freq
    ang = jnp.concatenate([ang_r, ang_r, ang_c, ang_c], axis=-1)
    reps = LANES // dim
    cos = jnp.tile(jnp.cos(ang), (1, reps))
    sin = jnp.tile(jnp.sin(ang), (1, reps))
    lane = jnp.arange(LANES)[None, :]
    first = (lane & quarter) == 0
    return cos, jnp.where(first, -sin, 0.0), jnp.where(first, 0.0, sin)


def _rope(x, cos, sin_a, sin_b, quarter):
    return (x * cos + pltpu.roll(x, LANES - quarter, 1) * sin_a + pltpu.roll(x, quarter, 1) * sin_b)


def _mla_prep_kernel(cq_ref, ckv_ref, kr_ref, qg_ref, kvg_ref, wq_ref, wkv_ref, cos_ref, sa_ref, sb_ref,
                     q_ref, k_ref, v_ref, *, n_lat_tiles):
    is_lat = pl.program_id(0) < n_lat_tiles
    cos = jnp.where(is_lat, cos_ref[...], 1.0)
    sa = jnp.where(is_lat, sa_ref[...], 0.0)
    sb = jnp.where(is_lat, sb_ref[...], 0.0)
    quarter = MLA_ROPE // 4

    def rms(x_ref, g_ref):
        x = x_ref[...].astype(F32)
        return (x * lax.rsqrt(jnp.mean(x * x, axis=-1, keepdims=True) + NORM_EPS) * g_ref[...]).astype(BF16)

    scale = (MLA_NOPE + MLA_ROPE) ** -0.5 * LOG2E
    q = _dot(rms(cq_ref, qg_ref), wq_ref[...].astype(BF16)) * scale
    kv = _dot(rms(ckv_ref, kvg_ref), wkv_ref[...].astype(BF16))
    kr = _rope(kr_ref[...].astype(F32), cos, sa, sb, quarter).astype(BF16)
    for hd in range(MLA_H):
        b = hd * 2 * LANES
        q_ref[:, b:b + LANES] = q[:, b:b + LANES].astype(BF16)
        q_ref[:, b + LANES:b + 2 * LANES] = _rope(q[:, b + LANES:b + 2 * LANES], cos, sa, sb, quarter).astype(BF16)
        k_ref[:, b:b + LANES] = kv[:, b:b + LANES].astype(BF16)
        k_ref[:, b + LANES:b + 2 * LANES] = kr
        v_ref[:, hd * LANES:(hd + 1) * LANES] = kv[:, b + LANES:b + 2 * LANES].astype(BF16)


def _mla_prep(u, cols, q_norm, kv_norm, w_uq_p, w_ukv, rope64, dims):
    n_lat, seq, _ = dims
    t = u.shape[0]
    tm = _tile(math.gcd(seq, t - n_lat), 256, 16)
    n_lat_tiles = n_lat // tm
    seq_tiles = seq // tm
    qw = MLA_H * 2 * LANES

    def tab_spec():
        return pl.BlockSpec((tm, LANES), lambda i: (jnp.where(i < n_lat_tiles, i % seq_tiles, 0), 0))

    kern = functools.partial(_mla_prep_kernel, n_lat_tiles=n_lat_tiles)
    return pl.pallas_call(
        kern, grid=(t // tm,),
        in_specs=[pl.BlockSpec((tm, 512), lambda i: (i, cols["cq"] // 512)),
                  pl.BlockSpec((tm, 256), lambda i: (i, cols["ckv"] // 256)),
                  pl.BlockSpec((tm, LANES), lambda i: (i, cols["krope"] // LANES)),
                  pl.BlockSpec((1, 512), lambda i: (0, 0)),
                  pl.BlockSpec((1, 256), lambda i: (0, 0)),
                  pl.BlockSpec((512, qw), lambda i: (0, 0)),
                  pl.BlockSpec((256, qw), lambda i: (0, 0)),
                  tab_spec(), tab_spec(), tab_spec()],
        out_specs=[pl.BlockSpec((tm, qw), lambda i: (i, 0)),
                   pl.BlockSpec((tm, qw), lambda i: (i, 0)),
                   pl.BlockSpec((tm, MLA_H * MLA_V), lambda i: (i, 0))],
        out_shape=[jax.ShapeDtypeStruct((t, qw), BF16),
                   jax.ShapeDtypeStruct((t, qw), BF16),
                   jax.ShapeDtypeStruct((t, MLA_H * MLA_V), BF16)],
        compiler_params=_params(1), name="mla_prep",
    )(u, u, u, q_norm.reshape(1, -1), kv_norm.reshape(1, -1), w_uq_p, w_ukv, *rope64)


def _softmax_pv(scores, values, sink=None, base2=False):
    ex = jnp.exp2 if base2 else jnp.exp
    m = None
    for s in scores:
        mi = jnp.max(s, axis=-1, keepdims=True)
        m = mi if m is None else jnp.maximum(m, mi)
    if sink is not None:
        m = jnp.maximum(m, sink)
    den = None
    acc = None
    for s, v in zip(scores, values):
        p = ex(s - m)
        li = jnp.sum(p, axis=-1, keepdims=True)
        den = li if den is None else den + li
        o = _dot(p.astype(BF16), v)
        acc = o if acc is None else acc + o
    if sink is not None:
        den = den + ex(sink - m)
    return acc / den


def _diff_lambda(lam_ref, lam_init):
    lp = lam_ref[...]
    return (jnp.exp(jnp.sum(lp[0:1, :] * lp[1:2, :], axis=-1, keepdims=True))
            - jnp.exp(jnp.sum(lp[2:3, :] * lp[3:4, :], axis=-1, keepdims=True)) + lam_init)


def _diff_finish(o1, o2, lam, g, lam_init):
    o = o1 - lam * o2
    y = o * lax.rsqrt(jnp.mean(o * o, axis=-1, keepdims=True) + NORM_EPS) * g
    return y * (1.0 - lam_init)


DENSE_SUB = 256
DENSE_CHUNK = 256


def _transpose_to(dst_ref, src_ref):
    dst_ref[...] = src_ref[...].astype(F32).T.astype(BF16)


def _two_pass_softmax_pv(q, segments, s_ref):
    rows = q.shape[0]
    m_run, off = None, 0
    for k_ref, _, n in segments:
        for c0 in range(0, n, DENSE_CHUNK):
            w = min(DENSE_CHUNK, n - c0)
            s = _dot_nt(k_ref[c0:c0 + w, :], q)
            s_ref[off + c0:off + c0 + w, :] = s
            part = jnp.max(s.reshape(w // 8, 8, rows), axis=0)
            m_run = part if m_run is None else jnp.maximum(m_run, part)
        off += n
    m = jnp.max(m_run, axis=0, keepdims=True)
    l_run, acc, off = None, None, 0
    for _, vt_ref, n in segments:
        for c0 in range(0, n, DENSE_CHUNK):
            w = min(DENSE_CHUNK, n - c0)
            p = jnp.exp2(s_ref[off + c0:off + c0 + w, :] - m)
            part = jnp.sum(p.reshape(w // 8, 8, rows), axis=0)
            l_run = part if l_run is None else l_run + part
            pv = _dot(vt_ref[:, c0:c0 + w], p.astype(BF16))
            acc = pv if acc is None else acc + pv
        off += n
    o_t = acc * (1.0 / jnp.sum(l_run, axis=0, keepdims=True))
    return o_t.T


def _mla_attn_kernel(q_ref, kc_ref, kl_ref, vc_ref, vl_ref, oin_ref, o_ref, vtc_ref, vtl_ref, s_ref, *, sub):
    del oin_ref

    @pl.when(pl.program_id(2) == 0)
    def _():
        _transpose_to(vtc_ref, vc_ref)
        _transpose_to(vtl_ref, vl_ref)

    segments = [(kc_ref, vtc_ref, kc_ref.shape[0]), (kl_ref, vtl_ref, kl_ref.shape[0])]

    def body(t, carry):
        r0 = pl.multiple_of(t * sub, sub)
        o = _two_pass_softmax_pv(q_ref[pl.ds(r0, sub), :], segments, s_ref)
        o_ref[pl.ds(r0, sub), :] = o.astype(BF16)
        return carry

    lax.fori_loop(0, q_ref.shape[0] // sub, body, 0)


def _mla_attention(qa, ka, va, o, dims, branch):
    n_lat, seq, n_batch = dims
    n_ctx = (qa.shape[0] - n_lat) // n_batch
    tq = _tile(seq, 1024, 16)
    sub = _tile(tq, DENSE_SUB, 16)
    nq = seq // tq
    ctx_blk0 = n_lat // n_ctx
    kw = 2 * LANES
    return pl.pallas_call(
        functools.partial(_mla_attn_kernel, sub=sub), grid=(n_batch, MLA_H, nq),
        scratch_shapes=[pltpu.VMEM((MLA_V, n_ctx), BF16), pltpu.VMEM((MLA_V, seq), BF16),
                        pltpu.VMEM((n_ctx + seq, sub), F32)],
        in_specs=[pl.BlockSpec((tq, kw), lambda b, h, i: (b * nq + i, h)),
                  pl.BlockSpec((n_ctx, kw), lambda b, h, i: (ctx_blk0 + b, h)),
                  pl.BlockSpec((seq, kw), lambda b, h, i: (b, h)),
                  pl.BlockSpec((n_ctx, LANES), lambda b, h, i: (ctx_blk0 + b, h)),
                  pl.BlockSpec((seq, LANES), lambda b, h, i: (b, h)),
                  pl.BlockSpec(memory_space=pl.ANY)],
        out_specs=pl.BlockSpec((tq, LANES), lambda b, h, i: (b * nq + i, branch * 4 + h)),
        out_shape=jax.ShapeDtypeStruct(o.shape, o.dtype),
        input_output_aliases={5: 0},
        compiler_params=_params(3), name="mla_attention",
    )(qa, ka, ka, va, va, o)


def _diff_attn_kernel(q_ref, kc_ref, kl_ref, vc_ref, vl_ref, cosq_ref, saq_ref, sbq_ref, cos_ref, sa_ref, sb_ref,
                      lam_ref, g_ref, oin_ref, o_ref, kr_ref, vtc_ref, vtl_ref, s0_ref, s1_ref, *, lam_init, sub):
    del oin_ref
    quarter = DIFF_DK // 4

    @pl.when(pl.program_id(2) == 0)
    def _():
        kr_ref[...] = _rope(kl_ref[...].astype(F32), cos_ref[...], sa_ref[...], sb_ref[...], quarter).astype(BF16)
        _transpose_to(vtc_ref, vc_ref)
        _transpose_to(vtl_ref, vl_ref)

    scale = DIFF_DK ** -0.5 * LOG2E
    segments = [(kc_ref, vtc_ref, kc_ref.shape[0]), (kr_ref, vtl_ref, kr_ref.shape[0])]
    lam = _diff_lambda(lam_ref, lam_init)

    def body(t, carry):
        r0 = pl.multiple_of(t * sub, sub)
        rows = pl.ds(r0, sub)
        q = _rope(q_ref[rows, :].astype(F32) * scale, cosq_ref[rows, :], saq_ref[rows, :], sbq_ref[rows, :], quarter)
        lane = lax.broadcasted_iota(jnp.int32, q.shape, 1)
        q0 = jnp.where(lane < DIFF_DK, q, 0.0).astype(BF16)
        q1 = jnp.where(lane >= DIFF_DK, q, 0.0).astype(BF16)
        o0 = _two_pass_softmax_pv(q0, segments, s0_ref)
        o1 = _two_pass_softmax_pv(q1, segments, s1_ref)
        o_ref[rows, :] = _diff_finish(o0, o1, lam, g_ref[...], lam_init).astype(BF16)
        return carry

    lax.fori_loop(0, q_ref.shape[0] // sub, body, 0)


def _diff_attention(u, cols, rope64, diff_lambda_l, subln_g, lam_init, o, dims, branch):
    n_lat, seq, n_batch = dims
    n_ctx = (u.shape[0] - n_lat) // n_batch
    tq = _tile(seq, 1024, 16)
    sub = _tile(tq, DENSE_SUB, 16)
    nq = seq // tq
    ctx_blk0 = n_lat // n_ctx
    qb, kb, vb = cols["diff_q"] // LANES, cols["diff_k"] // LANES, cols["diff_v"] // LANES
    kern = functools.partial(_diff_attn_kernel, lam_init=lam_init, sub=sub)

    def tab_q():
        return pl.BlockSpec((tq, LANES), lambda b, h, i: (i, 0))

    def tab_k():
        return pl.BlockSpec((seq, LANES), lambda b, h, i: (0, 0))

    return pl.pallas_call(
        kern, grid=(n_batch, DIFF_H, nq),
        in_specs=[pl.BlockSpec((tq, LANES), lambda b, h, i: (b * nq + i, qb + h)),
                  pl.BlockSpec((n_ctx, LANES), lambda b, h, i: (ctx_blk0 + b, kb + h)),
                  pl.BlockSpec((seq, LANES), lambda b, h, i: (b, kb + h)),
                  pl.BlockSpec((n_ctx, LANES), lambda b, h, i: (ctx_blk0 + b, vb + h)),
                  pl.BlockSpec((seq, LANES), lambda b, h, i: (b, vb + h)),
                  tab_q(), tab_q(), tab_q(), tab_k(), tab_k(), tab_k(),
                  pl.BlockSpec((4, DIFF_DK), lambda b, h, i: (0, 0)),
                  pl.BlockSpec((1, DIFF_DV), lambda b, h, i: (0, 0)),
                  pl.BlockSpec(memory_space=pl.ANY)],
        out_specs=pl.BlockSpec((tq, LANES), lambda b, h, i: (b * nq + i, branch * 4 + h)),
        out_shape=jax.ShapeDtypeStruct(o.shape, o.dtype),
        scratch_shapes=[pltpu.VMEM((seq, LANES), BF16),
                        pltpu.VMEM((DIFF_DV, n_ctx), BF16), pltpu.VMEM((DIFF_DV, seq), BF16),
                        pltpu.VMEM((n_ctx + seq, sub), F32), pltpu.VMEM((n_ctx + seq, sub), F32)],
        input_output_aliases={13: 0},
        compiler_params=_params(3), name="diff_attention",
    )(u, u, u, u, u, *rope64, *rope64, diff_lambda_l, subln_g.reshape(1, -1), o)


def _local_attn_kernel(*refs, tq, nk, back, seq, use_rope, use_sink):
    it = iter(refs)
    q_ref, kl_ref, vl_ref, kc_ref, vc_ref, bias_ref = (next(it) for _ in range(6))
    if use_rope:
        cosq_ref, saq_ref, sbq_ref, cos_ref, sa_ref, sb_ref = (next(it) for _ in range(6))
    if use_sink:
        sink_ref = next(it)
    oin_ref, o_ref = next(it), next(it)
    del oin_ref
    i = pl.program_id(2)
    scale = LANES ** -0.5
    if use_rope:
        kr_ref = next(it)
        quarter = SWA_HD // 4

        @pl.when(i == 0)
        def _():
            kr_ref[...] = _rope(kl_ref[...].astype(F32), cos_ref[...], sa_ref[...], sb_ref[...],
                                quarter).astype(BF16)

        q = _rope(q_ref[...].astype(F32) * scale, cosq_ref[...], saq_ref[...], sbq_ref[...], quarter).astype(BF16)
        k_src = kr_ref
    else:
        q = (q_ref[...].astype(F32) * scale).astype(BF16)
        k_src = kl_ref
    ks = pl.multiple_of(jnp.clip(i * tq - back, 0, seq - nk), 64)
    s_l = _dot_nt(q, k_src[pl.ds(ks, nk), :]) + bias_ref[...]
    s_c = _dot_nt(q, kc_ref[...])
    sink = sink_ref[pl.program_id(1)] if use_sink else None
    o = _softmax_pv([s_c, s_l], [vc_ref[...], vl_ref[pl.ds(ks, nk), :]], sink=sink)
    o_ref[...] = o.astype(BF16)


def _local_attention(u, qcol, kcol, vcol, n_heads, n_kv, bias_plan, rope, sink, o, dims, branch, name):
    bias, tile_map, nk, back = bias_plan

    def bias_tile(i):
        idx = tile_map[-1]
        for t in range(len(tile_map) - 2, -1, -1):
            idx = jnp.where(i == t, tile_map[t], idx)
        return idx

    n_lat, seq, n_batch = dims
    n_ctx = (u.shape[0] - n_lat) // n_batch
    tq = bias.shape[2]
    nq = seq // tq
    grp = n_heads // n_kv
    ctx_blk0 = n_lat // n_ctx
    qb, kb, vb = qcol // LANES, kcol // LANES, vcol // LANES
    per_head_bias = bias.shape[0] > 1
    use_rope, use_sink = rope is not None, sink is not None
    kern = functools.partial(_local_attn_kernel, tq=tq, nk=nk, back=back, seq=seq,
                             use_rope=use_rope, use_sink=use_sink)
    in_specs = [pl.BlockSpec((tq, LANES), lambda b, h, i: (b * nq + i, qb + h)),
                pl.BlockSpec((seq, LANES), lambda b, h, i: (b, kb + h // grp)),
                pl.BlockSpec((seq, LANES), lambda b, h, i: (b, vb + h // grp)),
                pl.BlockSpec((n_ctx, LANES), lambda b, h, i: (ctx_blk0 + b, kb + h // grp)),
                pl.BlockSpec((n_ctx, LANES), lambda b, h, i: (ctx_blk0 + b, vb + h // grp)),
                pl.BlockSpec((None, None, tq, nk),
                             lambda b, h, i: (h if per_head_bias else 0, bias_tile(i), 0, 0))]
    args = [u, u, u, u, u, bias]
    scratch = []
    if use_rope:
        in_specs += [pl.BlockSpec((tq, LANES), lambda b, h, i: (i, 0))] * 3
        in_specs += [pl.BlockSpec((seq, LANES), lambda b, h, i: (0, 0))] * 3
        args += [*rope, *rope]
        scratch.append(pltpu.VMEM((seq, LANES), BF16))
    if use_sink:
        in_specs.append(pl.BlockSpec(memory_space=pltpu.SMEM))
        args.append(sink)
    in_specs.append(pl.BlockSpec(memory_space=pl.ANY))
    args.append(o)
    return pl.pallas_call(
        kern, grid=(n_batch, n_heads, nq), in_specs=in_specs,
        out_specs=pl.BlockSpec((tq, LANES), lambda b, h, i: (b * nq + i, branch * 4 + h)),
        out_shape=jax.ShapeDtypeStruct(o.shape, o.dtype),
        scratch_shapes=scratch,
        input_output_aliases={len(args) - 1: 0},
        compiler_params=_params(3), name=name,
    )(*args)


def _local_tiling(seq, span):
    tq = _tile(seq, 512, 64)
    nk = min(seq, tq + 2 * span)
    return tq, nk


def _dedupe(keys):
    first, uniq, tile_map = {}, [], []
    for i, key in enumerate(keys):
        if key not in first:
            first[key] = len(uniq)
            uniq.append(i)
        tile_map.append(first[key])
    return uniq, tuple(tile_map)


def _swa_bias(seq):
    tq, nk = _local_tiling(seq, SWA_WINDOW)
    nq = seq // tq
    tiles = []
    for i in range(nq):
        ks = min(max(i * tq - SWA_WINDOW, 0), seq - nk)
        qp = i * tq + np.arange(tq)[:, None]
        kp = ks + np.arange(nk)[None, :]
        tiles.append(np.where(np.abs(qp - kp) <= SWA_WINDOW, 0.0, NEG_INF).astype(np.float32))
    uniq, tile_map = _dedupe([t.tobytes() for t in tiles])
    return jnp.asarray(np.stack([tiles[i] for i in uniq])[None], F32), tile_map, nk, SWA_WINDOW


def _na_bias(rpb, seq):
    rows = seq // GRID_W
    kh = min(NA_KH, rows)
    back_rows = kh // 2
    tq, nk = _local_tiling(seq, back_rows * GRID_W)
    nq, qr_n, kr_n = seq // tq, tq // GRID_W, nk // GRID_W
    col = np.arange(GRID_W)
    col_start = np.clip(col - NA_KW // 2, 0, GRID_W - NA_KW)
    col_valid = (col[None, :] >= col_start[:, None]) & (col[None, :] < col_start[:, None] + NA_KW)
    dc = np.clip(col[None, :] - col[:, None], -(NA_KW - 1), NA_KW - 1) + (NA_KW - 1)
    toe = jnp.where(col_valid[None, None], rpb[:, :, dc], NEG_INF)
    masked = jnp.full((rpb.shape[0], 1, GRID_W, GRID_W), NEG_INF, F32)
    toe = jnp.concatenate([toe.astype(F32), masked], axis=1)
    n_d = 2 * NA_KH - 1
    d_idx = []
    for i in range(nq):
        ks_row = min(max(i * qr_n - back_rows, 0), rows - kr_n)
        qr = i * qr_n + np.arange(qr_n)[:, None]
        kr = ks_row + np.arange(kr_n)[None, :]
        start = np.clip(qr - kh // 2, 0, rows - kh)
        ok = (kr >= start) & (kr < start + kh)
        d_idx.append(np.where(ok, kr - qr + (NA_KH - 1), n_d))
    uniq, tile_map = _dedupe([d.tobytes() for d in d_idx])
    tiles = []
    for i in uniq:
        blk = toe[:, d_idx[i]]
        tiles.append(blk.transpose(0, 1, 3, 2, 4).reshape(rpb.shape[0], tq, nk))
    return jnp.stack(tiles, axis=1), tile_map, nk, back_rows * GRID_W


def _ctx_attn_kernel(u_ref, qa_ref, ka_ref, va_ref, sink_ref, lam_ref, g_ref, oin_ref, o_ref, *, cols, lam_init):
    del oin_ref

    def col(name, hd, width=LANES):
        c0 = cols[name] + hd * width
        return u_ref[:, c0:c0 + width]

    for hd in range(MLA_H):
        q = qa_ref[:, hd * 2 * LANES:(hd + 1) * 2 * LANES]
        k = ka_ref[:, hd * 2 * LANES:(hd + 1) * 2 * LANES]
        v = va_ref[:, hd * LANES:(hd + 1) * LANES]
        o_ref[:, hd * LANES:(hd + 1) * LANES] = _softmax_pv([_dot_nt(q, k)], [v], base2=True).astype(BF16)
    grp = SWA_H // SWA_KV_H
    scale = SWA_HD ** -0.5
    for hd in range(SWA_H):
        q = (col("swa_q", hd).astype(F32) * scale).astype(BF16)
        o = _softmax_pv([_dot_nt(q, col("swa_k", hd // grp))], [col("swa_v", hd // grp)], sink=sink_ref[hd])
        o_ref[:, BRANCH_W + hd * LANES:BRANCH_W + (hd + 1) * LANES] = o.astype(BF16)
    scale = NA_HD ** -0.5
    for hd in range(NA_H):
        q = (col("na_q", hd).astype(F32) * scale).astype(BF16)
        o = _softmax_pv([_dot_nt(q, col("na_k", hd))], [col("na_v", hd)])
        o_ref[:, 2 * BRANCH_W + hd * LANES:2 * BRANCH_W + (hd + 1) * LANES] = o.astype(BF16)
    scale = DIFF_DK ** -0.5
    lam = _diff_lambda(lam_ref, lam_init)
    for hd in range(DIFF_H):
        q = col("diff_q", hd).astype(F32) * scale
        k = col("diff_k", hd)
        v = col("diff_v", hd)
        lane = lax.broadcasted_iota(jnp.int32, q.shape, 1)
        outs = []
        for j in range(2):
            qj = jnp.where((lane >= j * DIFF_DK) & (lane < (j + 1) * DIFF_DK), q, 0.0).astype(BF16)
            outs.append(_softmax_pv([_dot_nt(qj, k)], [v]))
        o = _diff_finish(outs[0], outs[1], lam, g_ref[...], lam_init)
        o_ref[:, 3 * BRANCH_W + hd * LANES:3 * BRANCH_W + (hd + 1) * LANES] = o.astype(BF16)


def _ctx_attention(u, cols, qa, ka, va, sink, diff_lambda_l, subln_g, lam_init, o, dims):
    n_lat, seq, n_batch = dims
    n_ctx = (u.shape[0] - n_lat) // n_batch
    blk0 = n_lat // n_ctx
    kern = functools.partial(_ctx_attn_kernel, cols=cols, lam_init=lam_init)
    return pl.pallas_call(
        kern, grid=(n_batch,),
        in_specs=[pl.BlockSpec((n_ctx, u.shape[1]), lambda b: (blk0 + b, 0)),
                  pl.BlockSpec((n_ctx, qa.shape[1]), lambda b: (blk0 + b, 0)),
                  pl.BlockSpec((n_ctx, ka.shape[1]), lambda b: (blk0 + b, 0)),
                  pl.BlockSpec((n_ctx, va.shape[1]), lambda b: (blk0 + b, 0)),
                  pl.BlockSpec(memory_space=pltpu.SMEM),
                  pl.BlockSpec((4, DIFF_DK), lambda b: (0, 0)),
                  pl.BlockSpec((1, DIFF_DV), lambda b: (0, 0)),
                  pl.BlockSpec(memory_space=pl.ANY)],
        out_specs=pl.BlockSpec((n_ctx, o.shape[1]), lambda b: (blk0 + b, 0)),
        out_shape=jax.ShapeDtypeStruct(o.shape, o.dtype),
        input_output_aliases={7: 0},
        compiler_params=_params(1), name="ctx_attention",
    )(u, qa, ka, va, sink, diff_lambda_l, subln_g.reshape(1, -1), o)


MOE_TM = 512
GATHER_ROWS = 256


def _route_plan(top_i, tm):
    t = top_i.shape[0]
    n_assign = t * TOP_K
    n_tiles = (n_assign + N_EXPERTS * (tm - 1)) // tm
    e_flat = top_i.reshape(-1)
    onehot = (e_flat[:, None] == jnp.arange(N_EXPERTS)[None, :]).astype(jnp.int32)
    csum = jnp.cumsum(onehot, axis=0)
    rank = jnp.sum(onehot * csum, axis=1) - 1
    counts = csum[-1]
    padded = ((counts + tm - 1) // tm) * tm
    ends = jnp.cumsum(padded)
    offs = ends - padded
    pos = (jnp.sum(onehot * offs[None, :], axis=1) + rank).astype(jnp.int32)
    src = jnp.zeros((n_tiles * tm,), jnp.int32).at[pos].set(jnp.arange(n_assign, dtype=jnp.int32) // TOP_K)
    tile_start = jnp.arange(n_tiles, dtype=jnp.int32) * tm
    tile_expert = jnp.minimum(jnp.sum((tile_start[:, None] >= ends[None, :]).astype(jnp.int32), axis=1),
                              N_EXPERTS - 1).astype(jnp.int32)
    tile_valid = (tile_start < ends[-1]).astype(jnp.int32)
    return pos, src, tile_expert, tile_valid


def _row_copies(idx_ref, base, src_hbm, bufs, sems, slot, n_rows, idx_stride):
    def body(r, carry):
        for s, buf in enumerate(bufs):
            row = idx_ref[(base + r) * idx_stride + s]
            pltpu.make_async_copy(src_hbm.at[pl.ds(row, 1), :], buf.at[slot, pl.ds(r, 1), :],
                                  sems.at[slot, s]).start()
        return carry
    lax.fori_loop(0, n_rows, body, 0, unroll=8)


def _wait_rows(src_hbm, bufs, sems, slot, n_rows):
    for s, buf in enumerate(bufs):
        pltpu.make_async_copy(src_hbm.at[pl.ds(0, n_rows), :], buf.at[slot], sems.at[slot, s]).wait()


def _gather_kernel(src_ref, tv_ref, x_hbm, o_ref, x_ref, buf_ref, sem, *, rows):
    i = pl.program_id(0)

    @pl.when(i == 0)
    def _():
        cp = pltpu.make_async_copy(x_hbm, x_ref, sem)
        cp.start()
        cp.wait()

    @pl.when(tv_ref[i] == 1)
    def _():
        def body(r, carry):
            buf_ref[pl.ds(r, 1), :] = x_ref[pl.ds(src_ref[i * rows + r], 1), :]
            return carry
        lax.fori_loop(0, rows, body, 0, unroll=8)
        w = buf_ref[...]
        half = w.shape[1]
        o_ref[:, :half] = lax.bitcast_convert_type(w << 16, F32).astype(BF16)
        o_ref[:, half:] = lax.bitcast_convert_type(w & jnp.uint32(0xFFFF0000), F32).astype(BF16)

    @pl.when(tv_ref[i] == 0)
    def _():
        o_ref[...] = jnp.zeros(o_ref.shape, BF16)


def _gather_rows(xp, src, tile_valid, rows):
    t, half = xp.shape
    n_rows = src.shape[0]
    kern = functools.partial(_gather_kernel, rows=rows)
    return pl.pallas_call(
        kern,
        grid_spec=pltpu.PrefetchScalarGridSpec(
            num_scalar_prefetch=2, grid=(n_rows // rows,),
            in_specs=[pl.BlockSpec(memory_space=pl.ANY)],
            out_specs=pl.BlockSpec((rows, 2 * half), lambda i, s, v: (i, 0)),
            scratch_shapes=[pltpu.VMEM((t, half), jnp.uint32), pltpu.VMEM((rows, half), jnp.uint32),
                            pltpu.SemaphoreType.DMA(())]),
        out_shape=jax.ShapeDtypeStruct((n_rows, 2 * half), BF16),
        compiler_params=_params(1), name="moe_gather",
    )(src, tile_valid, xp)


def _expert_changed(te_ref, i):
    return jnp.logical_or(i == 0, te_ref[i] != te_ref[jnp.maximum(i - 1, 0)])


def _gmm_up_kernel(te_ref, tv_ref, x_ref, w1_ref, w3_ref, o_ref, w1b_ref, w3b_ref):
    i = pl.program_id(1)
    _cast_weights(_expert_changed(te_ref, i), [w1_ref, w3_ref], [w1b_ref, w3b_ref])

    @pl.when(tv_ref[i] == 1)
    def _():
        x = x_ref[...]
        o_ref[...] = (_silu(_dot(x, w1b_ref[...])) * _dot(x, w3b_ref[...])).astype(BF16)

    @pl.when(tv_ref[i] == 0)
    def _():
        o_ref[...] = jnp.zeros(o_ref.shape, BF16)


def _gmm_up(xs, w1, w3, idx, tile_expert, tile_valid, tm):
    r, d = xs.shape
    f = w1.shape[3]
    tn = _tile(f, 512, LANES)
    return pl.pallas_call(
        _gmm_up_kernel,
        grid_spec=pltpu.PrefetchScalarGridSpec(
            num_scalar_prefetch=2, grid=(f // tn, r // tm),
            in_specs=[pl.BlockSpec((tm, d), lambda j, i, te, tv: (i, 0)),
                      pl.BlockSpec((None, None, d, tn), lambda j, i, te, tv: (idx, te[i], 0, j)),
                      pl.BlockSpec((None, None, d, tn), lambda j, i, te, tv: (idx, te[i], 0, j))],
            out_specs=pl.BlockSpec((tm, tn), lambda j, i, te, tv: (i, j)),
            scratch_shapes=[pltpu.VMEM((d, tn), BF16), pltpu.VMEM((d, tn), BF16)]),
        out_shape=jax.ShapeDtypeStruct((r, f), BF16),
        compiler_params=_params(2), name="moe_up",
    )(tile_expert, tile_valid, xs, w1, w3)


def _gmm_down_kernel(te_ref, tv_ref, a_ref, w_ref, o_ref, wb_ref):
    i = pl.program_id(1)
    _cast_weights(_expert_changed(te_ref, i), [w_ref], [wb_ref])

    @pl.when(tv_ref[i] == 1)
    def _():
        o_ref[...] = _dot(a_ref[...], wb_ref[...])

    @pl.when(tv_ref[i] == 0)
    def _():
        o_ref[...] = jnp.zeros(o_ref.shape, F32)


def _gmm_down(a, w2, idx, tile_expert, tile_valid, tm):
    r, f = a.shape
    d = w2.shape[3]
    tn = _tile(d, 512, LANES)
    return pl.pallas_call(
        _gmm_down_kernel,
        grid_spec=pltpu.PrefetchScalarGridSpec(
            num_scalar_prefetch=2, grid=(d // tn, r // tm),
            in_specs=[pl.BlockSpec((tm, f), lambda j, i, te, tv: (i, 0)),
                      pl.BlockSpec((None, None, f, tn), lambda j, i, te, tv: (idx, te[i], 0, j))],
            out_specs=pl.BlockSpec((tm, tn), lambda j, i, te, tv: (i, j)),
            scratch_shapes=[pltpu.VMEM((f, tn), BF16)]),
        out_shape=jax.ShapeDtypeStruct((r, d), F32),
        compiler_params=_params(2), name="moe_down",
    )(tile_expert, tile_valid, a, w2)


def _combine_kernel(pos_ref, y_hbm, h_ref, tg_ref, gate_ref, o_ref, buf0, buf1, sems, *, rows, n_lat, seq, n_batch):
    i = pl.program_id(0)
    n = pl.num_programs(0)
    slot = i % 2
    bufs = [buf0, buf1]

    @pl.when(i == 0)
    def _():
        _row_copies(pos_ref, 0, y_hbm, bufs, sems, 0, rows, TOP_K)

    @pl.when(i + 1 < n)
    def _():
        _row_copies(pos_ref, (i + 1) * rows, y_hbm, bufs, sems, 1 - slot, rows, TOP_K)

    _wait_rows(y_hbm, bufs, sems, slot, rows)
    grp = _group_of_tile(i, rows, n_lat, seq, n_batch)
    gate = gate_ref[pl.ds(grp, 1), :]
    tg = tg_ref[...]
    mix = tg[:, 0:1] * buf0[slot] + tg[:, 1:2] * buf1[slot]
    o_ref[...] = h_ref[...] + gate * mix


def _combine(h, y, pos, top_g, mod_l, gate_chunk, dims):
    n_lat, seq, n_batch = dims
    t, d = h.shape
    rows = _tile(math.gcd(seq, t - n_lat), GATHER_ROWS, 8)
    kern = functools.partial(_combine_kernel, rows=rows, n_lat=n_lat, seq=seq, n_batch=n_batch)
    return pl.pallas_call(
        kern,
        grid_spec=pltpu.PrefetchScalarGridSpec(
            num_scalar_prefetch=1, grid=(t // rows,),
            in_specs=[pl.BlockSpec(memory_space=pl.ANY),
                      pl.BlockSpec((rows, d), lambda i, p: (i, 0)),
                      pl.BlockSpec((rows, LANES), lambda i, p: (i, 0)),
                      pl.BlockSpec((8, d), lambda i, p: (0, gate_chunk))],
            out_specs=pl.BlockSpec((rows, d), lambda i, p: (i, 0)),
            scratch_shapes=[pltpu.VMEM((2, rows, d), F32), pltpu.VMEM((2, rows, d), F32),
                            pltpu.SemaphoreType.DMA((2, 2))]),
        out_shape=jax.ShapeDtypeStruct(h.shape, F32),
        input_output_aliases={2: 0},
        compiler_params=_params(1), name="moe_combine",
    )(pos, y, h, top_g, mod_l)


def _in_columns(d_model):
    order = [("gates", N_BRANCH * d_model), ("cq", 512), ("ckv", 256), ("krope", LANES),
             ("swa_q", SWA_H * SWA_HD), ("swa_k", SWA_KV_H * SWA_HD), ("swa_v", SWA_KV_H * SWA_HD),
             ("na_q", NA_H * NA_HD), ("na_k", NA_H * NA_HD), ("na_v", NA_H * NA_HD),
             ("diff_q", DIFF_H * 2 * DIFF_DK), ("diff_k", DIFF_H * 2 * DIFF_DK), ("diff_v", DIFF_H * DIFF_DV)]
    cols, off = {}, 0
    for name, width in order:
        cols[name] = off
        off += width
    return cols, off


def _permute_w_in(w_in, d_model):
    sizes = [512, 256, MLA_ROPE, 512, 256, 256, 512, 512, 512, 512, 512, 512, N_BRANCH * d_model]
    names = ["cq", "ckv", "krope", "swa_q", "swa_k", "swa_v", "na_q", "na_k", "na_v",
             "diff_q", "diff_k", "diff_v", "gates"]
    starts = np.concatenate([[0], np.cumsum(sizes)])
    piece = {n: w_in[:, :, int(starts[i]):int(starts[i + 1])].astype(BF16) for i, n in enumerate(names)}
    piece["krope"] = jnp.pad(piece["krope"], ((0, 0), (0, 0), (0, LANES - MLA_ROPE)))
    cols, width = _in_columns(d_model)
    total = -(-width // 1024) * 1024
    parts = [piece[n] for n in cols]
    if total > width:
        parts.append(jnp.zeros(w_in.shape[:2] + (total - width,), BF16))
    return jnp.concatenate(parts, axis=2), cols


def kernel(x, c, ctx, c_ctx, mod_w, mod_b, norm1_g, norm2_g, w_in, mla_q_norm, mla_kv_norm, mla_w_uq, mla_w_ukv,
           swa_sink, na_rpb, diff_lambda, diff_subln_g, w_branch, w_out, ffn_w1, ffn_w3, ffn_w2, moe_router,
           moe_w1, moe_w3, moe_w2, final_norm_g):
    n_batch, seq, d = x.shape
    n_ctx = ctx.shape[1]
    depth = mod_w.shape[0]
    n_lat = n_batch * seq
    dims = (n_lat, seq, n_batch)

    h = jnp.concatenate([x.reshape(n_lat, d), ctx.reshape(n_batch * n_ctx, d)], axis=0)
    t = h.shape[0]
    c8 = jnp.concatenate([c, c_ctx[None, :], jnp.zeros((8 - n_batch - 1, d), F32)], axis=0)
    mod = _mod_all(c8, mod_w, mod_b)

    w_in_p, cols = _permute_w_in(w_in, d)
    w_uq = mla_w_uq.reshape(depth, -1, MLA_H, MLA_NOPE + MLA_ROPE)
    w_uq_p = jnp.pad(w_uq, ((0, 0), (0, 0), (0, 0), (0, 2 * LANES - MLA_NOPE - MLA_ROPE)))
    w_uq_p = w_uq_p.reshape(depth, -1, MLA_H * 2 * LANES)
    rope64 = _rope_tables(seq, MLA_ROPE)
    rope128 = _rope_tables(seq, SWA_HD)
    swa_plan = _swa_bias(seq)
    router_p = jnp.pad(moe_router, ((0, 0), (0, 0), (0, LANES - N_EXPERTS)))

    for l in range(depth):
        need_ctx = l < depth - 1
        lam_init = 0.8 - 0.6 * math.exp(-0.3 * l)
        mod_l = mod[l]

        xn = _norm_mod(h, norm1_g[l], mod_l, 0, 1, dims)
        u = _project_in(xn, w_in_p, l)

        qa, ka, va = _mla_prep(u, cols, mla_q_norm[l], mla_kv_norm[l], w_uq_p[l], mla_w_ukv[l], rope64, dims)
        o = jnp.zeros((t, N_BRANCH * BRANCH_W), BF16)
        o = _mla_attention(qa, ka, va, o, dims, 0)
        o = _local_attention(u, cols["swa_q"], cols["swa_k"], cols["swa_v"], SWA_H, SWA_KV_H, swa_plan,
                             rope128, swa_sink[l], o, dims, 1, "swa_attention")
        o = _local_attention(u, cols["na_q"], cols["na_k"], cols["na_v"], NA_H, NA_H, _na_bias(na_rpb[l], seq),
                             None, None, o, dims, 2, "na_attention")
        o = _diff_attention(u, cols, rope64, diff_lambda[l], diff_subln_g[l], lam_init, o, dims, 3)
        if need_ctx:
            o = _ctx_attention(u, cols, qa, ka, va, swa_sink[l], diff_lambda[l], diff_subln_g[l], lam_init, o, dims)

        merged = _merge(o, u, w_branch, l)
        h = _project_residual(merged, w_out, l, h, mod_l, 2, dims)

        i = l // 2
        if l % 2 == 0:
            xn = _norm_mod(h, norm2_g[l], mod_l, 3, 4, dims)
            a = _swiglu_up(xn, ffn_w1, ffn_w3, i)
            h = _project_residual(a, ffn_w2, i, h, mod_l, 5, dims)
        else:
            xn, top_g, top_i = _norm_mod(h, norm2_g[l], mod_l, 3, 4, dims, router=router_p[i])
            pos, src, tile_expert, tile_valid = _route_plan(top_i[:, :TOP_K], MOE_TM)
            xs = _gather_rows(xn, src, tile_valid, MOE_TM)
            a = _gmm_up(xs, moe_w1, moe_w3, i, tile_expert, tile_valid, MOE_TM)
            y = _gmm_down(a, moe_w2, i, tile_expert, tile_valid, MOE_TM)
            h = _combine(h, y, pos, top_g, mod_l, 5, dims)

    out = _final_norm(h, final_norm_g, n_lat)
    return out.reshape(n_batch, seq, d)
```

```python
import functools
import math

import jax
import jax.numpy as jnp
import numpy as np
from jax import lax
from jax.experimental import pallas as pl
from jax.experimental.pallas import tpu as pltpu

F32 = jnp.float32
BF16 = jnp.bfloat16

GRID_W = 64
ROPE_THETA = 10000.0
NORM_EPS = 1e-6
NEG_INF = -1e30
LOG2E = math.log2(math.e)

MLA_H, MLA_NOPE, MLA_ROPE, MLA_V = 4, 128, 64, 128
SWA_H, SWA_KV_H, SWA_HD, SWA_WINDOW = 4, 2, 128, 128
NA_H, NA_HD, NA_KH, NA_KW = 4, 128, 8, 16
DIFF_H, DIFF_DK, DIFF_DV = 4, 64, 128
N_BRANCH, BRANCH_W = 4, 512
N_EXPERTS, TOP_K = 8, 2
LANES = 128

VMEM_LIMIT = 56 * 1024 * 1024


def _params(n_axes):
    return pltpu.CompilerParams(dimension_semantics=("arbitrary",) * n_axes, vmem_limit_bytes=VMEM_LIMIT)


def _tile(n, pref, mult):
    if n <= pref:
        return n
    t = (pref // mult) * mult
    while t > mult and n % t:
        t -= mult
    assert n % t == 0, (n, pref, mult)
    return t


def _dot(a, b):
    return jnp.dot(a, b, preferred_element_type=F32)


def _dot_nt(a, b):
    return lax.dot_general(a, b, (((1,), (1,)), ((), ())), preferred_element_type=F32)


def _silu(x):
    return x * (1.0 / (1.0 + jnp.exp(-x)))


def _sigmoid(x):
    return 0.5 * (1.0 + jnp.tanh(0.5 * x))


def _mod_kernel(c_ref, w_ref, b_ref, o_ref):
    a = _silu(c_ref[...]).astype(BF16)
    o_ref[...] = _dot(a, w_ref[...].astype(BF16)) + b_ref[...]


def _mod_all(c8, mod_w, mod_b):
    n_l, d, n = mod_w.shape
    tn = _tile(n, 1024, LANES)
    return pl.pallas_call(
        _mod_kernel,
        grid=(n_l, n // tn),
        in_specs=[
            pl.BlockSpec((8, d), lambda l, j: (0, 0)),
            pl.BlockSpec((None, d, tn), lambda l, j: (l, 0, j)),
            pl.BlockSpec((None, 1, tn), lambda l, j: (l, 0, j)),
        ],
        out_specs=pl.BlockSpec((None, 8, tn), lambda l, j: (l, 0, j)),
        out_shape=jax.ShapeDtypeStruct((n_l, 8, n), F32),
        compiler_params=_params(2),
        name="mod_all",
    )(c8, mod_w, mod_b.reshape(n_l, 1, n))


def _group_of_tile(i, tm, n_lat, seq, n_batch):
    row0 = i * tm
    return jnp.where(row0 < n_lat, row0 // seq, n_batch)


def _norm_kernel(h_ref, g_ref, sh_ref, sc_ref, *rest, tm, n_lat, seq, n_batch, route):
    grp = _group_of_tile(pl.program_id(0), tm, n_lat, seq, n_batch)
    x = h_ref[...]
    y = x * lax.rsqrt(jnp.mean(x * x, axis=-1, keepdims=True) + NORM_EPS) * g_ref[...]
    sh = sh_ref[pl.ds(grp, 1), :]
    sc = sc_ref[pl.ds(grp, 1), :]
    xn = y * (1.0 + sc) + sh
    if not route:
        (o_ref,) = rest
        o_ref[...] = xn.astype(BF16)
        return
    r_ref, o_ref, tg_ref, ti_ref = rest
    bits = lax.bitcast_convert_type(xn.astype(BF16).astype(F32), jnp.uint32)
    half = bits.shape[1] // 2
    o_ref[...] = (bits[:, half:] & jnp.uint32(0xFFFF0000)) | (bits[:, :half] >> 16)
    logits = jnp.dot(xn, r_ref[...], preferred_element_type=F32, precision=lax.Precision.HIGHEST)
    lane = lax.broadcasted_iota(jnp.int32, logits.shape, 1).astype(F32)
    logits = jnp.where(lane < N_EXPERTS, logits, -jnp.inf)
    m1 = jnp.max(logits, axis=-1, keepdims=True)
    i1 = jnp.min(jnp.where(logits == m1, lane, float(LANES)), axis=-1, keepdims=True)
    rest_l = jnp.where(lane == i1, -jnp.inf, logits)
    m2 = jnp.max(rest_l, axis=-1, keepdims=True)
    i2 = jnp.min(jnp.where(rest_l == m2, lane, float(LANES)), axis=-1, keepdims=True)
    e2 = jnp.exp(m2 - m1)
    den = 1.0 + e2
    tg_ref[...] = jnp.where(lane == 0.0, 1.0 / den, jnp.where(lane == 1.0, e2 / den, 0.0))
    ti_ref[...] = jnp.where(lane == 0.0, i1, jnp.where(lane == 1.0, i2, 0.0)).astype(jnp.int32)


def _norm_mod(h, g, mod_l, shift_chunk, scale_chunk, dims, router=None):
    n_lat, seq, n_batch = dims
    t, d = h.shape
    tm = _tile(math.gcd(seq, t - n_lat), 512, 8)
    route = router is not None
    kern = functools.partial(_norm_kernel, tm=tm, n_lat=n_lat, seq=seq, n_batch=n_batch, route=route)
    in_specs = [
        pl.BlockSpec((tm, d), lambda i: (i, 0)),
        pl.BlockSpec((1, d), lambda i: (0, 0)),
        pl.BlockSpec((8, d), lambda i: (0, shift_chunk)),
        pl.BlockSpec((8, d), lambda i: (0, scale_chunk)),
    ]
    args = [h, g.reshape(1, d), mod_l, mod_l]
    if route:
        in_specs.append(pl.BlockSpec((d, LANES), lambda i: (0, 0)))
        args.append(router)
        out_specs = [pl.BlockSpec((tm, d // 2), lambda i: (i, 0)),
                     pl.BlockSpec((tm, LANES), lambda i: (i, 0)),
                     pl.BlockSpec((tm, LANES), lambda i: (i, 0))]
        out_shape = [jax.ShapeDtypeStruct((t, d // 2), jnp.uint32),
                     jax.ShapeDtypeStruct((t, LANES), F32),
                     jax.ShapeDtypeStruct((t, LANES), jnp.int32)]
    else:
        out_specs = pl.BlockSpec((tm, d), lambda i: (i, 0))
        out_shape = jax.ShapeDtypeStruct((t, d), BF16)
    return pl.pallas_call(
        kern, grid=(t // tm,), in_specs=in_specs, out_specs=out_specs, out_shape=out_shape,
        compiler_params=_params(1), name="norm_route" if route else "norm_mod",
    )(*args)


def _final_norm_kernel(h_ref, g_ref, o_ref):
    x = h_ref[...]
    o_ref[...] = x * lax.rsqrt(jnp.mean(x * x, axis=-1, keepdims=True) + NORM_EPS) * g_ref[...]


def _final_norm(h, g, n_rows):
    d = h.shape[1]
    tm = _tile(n_rows, 256, 8)
    return pl.pallas_call(
        _final_norm_kernel, grid=(n_rows // tm,),
        in_specs=[pl.BlockSpec((tm, d), lambda i: (i, 0)), pl.BlockSpec((1, d), lambda i: (0, 0))],
        out_specs=pl.BlockSpec((tm, d), lambda i: (i, 0)),
        out_shape=jax.ShapeDtypeStruct((n_rows, d), F32),
        compiler_params=_params(1), name="final_norm",
    )(h, g.reshape(1, d))


def _cast_weights(first, w_refs, wb_refs):
    @pl.when(first)
    def _():
        for w_ref, wb_ref in zip(w_refs, wb_refs):
            wb_ref[...] = w_ref[...].astype(BF16)


IN_TN = 1024
IN_PREV = 256
IN_ROWS = 256


def _in_layout(d_model):
    pieces = [("cq", 512, 512), ("ckv", 256, 256), ("krope", MLA_ROPE, LANES),
              ("swa_q", SWA_H * SWA_HD, LANES), ("swa_k", SWA_KV_H * SWA_HD, LANES),
              ("swa_v", SWA_KV_H * SWA_HD, LANES), ("na_q", NA_H * NA_HD, LANES), ("na_k", NA_H * NA_HD, LANES),
              ("na_v", NA_H * NA_HD, LANES), ("diff_q", DIFF_H * 2 * DIFF_DK, LANES),
              ("diff_k", DIFF_H * 2 * DIFF_DK, LANES), ("diff_v", DIFF_H * DIFF_DV, LANES),
              ("gates", N_BRANCH * d_model, 512)]
    cols, runs, nat, u = {}, [], 0, 0
    for name, width, align in pieces:
        u = -(-u // align) * align
        cols[name] = u
        if runs and runs[-1][0] + runs[-1][2] == u and runs[-1][1] + runs[-1][2] == nat:
            runs[-1][2] += width
        else:
            runs.append([u, nat, width])
        nat += width
        u += width
    width_u = -(-u // IN_TN) * IN_TN
    plans = []
    for j in range(width_u // IN_TN):
        plan = []
        for u0, n0, w in runs:
            lo, hi = max(u0, j * IN_TN), min(u0 + w, (j + 1) * IN_TN)
            if lo < hi:
                assert 0 <= u0 - n0 <= IN_PREV
                plan.append((lo - j * IN_TN, hi - j * IN_TN, u0 - n0))
        plans.append(tuple(plan))
    return cols, width_u, tuple(plans)


def _relayout_weights(prev_ref, cur_ref, wb_ref, plan):
    d, tn = cur_ref.shape
    lane = lax.broadcasted_iota(jnp.int32, (IN_ROWS, tn), 1)

    def body(r, carry):
        rows = pl.ds(pl.multiple_of(r * IN_ROWS, IN_ROWS), IN_ROWS)
        cur = cur_ref[rows, :]
        out = jnp.zeros((IN_ROWS, tn), F32)
        cat = None
        for shift in sorted({s for _, _, s in plan}):
            if shift == 0:
                src = cur
            else:
                if cat is None:
                    cat = jnp.concatenate([prev_ref[rows, :], cur], axis=1)
                src = pltpu.roll(cat, tn + shift, 1)[:, :tn]
            mask = None
            for lo, hi, s in plan:
                if s == shift:
                    m = (lane >= lo) & (lane < hi)
                    mask = m if mask is None else mask | m
            out = jnp.where(mask, src, out)
        wb_ref[rows, :] = out.astype(BF16)
        return carry

    lax.fori_loop(0, d // IN_ROWS, body, 0)


def _proj_in_kernel(x_ref, prev_ref, cur_ref, o_ref, wb_ref, *, plans):
    j = pl.program_id(0)
    first = pl.program_id(1) == 0
    for plan in sorted(set(plans)):
        member = None
        for jj, p in enumerate(plans):
            if p == plan:
                member = (j == jj) if member is None else member | (j == jj)

        @pl.when(first & member)
        def _(plan=plan):
            _relayout_weights(prev_ref, cur_ref, wb_ref, plan)

    o_ref[...] = _dot(x_ref[...], wb_ref[...]).astype(o_ref.dtype)


def _project_in(xn, w_in, layer, width_u, plans):
    t, d = xn.shape
    assert d % IN_ROWS == 0
    tm = _tile(t, 512, 8)
    prev_per_tile = IN_TN // IN_PREV
    return pl.pallas_call(
        functools.partial(_proj_in_kernel, plans=plans), grid=(width_u // IN_TN, t // tm),
        in_specs=[pl.BlockSpec((tm, d), lambda j, i: (i, 0)),
                  pl.BlockSpec((None, d, IN_PREV), lambda j, i: (layer, 0, jnp.maximum(j * prev_per_tile - 1, 0))),
                  pl.BlockSpec((None, d, IN_TN), lambda j, i: (layer, 0, j))],
        out_specs=pl.BlockSpec((tm, IN_TN), lambda j, i: (i, j)),
        out_shape=jax.ShapeDtypeStruct((t, width_u), BF16),
        scratch_shapes=[pltpu.VMEM((d, IN_TN), BF16)],
        compiler_params=_params(2), name="project_in",
    )(xn, w_in, w_in)


def _proj_res_kernel(x_ref, w_ref, h_ref, gate_ref, o_ref, wb_ref, *, tm, n_lat, seq, n_batch):
    i = pl.program_id(1)
    _cast_weights(i == 0, [w_ref], [wb_ref])
    grp = _group_of_tile(i, tm, n_lat, seq, n_batch)
    gate = gate_ref[pl.ds(grp, 1), :]
    o_ref[...] = h_ref[...] + gate * _dot(x_ref[...], wb_ref[...])


def _project_residual(x, w_l, layer, h, mod_l, gate_chunk, dims):
    n_lat, seq, n_batch = dims
    t, k = x.shape
    n = w_l.shape[2]
    tm = _tile(math.gcd(seq, t - n_lat), 512, 8)
    tn = _tile(n, 512, LANES)
    kern = functools.partial(_proj_res_kernel, tm=tm, n_lat=n_lat, seq=seq, n_batch=n_batch)
    gate_blk = gate_chunk * (n // tn)
    return pl.pallas_call(
        kern, grid=(n // tn, t // tm),
        in_specs=[pl.BlockSpec((tm, k), lambda j, i: (i, 0)),
                  pl.BlockSpec((None, k, tn), lambda j, i: (layer, 0, j)),
                  pl.BlockSpec((tm, tn), lambda j, i: (i, j)),
                  pl.BlockSpec((8, tn), lambda j, i: (0, gate_blk + j))],
        out_specs=pl.BlockSpec((tm, tn), lambda j, i: (i, j)),
        out_shape=jax.ShapeDtypeStruct(h.shape, F32),
        scratch_shapes=[pltpu.VMEM((k, tn), BF16)],
        input_output_aliases={2: 0},
        compiler_params=_params(2), name="project_residual",
    )(x, w_l, h, mod_l)


def _swiglu_up_kernel(x_ref, w1_ref, w3_ref, o_ref, w1b_ref, w3b_ref):
    _cast_weights(pl.program_id(1) == 0, [w1_ref, w3_ref], [w1b_ref, w3b_ref])
    x = x_ref[...]
    a = _dot(x, w1b_ref[...])
    b = _dot(x, w3b_ref[...])
    o_ref[...] = (_silu(a) * b).astype(BF16)


def _swiglu_up(xn, w1, w3, idx):
    t, d = xn.shape
    f = w1.shape[2]
    tm = _tile(t, 1024, 8) if t % 1024 == 0 else _tile(t, 512, 8)
    tn = _tile(f, 512, LANES)
    return pl.pallas_call(
        _swiglu_up_kernel, grid=(f // tn, t // tm),
        in_specs=[pl.BlockSpec((tm, d), lambda j, i: (i, 0)),
                  pl.BlockSpec((None, d, tn), lambda j, i: (idx, 0, j)),
                  pl.BlockSpec((None, d, tn), lambda j, i: (idx, 0, j))],
        out_specs=pl.BlockSpec((tm, tn), lambda j, i: (i, j)),
        out_shape=jax.ShapeDtypeStruct((t, f), BF16),
        scratch_shapes=[pltpu.VMEM((d, tn), BF16), pltpu.VMEM((d, tn), BF16)],
        compiler_params=_params(2), name="swiglu_up",
    )(xn, w1, w3)


def _merge_kernel(o_ref, g0_ref, g1_ref, g2_ref, g3_ref, w_ref, out_ref, wb_ref):
    _cast_weights(pl.program_id(1) == 0, [w_ref], [wb_ref])
    acc = None
    for m, g_ref in enumerate((g0_ref, g1_ref, g2_ref, g3_ref)):
        y = _sigmoid(g_ref[...].astype(F32)) * _dot(o_ref[:, m * BRANCH_W:(m + 1) * BRANCH_W], wb_ref[m])
        acc = y if acc is None else acc + y
    out_ref[...] = acc.astype(BF16)


def _merge(o, u, gate_col, w_branch, layer):
    t = o.shape[0]
    d = w_branch.shape[3]
    tm = _tile(t, 512, 8)
    tn = _tile(d, 512, LANES)
    nj = d // tn
    assert gate_col % tn == 0
    gate_blk = gate_col // tn

    def gate_spec(m):
        return pl.BlockSpec((tm, tn), lambda j, i: (i, gate_blk + m * nj + j))

    return pl.pallas_call(
        _merge_kernel, grid=(nj, t // tm),
        in_specs=[pl.BlockSpec((tm, N_BRANCH * BRANCH_W), lambda j, i: (i, 0)),
                  gate_spec(0), gate_spec(1), gate_spec(2), gate_spec(3),
                  pl.BlockSpec((None, N_BRANCH, BRANCH_W, tn), lambda j, i: (layer, 0, 0, j))],
        out_specs=pl.BlockSpec((tm, tn), lambda j, i: (i, j)),
        out_shape=jax.ShapeDtypeStruct((t, d), BF16),
        scratch_shapes=[pltpu.VMEM((N_BRANCH, BRANCH_W, tn), BF16)],
        compiler_params=_params(2), name="merge",
    )(o, u, u, u, u, w_branch)


def _rope_tables(n, dim):
    pos = jnp.arange(n, dtype=jnp.int32)
    row = (pos // GRID_W).astype(F32)
    col = (pos % GRID_W).astype(F32)
    quarter = dim // 4
    inv_freq = ROPE_THETA ** (-jnp.arange(quarter, dtype=F32) / quarter)
    ang_r = row[:, None] * inv_freq
    ang_c = col[:, None] * inv_freq
    ang = jnp.concatenate([ang_r, ang_r, ang_c, ang_c], axis=-1)
    reps = LANES // dim
    cos = jnp.tile(jnp.cos(ang), (1, reps))
    sin = jnp.tile(jnp.sin(ang), (1, reps))
    lane = jnp.arange(LANES)[None, :]
    first = (lane & quarter) == 0
    return cos, jnp.where(first, -sin, 0.0), jnp.where(first, 0.0, sin)


def _rope(x, cos, sin_a, sin_b, quarter):
    return (x * cos + pltpu.roll(x, LANES - quarter, 1) * sin_a + pltpu.roll(x, quarter, 1) * sin_b)


def _mla_prep_kernel(cq_ref, ckv_ref, kr_ref, qg_ref, kvg_ref, wq_ref, wkv_ref, cos_ref, sa_ref, sb_ref,
                     q_ref, k_ref, v_ref, *, n_lat_tiles):
    is_lat = pl.program_id(0) < n_lat_tiles
    cos = jnp.where(is_lat, cos_ref[...], 1.0)
    sa = jnp.where(is_lat, sa_ref[...], 0.0)
    sb = jnp.where(is_lat, sb_ref[...], 0.0)
    quarter = MLA_ROPE // 4

    def rms(x_ref, g_ref):
        x = x_ref[...].astype(F32)
        return (x * lax.rsqrt(jnp.mean(x * x, axis=-1, keepdims=True) + NORM_EPS) * g_ref[...]).astype(BF16)

    scale = (MLA_NOPE + MLA_ROPE) ** -0.5 * LOG2E
    q = _dot(rms(cq_ref, qg_ref), wq_ref[...].astype(BF16)) * scale
    kv = _dot(rms(ckv_ref, kvg_ref), wkv_ref[...].astype(BF16))
    kr = _rope(kr_ref[...].astype(F32), cos, sa, sb, quarter).astype(BF16)
    for hd in range(MLA_H):
        b = hd * 2 * LANES
        q_ref[:, b:b + LANES] = q[:, b:b + LANES].astype(BF16)
        q_ref[:, b + LANES:b + 2 * LANES] = _rope(q[:, b + LANES:b + 2 * LANES], cos, sa, sb, quarter).astype(BF16)
        k_ref[:, b:b + LANES] = kv[:, b:b + LANES].astype(BF16)
        k_ref[:, b + LANES:b + 2 * LANES] = kr
        v_ref[:, hd * LANES:(hd + 1) * LANES] = kv[:, b + LANES:b + 2 * LANES].astype(BF16)


def _mla_prep(u, cols, q_norm, kv_norm, w_uq_p, w_ukv, rope64, dims):
    n_lat, seq, _ = dims
    t = u.shape[0]
    tm = _tile(math.gcd(seq, t - n_lat), 256, 16)
    n_lat_tiles = n_lat // tm
    seq_tiles = seq // tm
    qw = MLA_H * 2 * LANES

    def tab_spec():
        return pl.BlockSpec((tm, LANES), lambda i: (jnp.where(i < n_lat_tiles, i % seq_tiles, 0), 0))

    kern = functools.partial(_mla_prep_kernel, n_lat_tiles=n_lat_tiles)
    return pl.pallas_call(
        kern, grid=(t // tm,),
        in_specs=[pl.BlockSpec((tm, 512), lambda i: (i, cols["cq"] // 512)),
                  pl.BlockSpec((tm, 256), lambda i: (i, cols["ckv"] // 256)),
                  pl.BlockSpec((tm, LANES), lambda i: (i, cols["krope"] // LANES)),
                  pl.BlockSpec((1, 512), lambda i: (0, 0)),
                  pl.BlockSpec((1, 256), lambda i: (0, 0)),
                  pl.BlockSpec((512, qw), lambda i: (0, 0)),
                  pl.BlockSpec((256, qw), lambda i: (0, 0)),
                  tab_spec(), tab_spec(), tab_spec()],
        out_specs=[pl.BlockSpec((tm, qw), lambda i: (i, 0)),
                   pl.BlockSpec((tm, qw), lambda i: (i, 0)),
                   pl.BlockSpec((tm, MLA_H * MLA_V), lambda i: (i, 0))],
        out_shape=[jax.ShapeDtypeStruct((t, qw), BF16),
                   jax.ShapeDtypeStruct((t, qw), BF16),
                   jax.ShapeDtypeStruct((t, MLA_H * MLA_V), BF16)],
        compiler_params=_params(1), name="mla_prep",
    )(u, u, u, q_norm.reshape(1, -1), kv_norm.reshape(1, -1), w_uq_p, w_ukv, *rope64)


def _softmax_pv(scores, values, sink=None, base2=False):
    ex = jnp.exp2 if base2 else jnp.exp
    m = None
    for s in scores:
        mi = jnp.max(s, axis=-1, keepdims=True)
        m = mi if m is None else jnp.maximum(m, mi)
    if sink is not None:
        m = jnp.maximum(m, sink)
    den = None
    acc = None
    for s, v in zip(scores, values):
        p = ex(s - m)
        li = jnp.sum(p, axis=-1, keepdims=True)
        den = li if den is None else den + li
        o = _dot(p.astype(BF16), v)
        acc = o if acc is None else acc + o
    if sink is not None:
        den = den + ex(sink - m)
    return acc / den


def _diff_lambda(lam_ref, lam_init):
    lp = lam_ref[...]
    return (jnp.exp(jnp.sum(lp[0:1, :] * lp[1:2, :], axis=-1, keepdims=True))
            - jnp.exp(jnp.sum(lp[2:3, :] * lp[3:4, :], axis=-1, keepdims=True)) + lam_init)


def _diff_finish(o1, o2, lam, g, lam_init):
    o = o1 - lam * o2
    y = o * lax.rsqrt(jnp.mean(o * o, axis=-1, keepdims=True) + NORM_EPS) * g
    return y * (1.0 - lam_init)


DENSE_SUB = 256
DENSE_CHUNK = 256


def _transpose_to(dst_ref, src_ref):
    dst_ref[...] = src_ref[...].astype(F32).T.astype(BF16)


def _two_pass_softmax_pv(q, segments, s_ref):
    rows = q.shape[0]
    m_run, off = None, 0
    for k_ref, _, n in segments:
        for c0 in range(0, n, DENSE_CHUNK):
            w = min(DENSE_CHUNK, n - c0)
            s = _dot_nt(k_ref[c0:c0 + w, :], q)
            s_ref[off + c0:off + c0 + w, :] = s
            part = jnp.max(s.reshape(w // 8, 8, rows), axis=0)
            m_run = part if m_run is None else jnp.maximum(m_run, part)
        off += n
    m = jnp.max(m_run, axis=0, keepdims=True)
    l_run, acc, off = None, None, 0
    for _, vt_ref, n in segments:
        for c0 in range(0, n, DENSE_CHUNK):
            w = min(DENSE_CHUNK, n - c0)
            p = jnp.exp2(s_ref[off + c0:off + c0 + w, :] - m)
            part = jnp.sum(p.reshape(w // 8, 8, rows), axis=0)
            l_run = part if l_run is None else l_run + part
            pv = _dot(vt_ref[:, c0:c0 + w], p.astype(BF16))
            acc = pv if acc is None else acc + pv
        off += n
    o_t = acc * (1.0 / jnp.sum(l_run, axis=0, keepdims=True))
    return o_t.T


def _mla_attn_kernel(q_ref, kc_ref, kl_ref, vc_ref, vl_ref, oin_ref, o_ref, vtc_ref, vtl_ref, s_ref, *, sub):
    del oin_ref

    @pl.when(pl.program_id(2) == 0)
    def _():
        _transpose_to(vtc_ref, vc_ref)
        _transpose_to(vtl_ref, vl_ref)

    segments = [(kc_ref, vtc_ref, kc_ref.shape[0]), (kl_ref, vtl_ref, kl_ref.shape[0])]

    def body(t, carry):
        r0 = pl.multiple_of(t * sub, sub)
        o = _two_pass_softmax_pv(q_ref[pl.ds(r0, sub), :], segments, s_ref)
        o_ref[pl.ds(r0, sub), :] = o.astype(BF16)
        return carry

    lax.fori_loop(0, q_ref.shape[0] // sub, body, 0)


def _mla_attention(qa, ka, va, o, dims, branch):
    n_lat, seq, n_batch = dims
    n_ctx = (qa.shape[0] - n_lat) // n_batch
    tq = _tile(seq, 1024, 16)
    sub = _tile(tq, DENSE_SUB, 16)
    nq = seq // tq
    ctx_blk0 = n_lat // n_ctx
    kw = 2 * LANES
    return pl.pallas_call(
        functools.partial(_mla_attn_kernel, sub=sub), grid=(n_batch, MLA_H, nq),
        scratch_shapes=[pltpu.VMEM((MLA_V, n_ctx), BF16), pltpu.VMEM((MLA_V, seq), BF16),
                        pltpu.VMEM((n_ctx + seq, sub), F32)],
        in_specs=[pl.BlockSpec((tq, kw), lambda b, h, i: (b * nq + i, h)),
                  pl.BlockSpec((n_ctx, kw), lambda b, h, i: (ctx_blk0 + b, h)),
                  pl.BlockSpec((seq, kw), lambda b, h, i: (b, h)),
                  pl.BlockSpec((n_ctx, LANES), lambda b, h, i: (ctx_blk0 + b, h)),
                  pl.BlockSpec((seq, LANES), lambda b, h, i: (b, h)),
                  pl.BlockSpec(memory_space=pl.ANY)],
        out_specs=pl.BlockSpec((tq, LANES), lambda b, h, i: (b * nq + i, branch * 4 + h)),
        out_shape=jax.ShapeDtypeStruct(o.shape, o.dtype),
        input_output_aliases={5: 0},
        compiler_params=_params(3), name="mla_attention",
    )(qa, ka, ka, va, va, o)


def _diff_attn_kernel(q_ref, kc_ref, kl_ref, vc_ref, vl_ref, cosq_ref, saq_ref, sbq_ref, cos_ref, sa_ref, sb_ref,
                      lam_ref, g_ref, oin_ref, o_ref, kr_ref, vtc_ref, vtl_ref, s0_ref, s1_ref, *, lam_init, sub):
    del oin_ref
    quarter = DIFF_DK // 4

    @pl.when(pl.program_id(2) == 0)
    def _():
        kr_ref[...] = _rope(kl_ref[...].astype(F32), cos_ref[...], sa_ref[...], sb_ref[...], quarter).astype(BF16)
        _transpose_to(vtc_ref, vc_ref)
        _transpose_to(vtl_ref, vl_ref)

    scale = DIFF_DK ** -0.5 * LOG2E
    segments = [(kc_ref, vtc_ref, kc_ref.shape[0]), (kr_ref, vtl_ref, kr_ref.shape[0])]
    lam = _diff_lambda(lam_ref, lam_init)

    def body(t, carry):
        r0 = pl.multiple_of(t * sub, sub)
        rows = pl.ds(r0, sub)
        q = _rope(q_ref[rows, :].astype(F32) * scale, cosq_ref[rows, :], saq_ref[rows, :], sbq_ref[rows, :], quarter)
        lane = lax.broadcasted_iota(jnp.int32, q.shape, 1)
        q0 = jnp.where(lane < DIFF_DK, q, 0.0).astype(BF16)
        q1 = jnp.where(lane >= DIFF_DK, q, 0.0).astype(BF16)
        o0 = _two_pass_softmax_pv(q0, segments, s0_ref)
        o1 = _two_pass_softmax_pv(q1, segments, s1_ref)
        o_ref[rows, :] = _diff_finish(o0, o1, lam, g_ref[...], lam_init).astype(BF16)
        return carry

    lax.fori_loop(0, q_ref.shape[0] // sub, body, 0)


def _diff_attention(u, cols, rope64, diff_lambda_l, subln_g, lam_init, o, dims, branch):
    n_lat, seq, n_batch = dims
    n_ctx = (u.shape[0] - n_lat) // n_batch
    tq = _tile(seq, 1024, 16)
    sub = _tile(tq, DENSE_SUB, 16)
    nq = seq // tq
    ctx_blk0 = n_lat // n_ctx
    qb, kb, vb = cols["diff_q"] // LANES, cols["diff_k"] // LANES, cols["diff_v"] // LANES
    kern = functools.partial(_diff_attn_kernel, lam_init=lam_init, sub=sub)

    def tab_q():
        return pl.BlockSpec((tq, LANES), lambda b, h, i: (i, 0))

    def tab_k():
        return pl.BlockSpec((seq, LANES), lambda b, h, i: (0, 0))

    return pl.pallas_call(
        kern, grid=(n_batch, DIFF_H, nq),
        in_specs=[pl.BlockSpec((tq, LANES), lambda b, h, i: (b * nq + i, qb + h)),
                  pl.BlockSpec((n_ctx, LANES), lambda b, h, i: (ctx_blk0 + b, kb + h)),
                  pl.BlockSpec((seq, LANES), lambda b, h, i: (b, kb + h)),
                  pl.BlockSpec((n_ctx, LANES), lambda b, h, i: (ctx_blk0 + b, vb + h)),
                  pl.BlockSpec((seq, LANES), lambda b, h, i: (b, vb + h)),
                  tab_q(), tab_q(), tab_q(), tab_k(), tab_k(), tab_k(),
                  pl.BlockSpec((4, DIFF_DK), lambda b, h, i: (0, 0)),
                  pl.BlockSpec((1, DIFF_DV), lambda b, h, i: (0, 0)),
                  pl.BlockSpec(memory_space=pl.ANY)],
        out_specs=pl.BlockSpec((tq, LANES), lambda b, h, i: (b * nq + i, branch * 4 + h)),
        out_shape=jax.ShapeDtypeStruct(o.shape, o.dtype),
        scratch_shapes=[pltpu.VMEM((seq, LANES), BF16),
                        pltpu.VMEM((DIFF_DV, n_ctx), BF16), pltpu.VMEM((DIFF_DV, seq), BF16),
                        pltpu.VMEM((n_ctx + seq, sub), F32), pltpu.VMEM((n_ctx + seq, sub), F32)],
        input_output_aliases={13: 0},
        compiler_params=_params(3), name="diff_attention",
    )(u, u, u, u, u, *rope64, *rope64, diff_lambda_l, subln_g.reshape(1, -1), o)


def _local_attn_kernel(*refs, tq, nk, back, seq, use_rope, use_sink):
    it = iter(refs)
    q_ref, kl_ref, vl_ref, kc_ref, vc_ref, bias_ref = (next(it) for _ in range(6))
    if use_rope:
        cosq_ref, saq_ref, sbq_ref, cos_ref, sa_ref, sb_ref = (next(it) for _ in range(6))
    if use_sink:
        sink_ref = next(it)
    oin_ref, o_ref = next(it), next(it)
    del oin_ref
    i = pl.program_id(2)
    scale = LANES ** -0.5
    if use_rope:
        kr_ref = next(it)
        quarter = SWA_HD // 4

        @pl.when(i == 0)
        def _():
            kr_ref[...] = _rope(kl_ref[...].astype(F32), cos_ref[...], sa_ref[...], sb_ref[...],
                                quarter).astype(BF16)

        q = _rope(q_ref[...].astype(F32) * scale, cosq_ref[...], saq_ref[...], sbq_ref[...], quarter).astype(BF16)
        k_src = kr_ref
    else:
        q = (q_ref[...].astype(F32) * scale).astype(BF16)
        k_src = kl_ref
    ks = pl.multiple_of(jnp.clip(i * tq - back, 0, seq - nk), 64)
    s_l = _dot_nt(q, k_src[pl.ds(ks, nk), :]) + bias_ref[...]
    s_c = _dot_nt(q, kc_ref[...])
    sink = sink_ref[pl.program_id(1)] if use_sink else None
    o = _softmax_pv([s_c, s_l], [vc_ref[...], vl_ref[pl.ds(ks, nk), :]], sink=sink)
    o_ref[...] = o.astype(BF16)


def _local_attention(u, qcol, kcol, vcol, n_heads, n_kv, bias_plan, rope, sink, o, dims, branch, name):
    bias, tile_map, nk, back = bias_plan

    def bias_tile(i):
        idx = tile_map[-1]
        for t in range(len(tile_map) - 2, -1, -1):
            idx = jnp.where(i == t, tile_map[t], idx)
        return idx

    n_lat, seq, n_batch = dims
    n_ctx = (u.shape[0] - n_lat) // n_batch
    tq = bias.shape[2]
    nq = seq // tq
    grp = n_heads // n_kv
    ctx_blk0 = n_lat // n_ctx
    qb, kb, vb = qcol // LANES, kcol // LANES, vcol // LANES
    per_head_bias = bias.shape[0] > 1
    use_rope, use_sink = rope is not None, sink is not None
    kern = functools.partial(_local_attn_kernel, tq=tq, nk=nk, back=back, seq=seq,
                             use_rope=use_rope, use_sink=use_sink)
    in_specs = [pl.BlockSpec((tq, LANES), lambda b, h, i: (b * nq + i, qb + h)),
                pl.BlockSpec((seq, LANES), lambda b, h, i: (b, kb + h // grp)),
                pl.BlockSpec((seq, LANES), lambda b, h, i: (b, vb + h // grp)),
                pl.BlockSpec((n_ctx, LANES), lambda b, h, i: (ctx_blk0 + b, kb + h // grp)),
                pl.BlockSpec((n_ctx, LANES), lambda b, h, i: (ctx_blk0 + b, vb + h // grp)),
                pl.BlockSpec((None, None, tq, nk),
                             lambda b, h, i: (h if per_head_bias else 0, bias_tile(i), 0, 0))]
    args = [u, u, u, u, u, bias]
    scratch = []
    if use_rope:
        in_specs += [pl.BlockSpec((tq, LANES), lambda b, h, i: (i, 0))] * 3
        in_specs += [pl.BlockSpec((seq, LANES), lambda b, h, i: (0, 0))] * 3
        args += [*rope, *rope]
        scratch.append(pltpu.VMEM((seq, LANES), BF16))
    if use_sink:
        in_specs.append(pl.BlockSpec(memory_space=pltpu.SMEM))
        args.append(sink)
    in_specs.append(pl.BlockSpec(memory_space=pl.ANY))
    args.append(o)
    return pl.pallas_call(
        kern, grid=(n_batch, n_heads, nq), in_specs=in_specs,
        out_specs=pl.BlockSpec((tq, LANES), lambda b, h, i: (b * nq + i, branch * 4 + h)),
        out_shape=jax.ShapeDtypeStruct(o.shape, o.dtype),
        scratch_shapes=scratch,
        input_output_aliases={len(args) - 1: 0},
        compiler_params=_params(3), name=name,
    )(*args)


def _local_tiling(seq, span):
    tq = _tile(seq, 512, 64)
    nk = min(seq, tq + 2 * span)
    return tq, nk


def _dedupe(keys):
    first, uniq, tile_map = {}, [], []
    for i, key in enumerate(keys):
        if key not in first:
            first[key] = len(uniq)
            uniq.append(i)
        tile_map.append(first[key])
    return uniq, tuple(tile_map)


def _swa_bias(seq):
    tq, nk = _local_tiling(seq, SWA_WINDOW)
    nq = seq // tq
    tiles = []
    for i in range(nq):
        ks = min(max(i * tq - SWA_WINDOW, 0), seq - nk)
        qp = i * tq + np.arange(tq)[:, None]
        kp = ks + np.arange(nk)[None, :]
        tiles.append(np.where(np.abs(qp - kp) <= SWA_WINDOW, 0.0, NEG_INF).astype(np.float32))
    uniq, tile_map = _dedupe([t.tobytes() for t in tiles])
    return jnp.asarray(np.stack([tiles[i] for i in uniq])[None], F32), tile_map, nk, SWA_WINDOW


def _na_bias_kernel(rpb_ref, out_ref, toe_ref, *, row_offsets):
    n_d, n_c = 2 * NA_KH - 1, 2 * NA_KW - 1
    hd = pl.program_id(0)
    shape = (GRID_W, 2 * GRID_W)
    qc = lax.broadcasted_iota(jnp.int32, shape, 0)
    lane = lax.broadcasted_iota(jnp.int32, shape, 1)
    kc = lane & (GRID_W - 1)
    col_start = jnp.clip(qc - NA_KW // 2, 0, GRID_W - NA_KW)
    valid = (kc >= col_start) & (kc < col_start + NA_KW)
    dc = jnp.clip(kc - qc, -(NA_KW - 1), NA_KW - 1) + (NA_KW - 1)
    for d in range(n_d):
        blk = jnp.zeros(shape, F32)
        for j in range(n_c):
            blk = jnp.where(dc == j, rpb_ref[(hd * n_d + d) * n_c + j], blk)
        toe_ref[d] = jnp.where(valid, blk, NEG_INF)
    toe_ref[n_d] = jnp.full(shape, NEG_INF, F32)
    left = lane < GRID_W
    for u, offs in enumerate(row_offsets):
        for a, row in enumerate(offs):
            for b in range(0, len(row), 2):
                out_ref[u, a * GRID_W:(a + 1) * GRID_W, b * GRID_W:(b + 2) * GRID_W] = jnp.where(
                    left, toe_ref[row[b]], toe_ref[row[b + 1]])


def _na_bias(rpb, seq):
    rows = seq // GRID_W
    kh = min(NA_KH, rows)
    back_rows = kh // 2
    tq, nk = _local_tiling(seq, back_rows * GRID_W)
    nq, qr_n, kr_n = seq // tq, tq // GRID_W, nk // GRID_W
    assert kr_n % 2 == 0
    n_heads = rpb.shape[0]
    n_d = 2 * NA_KH - 1
    d_idx = []
    for i in range(nq):
        ks_row = min(max(i * qr_n - back_rows, 0), rows - kr_n)
        qr = i * qr_n + np.arange(qr_n)[:, None]
        kr = ks_row + np.arange(kr_n)[None, :]
        start = np.clip(qr - kh // 2, 0, rows - kh)
        ok = (kr >= start) & (kr < start + kh)
        d_idx.append(np.where(ok, kr - qr + (NA_KH - 1), n_d))
    uniq, tile_map = _dedupe([d.tobytes() for d in d_idx])
    row_offsets = tuple(tuple(tuple(int(v) for v in row) for row in d_idx[i]) for i in uniq)
    bias = pl.pallas_call(
        functools.partial(_na_bias_kernel, row_offsets=row_offsets), grid=(n_heads,),
        in_specs=[pl.BlockSpec(memory_space=pltpu.SMEM)],
        out_specs=pl.BlockSpec((None, len(uniq), tq, nk), lambda hd: (hd, 0, 0, 0)),
        out_shape=jax.ShapeDtypeStruct((n_heads, len(uniq), tq, nk), F32),
        scratch_shapes=[pltpu.VMEM((n_d + 1, GRID_W, 2 * GRID_W), F32)],
        compiler_params=_params(1), name="na_bias",
    )(rpb.reshape(-1))
    return bias, tile_map, nk, back_rows * GRID_W


def _ctx_attn_kernel(u_ref, qa_ref, ka_ref, va_ref, sink_ref, lam_ref, g_ref, oin_ref, o_ref, *, cols, lam_init):
    del oin_ref

    def col(name, hd, width=LANES):
        c0 = cols[name] + hd * width
        return u_ref[:, c0:c0 + width]

    for hd in range(MLA_H):
        q = qa_ref[:, hd * 2 * LANES:(hd + 1) * 2 * LANES]
        k = ka_ref[:, hd * 2 * LANES:(hd + 1) * 2 * LANES]
        v = va_ref[:, hd * LANES:(hd + 1) * LANES]
        o_ref[:, hd * LANES:(hd + 1) * LANES] = _softmax_pv([_dot_nt(q, k)], [v], base2=True).astype(BF16)
    grp = SWA_H // SWA_KV_H
    scale = SWA_HD ** -0.5
    for hd in range(SWA_H):
        q = (col("swa_q", hd).astype(F32) * scale).astype(BF16)
        o = _softmax_pv([_dot_nt(q, col("swa_k", hd // grp))], [col("swa_v", hd // grp)], sink=sink_ref[hd])
        o_ref[:, BRANCH_W + hd * LANES:BRANCH_W + (hd + 1) * LANES] = o.astype(BF16)
    scale = NA_HD ** -0.5
    for hd in range(NA_H):
        q = (col("na_q", hd).astype(F32) * scale).astype(BF16)
        o = _softmax_pv([_dot_nt(q, col("na_k", hd))], [col("na_v", hd)])
        o_ref[:, 2 * BRANCH_W + hd * LANES:2 * BRANCH_W + (hd + 1) * LANES] = o.astype(BF16)
    scale = DIFF_DK ** -0.5
    lam = _diff_lambda(lam_ref, lam_init)
    for hd in range(DIFF_H):
        q = col("diff_q", hd).astype(F32) * scale
        k = col("diff_k", hd)
        v = col("diff_v", hd)
        lane = lax.broadcasted_iota(jnp.int32, q.shape, 1)
        outs = []
        for j in range(2):
            qj = jnp.where((lane >= j * DIFF_DK) & (lane < (j + 1) * DIFF_DK), q, 0.0).astype(BF16)
            outs.append(_softmax_pv([_dot_nt(qj, k)], [v]))
        o = _diff_finish(outs[0], outs[1], lam, g_ref[...], lam_init)
        o_ref[:, 3 * BRANCH_W + hd * LANES:3 * BRANCH_W + (hd + 1) * LANES] = o.astype(BF16)


def _ctx_attention(u, cols, qa, ka, va, sink, diff_lambda_l, subln_g, lam_init, o, dims):
    n_lat, seq, n_batch = dims
    n_ctx = (u.shape[0] - n_lat) // n_batch
    blk0 = n_lat // n_ctx
    kern = functools.partial(_ctx_attn_kernel, cols=cols, lam_init=lam_init)
    return pl.pallas_call(
        kern, grid=(n_batch,),
        in_specs=[pl.BlockSpec((n_ctx, u.shape[1]), lambda b: (blk0 + b, 0)),
                  pl.BlockSpec((n_ctx, qa.shape[1]), lambda b: (blk0 + b, 0)),
                  pl.BlockSpec((n_ctx, ka.shape[1]), lambda b: (blk0 + b, 0)),
                  pl.BlockSpec((n_ctx, va.shape[1]), lambda b: (blk0 + b, 0)),
                  pl.BlockSpec(memory_space=pltpu.SMEM),
                  pl.BlockSpec((4, DIFF_DK), lambda b: (0, 0)),
                  pl.BlockSpec((1, DIFF_DV), lambda b: (0, 0)),
                  pl.BlockSpec(memory_space=pl.ANY)],
        out_specs=pl.BlockSpec((n_ctx, o.shape[1]), lambda b: (blk0 + b, 0)),
        out_shape=jax.ShapeDtypeStruct(o.shape, o.dtype),
        input_output_aliases={7: 0},
        compiler_params=_params(1), name="ctx_attention",
    )(u, qa, ka, va, sink, diff_lambda_l, subln_g.reshape(1, -1), o)


MOE_TM = 512
GATHER_ROWS = 256


def _route_plan(top_i, tm):
    t = top_i.shape[0]
    n_assign = t * TOP_K
    n_tiles = (n_assign + N_EXPERTS * (tm - 1)) // tm
    e_flat = top_i.reshape(-1)
    onehot = (e_flat[:, None] == jnp.arange(N_EXPERTS)[None, :]).astype(jnp.int32)
    csum = jnp.cumsum(onehot, axis=0)
    rank = jnp.sum(onehot * csum, axis=1) - 1
    counts = csum[-1]
    padded = ((counts + tm - 1) // tm) * tm
    ends = jnp.cumsum(padded)
    offs = ends - padded
    pos = (jnp.sum(onehot * offs[None, :], axis=1) + rank).astype(jnp.int32)
    src = jnp.zeros((n_tiles * tm,), jnp.int32).at[pos].set(
        jnp.arange(n_assign, dtype=jnp.int32) // TOP_K, unique_indices=True, mode="promise_in_bounds")
    tile_start = jnp.arange(n_tiles, dtype=jnp.int32) * tm
    tile_expert = jnp.minimum(jnp.sum((tile_start[:, None] >= ends[None, :]).astype(jnp.int32), axis=1),
                              N_EXPERTS - 1).astype(jnp.int32)
    tile_valid = (tile_start < ends[-1]).astype(jnp.int32)
    return pos, src, tile_expert, tile_valid


def _row_copies(idx_ref, base, src_hbm, bufs, sems, slot, n_rows, idx_stride):
    def body(r, carry):
        for s, buf in enumerate(bufs):
            row = idx_ref[(base + r) * idx_stride + s]
            pltpu.make_async_copy(src_hbm.at[pl.ds(row, 1), :], buf.at[slot, pl.ds(r, 1), :],
                                  sems.at[slot, s]).start()
        return carry
    lax.fori_loop(0, n_rows, body, 0, unroll=8)


def _wait_rows(src_hbm, bufs, sems, slot, n_rows):
    for s, buf in enumerate(bufs):
        pltpu.make_async_copy(src_hbm.at[pl.ds(0, n_rows), :], buf.at[slot], sems.at[slot, s]).wait()


def _gather_kernel(src_ref, tv_ref, x_hbm, o_ref, x_ref, buf_ref, sem, *, rows):
    i = pl.program_id(0)

    @pl.when(i == 0)
    def _():
        cp = pltpu.make_async_copy(x_hbm, x_ref, sem)
        cp.start()
        cp.wait()

    @pl.when(tv_ref[i] == 1)
    def _():
        def body(r, carry):
            buf_ref[pl.ds(r, 1), :] = x_ref[pl.ds(src_ref[i * rows + r], 1), :]
            return carry
        lax.fori_loop(0, rows, body, 0, unroll=8)
        w = buf_ref[...]
        half = w.shape[1]
        o_ref[:, :half] = lax.bitcast_convert_type(w << 16, F32).astype(BF16)
        o_ref[:, half:] = lax.bitcast_convert_type(w & jnp.uint32(0xFFFF0000), F32).astype(BF16)

    @pl.when(tv_ref[i] == 0)
    def _():
        o_ref[...] = jnp.zeros(o_ref.shape, BF16)


def _gather_rows(xp, src, tile_valid, rows):
    t, half = xp.shape
    n_rows = src.shape[0]
    kern = functools.partial(_gather_kernel, rows=rows)
    return pl.pallas_call(
        kern,
        grid_spec=pltpu.PrefetchScalarGridSpec(
            num_scalar_prefetch=2, grid=(n_rows // rows,),
            in_specs=[pl.BlockSpec(memory_space=pl.ANY)],
            out_specs=pl.BlockSpec((rows, 2 * half), lambda i, s, v: (i, 0)),
            scratch_shapes=[pltpu.VMEM((t, half), jnp.uint32), pltpu.VMEM((rows, half), jnp.uint32),
                            pltpu.SemaphoreType.DMA(())]),
        out_shape=jax.ShapeDtypeStruct((n_rows, 2 * half), BF16),
        compiler_params=_params(1), name="moe_gather",
    )(src, tile_valid, xp)


def _expert_changed(te_ref, i):
    return jnp.logical_or(i == 0, te_ref[i] != te_ref[jnp.maximum(i - 1, 0)])


def _gmm_up_kernel(te_ref, tv_ref, x_ref, w1_ref, w3_ref, o_ref, w1b_ref, w3b_ref):
    i = pl.program_id(1)
    _cast_weights(_expert_changed(te_ref, i), [w1_ref, w3_ref], [w1b_ref, w3b_ref])

    @pl.when(tv_ref[i] == 1)
    def _():
        x = x_ref[...]
        o_ref[...] = (_silu(_dot(x, w1b_ref[...])) * _dot(x, w3b_ref[...])).astype(BF16)

    @pl.when(tv_ref[i] == 0)
    def _():
        o_ref[...] = jnp.zeros(o_ref.shape, BF16)


def _gmm_up(xs, w1, w3, idx, tile_expert, tile_valid, tm):
    r, d = xs.shape
    f = w1.shape[3]
    tn = _tile(f, 512, LANES)
    return pl.pallas_call(
        _gmm_up_kernel,
        grid_spec=pltpu.PrefetchScalarGridSpec(
            num_scalar_prefetch=2, grid=(f // tn, r // tm),
            in_specs=[pl.BlockSpec((tm, d), lambda j, i, te, tv: (i, 0)),
                      pl.BlockSpec((None, None, d, tn), lambda j, i, te, tv: (idx, te[i], 0, j)),
                      pl.BlockSpec((None, None, d, tn), lambda j, i, te, tv: (idx, te[i], 0, j))],
            out_specs=pl.BlockSpec((tm, tn), lambda j, i, te, tv: (i, j)),
            scratch_shapes=[pltpu.VMEM((d, tn), BF16), pltpu.VMEM((d, tn), BF16)]),
        out_shape=jax.ShapeDtypeStruct((r, f), BF16),
        compiler_params=_params(2), name="moe_up",
    )(tile_expert, tile_valid, xs, w1, w3)


def _gmm_down_kernel(te_ref, tv_ref, a_ref, w_ref, o_ref, wb_ref):
    i = pl.program_id(1)
    _cast_weights(_expert_changed(te_ref, i), [w_ref], [wb_ref])

    @pl.when(tv_ref[i] == 1)
    def _():
        o_ref[...] = _dot(a_ref[...], wb_ref[...])

    @pl.when(tv_ref[i] == 0)
    def _():
        o_ref[...] = jnp.zeros(o_ref.shape, F32)


def _gmm_down(a, w2, idx, tile_expert, tile_valid, tm):
    r, f = a.shape
    d = w2.shape[3]
    tn = _tile(d, 512, LANES)
    return pl.pallas_call(
        _gmm_down_kernel,
        grid_spec=pltpu.PrefetchScalarGridSpec(
            num_scalar_prefetch=2, grid=(d // tn, r // tm),
            in_specs=[pl.BlockSpec((tm, f), lambda j, i, te, tv: (i, 0)),
                      pl.BlockSpec((None, None, f, tn), lambda j, i, te, tv: (idx, te[i], 0, j))],
            out_specs=pl.BlockSpec((tm, tn), lambda j, i, te, tv: (i, j)),
            scratch_shapes=[pltpu.VMEM((f, tn), BF16)]),
        out_shape=jax.ShapeDtypeStruct((r, d), F32),
        compiler_params=_params(2), name="moe_down",
    )(tile_expert, tile_valid, a, w2)


def _combine_kernel(pos_ref, y_hbm, h_ref, tg_ref, gate_ref, o_ref, buf0, buf1, sems, *, rows, n_lat, seq, n_batch):
    i = pl.program_id(0)
    n = pl.num_programs(0)
    slot = i % 2
    bufs = [buf0, buf1]

    @pl.when(i == 0)
    def _():
        _row_copies(pos_ref, 0, y_hbm, bufs, sems, 0, rows, TOP_K)

    @pl.when(i + 1 < n)
    def _():
        _row_copies(pos_ref, (i + 1) * rows, y_hbm, bufs, sems, 1 - slot, rows, TOP_K)

    _wait_rows(y_hbm, bufs, sems, slot, rows)
    grp = _group_of_tile(i, rows, n_lat, seq, n_batch)
    gate = gate_ref[pl.ds(grp, 1), :]
    tg = tg_ref[...]
    mix = tg[:, 0:1] * buf0[slot] + tg[:, 1:2] * buf1[slot]
    o_ref[...] = h_ref[...] + gate * mix


def _combine(h, y, pos, top_g, mod_l, gate_chunk, dims):
    n_lat, seq, n_batch = dims
    t, d = h.shape
    rows = _tile(math.gcd(seq, t - n_lat), GATHER_ROWS, 8)
    kern = functools.partial(_combine_kernel, rows=rows, n_lat=n_lat, seq=seq, n_batch=n_batch)
    return pl.pallas_call(
        kern,
        grid_spec=pltpu.PrefetchScalarGridSpec(
            num_scalar_prefetch=1, grid=(t // rows,),
            in_specs=[pl.BlockSpec(memory_space=pl.ANY),
                      pl.BlockSpec((rows, d), lambda i, p: (i, 0)),
                      pl.BlockSpec((rows, LANES), lambda i, p: (i, 0)),
                      pl.BlockSpec((8, d), lambda i, p: (0, gate_chunk))],
            out_specs=pl.BlockSpec((rows, d), lambda i, p: (i, 0)),
            scratch_shapes=[pltpu.VMEM((2, rows, d), F32), pltpu.VMEM((2, rows, d), F32),
                            pltpu.SemaphoreType.DMA((2, 2))]),
        out_shape=jax.ShapeDtypeStruct(h.shape, F32),
        input_output_aliases={2: 0},
        compiler_params=_params(1), name="moe_combine",
    )(pos, y, h, top_g, mod_l)


def kernel(x, c, ctx, c_ctx, mod_w, mod_b, norm1_g, norm2_g, w_in, mla_q_norm, mla_kv_norm, mla_w_uq, mla_w_ukv,
           swa_sink, na_rpb, diff_lambda, diff_subln_g, w_branch, w_out, ffn_w1, ffn_w3, ffn_w2, moe_router,
           moe_w1, moe_w3, moe_w2, final_norm_g):
    n_batch, seq, d = x.shape
    n_ctx = ctx.shape[1]
    depth = mod_w.shape[0]
    n_lat = n_batch * seq
    dims = (n_lat, seq, n_batch)

    h = jnp.concatenate([x.reshape(n_lat, d), ctx.reshape(n_batch * n_ctx, d)], axis=0)
    t = h.shape[0]
    c8 = jnp.concatenate([c, c_ctx[None, :], jnp.zeros((8 - n_batch - 1, d), F32)], axis=0)
    mod = _mod_all(c8, mod_w, mod_b)

    cols, width_u, in_plans = _in_layout(d)
    w_uq = mla_w_uq.reshape(depth, -1, MLA_H, MLA_NOPE + MLA_ROPE)
    w_uq_p = jnp.pad(w_uq, ((0, 0), (0, 0), (0, 0), (0, 2 * LANES - MLA_NOPE - MLA_ROPE)))
    w_uq_p = w_uq_p.reshape(depth, -1, MLA_H * 2 * LANES)
    rope64 = _rope_tables(seq, MLA_ROPE)
    rope128 = _rope_tables(seq, SWA_HD)
    swa_plan = _swa_bias(seq)
    router_p = jnp.pad(moe_router, ((0, 0), (0, 0), (0, LANES - N_EXPERTS)))

    for l in range(depth):
        need_ctx = l < depth - 1
        lam_init = 0.8 - 0.6 * math.exp(-0.3 * l)
        mod_l = mod[l]

        xn = _norm_mod(h, norm1_g[l], mod_l, 0, 1, dims)
        u = _project_in(xn, w_in, l, width_u, in_plans)

        qa, ka, va = _mla_prep(u, cols, mla_q_norm[l], mla_kv_norm[l], w_uq_p[l], mla_w_ukv[l], rope64, dims)
        o = jnp.zeros((t, N_BRANCH * BRANCH_W), BF16)
        o = _mla_attention(qa, ka, va, o, dims, 0)
        o = _local_attention(u, cols["swa_q"], cols["swa_k"], cols["swa_v"], SWA_H, SWA_KV_H, swa_plan,
                             rope128, swa_sink[l], o, dims, 1, "swa_attention")
        o = _local_attention(u, cols["na_q"], cols["na_k"], cols["na_v"], NA_H, NA_H, _na_bias(na_rpb[l], seq),
                             None, None, o, dims, 2, "na_attention")
        o = _diff_attention(u, cols, rope64, diff_lambda[l], diff_subln_g[l], lam_init, o, dims, 3)
        if need_ctx:
            o = _ctx_attention(u, cols, qa, ka, va, swa_sink[l], diff_lambda[l], diff_subln_g[l], lam_init, o, dims)

        merged = _merge(o, u, cols["gates"], w_branch, l)
        h = _project_residual(merged, w_out, l, h, mod_l, 2, dims)

        i = l // 2
        if l % 2 == 0:
            xn = _norm_mod(h, norm2_g[l], mod_l, 3, 4, dims)
            a = _swiglu_up(xn, ffn_w1, ffn_w3, i)
            h = _project_residual(a, ffn_w2, i, h, mod_l, 5, dims)
        else:
            xn, top_g, top_i = _norm_mod(h, norm2_g[l], mod_l, 3, 4, dims, router=router_p[i])
            pos, src, tile_expert, tile_valid = _route_plan(top_i[:, :TOP_K], MOE_TM)
            xs = _gather_rows(xn, src, tile_valid, MOE_TM)
            a = _gmm_up(xs, moe_w1, moe_w3, i, tile_expert, tile_valid, MOE_TM)
            y = _gmm_down(a, moe_w2, i, tile_expert, tile_valid, MOE_TM)
            h = _combine(h, y, pos, top_g, mod_l, 5, dims)

    out = _final_norm(h, final_norm_g, n_lat)
    return out.reshape(n_batch, seq, d)
```

```python
import functools
import math

import jax
import jax.numpy as jnp
import numpy as np
from jax import lax
from jax.experimental import pallas as pl
from jax.experimental.pallas import tpu as pltpu

F32 = jnp.float32
BF16 = jnp.bfloat16

GRID_W = 64
ROPE_THETA = 10000.0
NORM_EPS = 1e-6
NEG_INF = -1e30
LOG2E = math.log2(math.e)

MLA_H, MLA_NOPE, MLA_ROPE, MLA_V = 4, 128, 64, 128
SWA_H, SWA_KV_H, SWA_HD, SWA_WINDOW = 4, 2, 128, 128
NA_H, NA_HD, NA_KH, NA_KW = 4, 128, 8, 16
DIFF_H, DIFF_DK, DIFF_DV = 4, 64, 128
N_BRANCH, BRANCH_W = 4, 512
N_EXPERTS, TOP_K = 8, 2
LANES = 128

VMEM_LIMIT = 56 * 1024 * 1024


def _params(n_axes):
    return pltpu.CompilerParams(dimension_semantics=("arbitrary",) * n_axes, vmem_limit_bytes=VMEM_LIMIT)


def _tile(n, pref, mult):
    if n <= pref:
        return n
    t = (pref // mult) * mult
    while t > mult and n % t:
        t -= mult
    assert n % t == 0, (n, pref, mult)
    return t


def _dot(a, b):
    return jnp.dot(a, b, preferred_element_type=F32)


def _dot_nt(a, b):
    return lax.dot_general(a, b, (((1,), (1,)), ((), ())), preferred_element_type=F32)


def _silu(x):
    return x * (1.0 / (1.0 + jnp.exp(-x)))


def _sigmoid(x):
    return 0.5 * (1.0 + jnp.tanh(0.5 * x))


def _mod_kernel(c_ref, w_ref, b_ref, o_ref):
    a = _silu(c_ref[...]).astype(BF16)
    o_ref[...] = _dot(a, w_ref[...].astype(BF16)) + b_ref[...]


def _mod_all(c8, mod_w, mod_b):
    n_l, d, n = mod_w.shape
    tn = _tile(n, 1024, LANES)
    return pl.pallas_call(
        _mod_kernel,
        grid=(n_l, n // tn),
        in_specs=[
            pl.BlockSpec((8, d), lambda l, j: (0, 0)),
            pl.BlockSpec((None, d, tn), lambda l, j: (l, 0, j)),
            pl.BlockSpec((None, 1, tn), lambda l, j: (l, 0, j)),
        ],
        out_specs=pl.BlockSpec((None, 8, tn), lambda l, j: (l, 0, j)),
        out_shape=jax.ShapeDtypeStruct((n_l, 8, n), F32),
        compiler_params=_params(2),
        name="mod_all",
    )(c8, mod_w, mod_b.reshape(n_l, 1, n))


def _group_of_tile(i, tm, n_lat, seq, n_batch):
    row0 = i * tm
    return jnp.where(row0 < n_lat, row0 // seq, n_batch)


def _norm_kernel(h_ref, g_ref, sh_ref, sc_ref, *rest, tm, n_lat, seq, n_batch, route):
    grp = _group_of_tile(pl.program_id(0), tm, n_lat, seq, n_batch)
    x = h_ref[...]
    y = x * lax.rsqrt(jnp.mean(x * x, axis=-1, keepdims=True) + NORM_EPS) * g_ref[...]
    sh = sh_ref[pl.ds(grp, 1), :]
    sc = sc_ref[pl.ds(grp, 1), :]
    xn = y * (1.0 + sc) + sh
    if not route:
        (o_ref,) = rest
        o_ref[...] = xn.astype(BF16)
        return
    r_ref, o_ref, tg_ref, ti_ref = rest
    bits = lax.bitcast_convert_type(xn.astype(BF16).astype(F32), jnp.uint32)
    half = bits.shape[1] // 2
    o_ref[...] = (bits[:, half:] & jnp.uint32(0xFFFF0000)) | (bits[:, :half] >> 16)
    logits = jnp.dot(xn, r_ref[...], preferred_element_type=F32, precision=lax.Precision.HIGHEST)
    lane = lax.broadcasted_iota(jnp.int32, logits.shape, 1).astype(F32)
    logits = jnp.where(lane < N_EXPERTS, logits, -jnp.inf)
    m1 = jnp.max(logits, axis=-1, keepdims=True)
    i1 = jnp.min(jnp.where(logits == m1, lane, float(LANES)), axis=-1, keepdims=True)
    rest_l = jnp.where(lane == i1, -jnp.inf, logits)
    m2 = jnp.max(rest_l, axis=-1, keepdims=True)
    i2 = jnp.min(jnp.where(rest_l == m2, lane, float(LANES)), axis=-1, keepdims=True)
    e2 = jnp.exp(m2 - m1)
    den = 1.0 + e2
    tg_ref[...] = jnp.where(lane == 0.0, 1.0 / den, jnp.where(lane == 1.0, e2 / den, 0.0))
    ti_ref[...] = jnp.where(lane == 0.0, i1, jnp.where(lane == 1.0, i2, 0.0)).astype(jnp.int32)


def _norm_mod(h, g, mod_l, shift_chunk, scale_chunk, dims, router=None):
    n_lat, seq, n_batch = dims
    t, d = h.shape
    tm = _tile(math.gcd(seq, t - n_lat), 512, 8)
    route = router is not None
    kern = functools.partial(_norm_kernel, tm=tm, n_lat=n_lat, seq=seq, n_batch=n_batch, route=route)
    in_specs = [
        pl.BlockSpec((tm, d), lambda i: (i, 0)),
        pl.BlockSpec((1, d), lambda i: (0, 0)),
        pl.BlockSpec((8, d), lambda i: (0, shift_chunk)),
        pl.BlockSpec((8, d), lambda i: (0, scale_chunk)),
    ]
    args = [h, g.reshape(1, d), mod_l, mod_l]
    if route:
        in_specs.append(pl.BlockSpec((d, LANES), lambda i: (0, 0)))
        args.append(router)
        out_specs = [pl.BlockSpec((tm, d // 2), lambda i: (i, 0)),
                     pl.BlockSpec((tm, LANES), lambda i: (i, 0)),
                     pl.BlockSpec((tm, LANES), lambda i: (i, 0))]
        out_shape = [jax.ShapeDtypeStruct((t, d // 2), jnp.uint32),
                     jax.ShapeDtypeStruct((t, LANES), F32),
                     jax.ShapeDtypeStruct((t, LANES), jnp.int32)]
    else:
        out_specs = pl.BlockSpec((tm, d), lambda i: (i, 0))
        out_shape = jax.ShapeDtypeStruct((t, d), BF16)
    return pl.pallas_call(
        kern, grid=(t // tm,), in_specs=in_specs, out_specs=out_specs, out_shape=out_shape,
        compiler_params=_params(1), name="norm_route" if route else "norm_mod",
    )(*args)


def _join_norm_kernel(x_ref, c_ref, g_ref, sh_ref, sc_ref, h_ref, o_ref, *, tm, n_lat, seq, n_batch):
    i = pl.program_id(0)
    grp = _group_of_tile(i, tm, n_lat, seq, n_batch)
    x = jnp.where(i * tm < n_lat, x_ref[...], c_ref[...])
    h_ref[...] = x
    y = x * lax.rsqrt(jnp.mean(x * x, axis=-1, keepdims=True) + NORM_EPS) * g_ref[...]
    o_ref[...] = (y * (1.0 + sc_ref[pl.ds(grp, 1), :]) + sh_ref[pl.ds(grp, 1), :]).astype(BF16)


def _join_norm_mod(x_lat, x_ctx, g, mod_l, shift_chunk, scale_chunk, dims):
    n_lat, seq, n_batch = dims
    d = x_lat.shape[1]
    t = n_lat + x_ctx.shape[0]
    tm = _tile(math.gcd(seq, t - n_lat), 512, 8)
    lat_tiles = n_lat // tm
    kern = functools.partial(_join_norm_kernel, tm=tm, n_lat=n_lat, seq=seq, n_batch=n_batch)
    return pl.pallas_call(
        kern, grid=(t // tm,),
        in_specs=[pl.BlockSpec((tm, d), lambda i: (jnp.minimum(i, lat_tiles - 1), 0)),
                  pl.BlockSpec((tm, d), lambda i: (jnp.maximum(i - lat_tiles, 0), 0)),
                  pl.BlockSpec((1, d), lambda i: (0, 0)),
                  pl.BlockSpec((8, d), lambda i: (0, shift_chunk)),
                  pl.BlockSpec((8, d), lambda i: (0, scale_chunk))],
        out_specs=[pl.BlockSpec((tm, d), lambda i: (i, 0)), pl.BlockSpec((tm, d), lambda i: (i, 0))],
        out_shape=[jax.ShapeDtypeStruct((t, d), F32), jax.ShapeDtypeStruct((t, d), BF16)],
        compiler_params=_params(1), name="join_norm_mod",
    )(x_lat, x_ctx, g.reshape(1, d), mod_l, mod_l)


def _final_norm_kernel(h_ref, g_ref, o_ref):
    x = h_ref[...]
    o_ref[...] = x * lax.rsqrt(jnp.mean(x * x, axis=-1, keepdims=True) + NORM_EPS) * g_ref[...]


def _final_norm(h, g, n_rows):
    d = h.shape[1]
    tm = _tile(n_rows, 256, 8)
    return pl.pallas_call(
        _final_norm_kernel, grid=(n_rows // tm,),
        in_specs=[pl.BlockSpec((tm, d), lambda i: (i, 0)), pl.BlockSpec((1, d), lambda i: (0, 0))],
        out_specs=pl.BlockSpec((tm, d), lambda i: (i, 0)),
        out_shape=jax.ShapeDtypeStruct((n_rows, d), F32),
        compiler_params=_params(1), name="final_norm",
    )(h, g.reshape(1, d))


def _cast_weights(first, w_refs, wb_refs):
    @pl.when(first)
    def _():
        for w_ref, wb_ref in zip(w_refs, wb_refs):
            wb_ref[...] = w_ref[...].astype(BF16)


IN_TN = 1024


def _in_layout(d_model):
    pieces = [("cq", 512, 512), ("ckv", 256, 256), ("krope", MLA_ROPE, LANES),
              ("swa_q", SWA_H * SWA_HD, LANES), ("swa_k", SWA_KV_H * SWA_HD, LANES),
              ("swa_v", SWA_KV_H * SWA_HD, LANES), ("na_q", NA_H * NA_HD, LANES), ("na_k", NA_H * NA_HD, LANES),
              ("na_v", NA_H * NA_HD, LANES), ("diff_q", DIFF_H * 2 * DIFF_DK, LANES),
              ("diff_k", DIFF_H * 2 * DIFF_DK, LANES), ("diff_v", DIFF_H * DIFF_DV, LANES),
              ("gates", N_BRANCH * d_model, 512)]
    cols, runs, nat, u = {}, [], 0, 0
    for name, width, align in pieces:
        u = -(-u // align) * align
        cols[name] = u
        if runs and runs[-1][0] + runs[-1][2] == u and runs[-1][1] + runs[-1][2] == nat:
            runs[-1][2] += width
        else:
            runs.append([u, nat, width])
        nat += width
        u += width
    width_u = -(-u // IN_TN) * IN_TN
    return cols, width_u, tuple(tuple(r) for r in runs)


def _pad_w_in(w_in, width_u, runs):
    parts, u = [], 0
    for u0, n0, w in runs:
        if u0 > u:
            parts.append(jnp.zeros(w_in.shape[:2] + (u0 - u,), BF16))
        parts.append(w_in[:, :, n0:n0 + w].astype(BF16))
        u = u0 + w
    if width_u > u:
        parts.append(jnp.zeros(w_in.shape[:2] + (width_u - u,), BF16))
    return jnp.concatenate(parts, axis=2)


def _proj_kernel(x_ref, w_ref, o_ref):
    o_ref[...] = _dot(x_ref[...], w_ref[...]).astype(o_ref.dtype)


def _project_in(xn, w_p, layer):
    t, d = xn.shape
    n = w_p.shape[2]
    tm = _tile(t, 512, 8)
    return pl.pallas_call(
        _proj_kernel, grid=(n // IN_TN, t // tm),
        in_specs=[pl.BlockSpec((tm, d), lambda j, i: (i, 0)),
                  pl.BlockSpec((None, d, IN_TN), lambda j, i: (layer, 0, j))],
        out_specs=pl.BlockSpec((tm, IN_TN), lambda j, i: (i, j)),
        out_shape=jax.ShapeDtypeStruct((t, n), BF16),
        compiler_params=_params(2), name="project_in",
    )(xn, w_p)


def _proj_res_kernel(x_ref, w_ref, h_ref, gate_ref, o_ref, wb_ref, *, tm, n_lat, seq, n_batch):
    i = pl.program_id(1)
    _cast_weights(i == 0, [w_ref], [wb_ref])
    grp = _group_of_tile(i, tm, n_lat, seq, n_batch)
    gate = gate_ref[pl.ds(grp, 1), :]
    o_ref[...] = h_ref[...] + gate * _dot(x_ref[...], wb_ref[...])


def _project_residual(x, w_l, layer, h, mod_l, gate_chunk, dims):
    n_lat, seq, n_batch = dims
    t, k = x.shape
    n = w_l.shape[2]
    tm = _tile(math.gcd(seq, t - n_lat), 512, 8)
    tn = _tile(n, 512, LANES)
    kern = functools.partial(_proj_res_kernel, tm=tm, n_lat=n_lat, seq=seq, n_batch=n_batch)
    gate_blk = gate_chunk * (n // tn)
    return pl.pallas_call(
        kern, grid=(n // tn, t // tm),
        in_specs=[pl.BlockSpec((tm, k), lambda j, i: (i, 0)),
                  pl.BlockSpec((None, k, tn), lambda j, i: (layer, 0, j)),
                  pl.BlockSpec((tm, tn), lambda j, i: (i, j)),
                  pl.BlockSpec((8, tn), lambda j, i: (0, gate_blk + j))],
        out_specs=pl.BlockSpec((tm, tn), lambda j, i: (i, j)),
        out_shape=jax.ShapeDtypeStruct(h.shape, F32),
        scratch_shapes=[pltpu.VMEM((k, tn), BF16)],
        input_output_aliases={2: 0},
        compiler_params=_params(2), name="project_residual",
    )(x, w_l, h, mod_l)


def _swiglu_up_kernel(x_ref, w1_ref, w3_ref, o_ref, w1b_ref, w3b_ref):
    _cast_weights(pl.program_id(1) == 0, [w1_ref, w3_ref], [w1b_ref, w3b_ref])
    x = x_ref[...]
    a = _dot(x, w1b_ref[...])
    b = _dot(x, w3b_ref[...])
    o_ref[...] = (_silu(a) * b).astype(BF16)


def _swiglu_up(xn, w1, w3, idx):
    t, d = xn.shape
    f = w1.shape[2]
    tm = _tile(t, 1024, 8) if t % 1024 == 0 else _tile(t, 512, 8)
    tn = _tile(f, 512, LANES)
    return pl.pallas_call(
        _swiglu_up_kernel, grid=(f // tn, t // tm),
        in_specs=[pl.BlockSpec((tm, d), lambda j, i: (i, 0)),
                  pl.BlockSpec((None, d, tn), lambda j, i: (idx, 0, j)),
                  pl.BlockSpec((None, d, tn), lambda j, i: (idx, 0, j))],
        out_specs=pl.BlockSpec((tm, tn), lambda j, i: (i, j)),
        out_shape=jax.ShapeDtypeStruct((t, f), BF16),
        scratch_shapes=[pltpu.VMEM((d, tn), BF16), pltpu.VMEM((d, tn), BF16)],
        compiler_params=_params(2), name="swiglu_up",
    )(xn, w1, w3)


def _merge_kernel(o_ref, g0_ref, g1_ref, g2_ref, g3_ref, w_ref, out_ref, wb_ref):
    _cast_weights(pl.program_id(1) == 0, [w_ref], [wb_ref])
    acc = None
    for m, g_ref in enumerate((g0_ref, g1_ref, g2_ref, g3_ref)):
        y = _sigmoid(g_ref[...].astype(F32)) * _dot(o_ref[:, m * BRANCH_W:(m + 1) * BRANCH_W], wb_ref[m])
        acc = y if acc is None else acc + y
    out_ref[...] = acc.astype(BF16)


def _merge(o, u, gate_col, w_branch, layer):
    t = o.shape[0]
    d = w_branch.shape[3]
    tm = _tile(t, 512, 8)
    tn = _tile(d, 512, LANES)
    nj = d // tn
    assert gate_col % tn == 0
    gate_blk = gate_col // tn

    def gate_spec(m):
        return pl.BlockSpec((tm, tn), lambda j, i: (i, gate_blk + m * nj + j))

    return pl.pallas_call(
        _merge_kernel, grid=(nj, t // tm),
        in_specs=[pl.BlockSpec((tm, N_BRANCH * BRANCH_W), lambda j, i: (i, 0)),
                  gate_spec(0), gate_spec(1), gate_spec(2), gate_spec(3),
                  pl.BlockSpec((None, N_BRANCH, BRANCH_W, tn), lambda j, i: (layer, 0, 0, j))],
        out_specs=pl.BlockSpec((tm, tn), lambda j, i: (i, j)),
        out_shape=jax.ShapeDtypeStruct((t, d), BF16),
        scratch_shapes=[pltpu.VMEM((N_BRANCH, BRANCH_W, tn), BF16)],
        compiler_params=_params(2), name="merge",
    )(o, u, u, u, u, w_branch)


def _rope_tables(n, dim):
    pos = jnp.arange(n, dtype=jnp.int32)
    row = (pos // GRID_W).astype(F32)
    col = (pos % GRID_W).astype(F32)
    quarter = dim // 4
    inv_freq = ROPE_THETA ** (-jnp.arange(quarter, dtype=F32) / quarter)
    ang_r = row[:, None] * inv_freq
    ang_c = col[:, None] * inv_freq
    ang = jnp.concatenate([ang_r, ang_r, ang_c, ang_c], axis=-1)
    reps = LANES // dim
    cos = jnp.tile(jnp.cos(ang), (1, reps))
    sin = jnp.tile(jnp.sin(ang), (1, reps))
    lane = jnp.arange(LANES)[None, :]
    first = (lane & quarter) == 0
    return cos, jnp.where(first, -sin, 0.0), jnp.where(first, 0.0, sin)


def _rope(x, cos, sin_a, sin_b, quarter):
    return (x * cos + pltpu.roll(x, LANES - quarter, 1) * sin_a + pltpu.roll(x, quarter, 1) * sin_b)


def _mla_prep_kernel(cq_ref, ckv_ref, kr_ref, qg_ref, kvg_ref, wq_ref, wkv_ref, cos_ref, sa_ref, sb_ref,
                     q_ref, k_ref, v_ref, *, n_lat_tiles):
    is_lat = pl.program_id(0) < n_lat_tiles
    cos = jnp.where(is_lat, cos_ref[...], 1.0)
    sa = jnp.where(is_lat, sa_ref[...], 0.0)
    sb = jnp.where(is_lat, sb_ref[...], 0.0)
    quarter = MLA_ROPE // 4

    def rms(x_ref, g_ref):
        x = x_ref[...].astype(F32)
        return (x * lax.rsqrt(jnp.mean(x * x, axis=-1, keepdims=True) + NORM_EPS) * g_ref[...]).astype(BF16)

    scale = (MLA_NOPE + MLA_ROPE) ** -0.5 * LOG2E
    q = _dot(rms(cq_ref, qg_ref), wq_ref[...].astype(BF16)) * scale
    kv = _dot(rms(ckv_ref, kvg_ref), wkv_ref[...].astype(BF16))
    kr = _rope(kr_ref[...].astype(F32), cos, sa, sb, quarter).astype(BF16)
    for hd in range(MLA_H):
        b = hd * 2 * LANES
        q_ref[:, b:b + LANES] = q[:, b:b + LANES].astype(BF16)
        q_ref[:, b + LANES:b + 2 * LANES] = _rope(q[:, b + LANES:b + 2 * LANES], cos, sa, sb, quarter).astype(BF16)
        k_ref[:, b:b + LANES] = kv[:, b:b + LANES].astype(BF16)
        k_ref[:, b + LANES:b + 2 * LANES] = kr
        v_ref[:, hd * LANES:(hd + 1) * LANES] = kv[:, b + LANES:b + 2 * LANES].astype(BF16)


def _mla_prep(u, cols, q_norm, kv_norm, w_uq_p, w_ukv, rope64, dims):
    n_lat, seq, _ = dims
    t = u.shape[0]
    tm = _tile(math.gcd(seq, t - n_lat), 256, 16)
    n_lat_tiles = n_lat // tm
    seq_tiles = seq // tm
    qw = MLA_H * 2 * LANES

    def tab_spec():
        return pl.BlockSpec((tm, LANES), lambda i: (jnp.where(i < n_lat_tiles, i % seq_tiles, 0), 0))

    kern = functools.partial(_mla_prep_kernel, n_lat_tiles=n_lat_tiles)
    return pl.pallas_call(
        kern, grid=(t // tm,),
        in_specs=[pl.BlockSpec((tm, 512), lambda i: (i, cols["cq"] // 512)),
                  pl.BlockSpec((tm, 256), lambda i: (i, cols["ckv"] // 256)),
                  pl.BlockSpec((tm, LANES), lambda i: (i, cols["krope"] // LANES)),
                  pl.BlockSpec((1, 512), lambda i: (0, 0)),
                  pl.BlockSpec((1, 256), lambda i: (0, 0)),
                  pl.BlockSpec((512, qw), lambda i: (0, 0)),
                  pl.BlockSpec((256, qw), lambda i: (0, 0)),
                  tab_spec(), tab_spec(), tab_spec()],
        out_specs=[pl.BlockSpec((tm, qw), lambda i: (i, 0)),
                   pl.BlockSpec((tm, qw), lambda i: (i, 0)),
                   pl.BlockSpec((tm, MLA_H * MLA_V), lambda i: (i, 0))],
        out_shape=[jax.ShapeDtypeStruct((t, qw), BF16),
                   jax.ShapeDtypeStruct((t, qw), BF16),
                   jax.ShapeDtypeStruct((t, MLA_H * MLA_V), BF16)],
        compiler_params=_params(1), name="mla_prep",
    )(u, u, u, q_norm.reshape(1, -1), kv_norm.reshape(1, -1), w_uq_p, w_ukv, *rope64)


def _softmax_pv(scores, values, sink=None, base2=False):
    ex = jnp.exp2 if base2 else jnp.exp
    m = None
    for s in scores:
        mi = jnp.max(s, axis=-1, keepdims=True)
        m = mi if m is None else jnp.maximum(m, mi)
    if sink is not None:
        m = jnp.maximum(m, sink)
    den = None
    acc = None
    for s, v in zip(scores, values):
        p = ex(s - m)
        li = jnp.sum(p, axis=-1, keepdims=True)
        den = li if den is None else den + li
        o = _dot(p.astype(BF16), v)
        acc = o if acc is None else acc + o
    if sink is not None:
        den = den + ex(sink - m)
    return acc / den


def _diff_lambda(lam_ref, lam_init):
    lp = lam_ref[...]
    return (jnp.exp(jnp.sum(lp[0:1, :] * lp[1:2, :], axis=-1, keepdims=True))
            - jnp.exp(jnp.sum(lp[2:3, :] * lp[3:4, :], axis=-1, keepdims=True)) + lam_init)


def _diff_finish(o1, o2, lam, g, lam_init):
    o = o1 - lam * o2
    y = o * lax.rsqrt(jnp.mean(o * o, axis=-1, keepdims=True) + NORM_EPS) * g
    return y * (1.0 - lam_init)


DENSE_SUB = 256
DENSE_CHUNK = 256


def _transpose_to(dst_ref, src_ref):
    dst_ref[...] = src_ref[...].astype(F32).T.astype(BF16)


def _ref_segment(k_ref, vt_ref):
    return (k_ref.shape[0], lambda c0, w: k_ref[c0:c0 + w, :], lambda c0, w: vt_ref[:, c0:c0 + w], None)


def _two_pass_softmax_pv(q, segments, s_ref, sink=None):
    rows = q.shape[0]
    m_run, off = None, 0
    for n, keys, _, bias in segments:
        for c0 in range(0, n, DENSE_CHUNK):
            w = min(DENSE_CHUNK, n - c0)
            s = _dot_nt(keys(c0, w), q)
            if bias is not None:
                s = s + bias(c0, w)
            s_ref[off + c0:off + c0 + w, :] = s
            part = jnp.max(s.reshape(w // 8, 8, rows), axis=0)
            m_run = part if m_run is None else jnp.maximum(m_run, part)
        off += n
    m = jnp.max(m_run, axis=0, keepdims=True)
    if sink is not None:
        m = jnp.maximum(m, sink)
    l_run, acc, off = None, None, 0
    for n, _, values_t, _ in segments:
        for c0 in range(0, n, DENSE_CHUNK):
            w = min(DENSE_CHUNK, n - c0)
            p = jnp.exp2(s_ref[off + c0:off + c0 + w, :] - m)
            part = jnp.sum(p.reshape(w // 8, 8, rows), axis=0)
            l_run = part if l_run is None else l_run + part
            pv = _dot(values_t(c0, w), p.astype(BF16))
            acc = pv if acc is None else acc + pv
        off += n
    den = jnp.sum(l_run, axis=0, keepdims=True)
    if sink is not None:
        den = den + jnp.exp2(sink - m)
    o_t = acc * (1.0 / den)
    return o_t.T


def _mla_attn_kernel(q_ref, kc_ref, kl_ref, vc_ref, vl_ref, oin_ref, o_ref, vtc_ref, vtl_ref, s_ref, *, sub):
    del oin_ref

    @pl.when(pl.program_id(2) == 0)
    def _():
        _transpose_to(vtc_ref, vc_ref)
        _transpose_to(vtl_ref, vl_ref)

    segments = [_ref_segment(kc_ref, vtc_ref), _ref_segment(kl_ref, vtl_ref)]

    def body(t, carry):
        r0 = pl.multiple_of(t * sub, sub)
        o = _two_pass_softmax_pv(q_ref[pl.ds(r0, sub), :], segments, s_ref)
        o_ref[pl.ds(r0, sub), :] = o.astype(BF16)
        return carry

    lax.fori_loop(0, q_ref.shape[0] // sub, body, 0)


def _mla_attention(qa, ka, va, o, dims, branch):
    n_lat, seq, n_batch = dims
    n_ctx = (qa.shape[0] - n_lat) // n_batch
    tq = _tile(seq, 1024, 16)
    sub = _tile(tq, DENSE_SUB, 16)
    nq = seq // tq
    ctx_blk0 = n_lat // n_ctx
    kw = 2 * LANES
    return pl.pallas_call(
        functools.partial(_mla_attn_kernel, sub=sub), grid=(n_batch, MLA_H, nq),
        scratch_shapes=[pltpu.VMEM((MLA_V, n_ctx), BF16), pltpu.VMEM((MLA_V, seq), BF16),
                        pltpu.VMEM((n_ctx + seq, sub), F32)],
        in_specs=[pl.BlockSpec((tq, kw), lambda b, h, i: (b * nq + i, h)),
                  pl.BlockSpec((n_ctx, kw), lambda b, h, i: (ctx_blk0 + b, h)),
                  pl.BlockSpec((seq, kw), lambda b, h, i: (b, h)),
                  pl.BlockSpec((n_ctx, LANES), lambda b, h, i: (ctx_blk0 + b, h)),
                  pl.BlockSpec((seq, LANES), lambda b, h, i: (b, h)),
                  pl.BlockSpec(memory_space=pl.ANY)],
        out_specs=pl.BlockSpec((tq, LANES), lambda b, h, i: (b * nq + i, branch * 4 + h)),
        out_shape=jax.ShapeDtypeStruct(o.shape, o.dtype),
        input_output_aliases={5: 0},
        compiler_params=_params(3), name="mla_attention",
    )(qa, ka, ka, va, va, o)


def _diff_attn_kernel(q_ref, kc_ref, kl_ref, vc_ref, vl_ref, cosq_ref, saq_ref, sbq_ref, cos_ref, sa_ref, sb_ref,
                      lam_ref, g_ref, oin_ref, o_ref, kr_ref, vtc_ref, vtl_ref, s0_ref, s1_ref, *, lam_init, sub):
    del oin_ref
    quarter = DIFF_DK // 4

    @pl.when(pl.program_id(2) == 0)
    def _():
        kr_ref[...] = _rope(kl_ref[...].astype(F32), cos_ref[...], sa_ref[...], sb_ref[...], quarter).astype(BF16)
        _transpose_to(vtc_ref, vc_ref)
        _transpose_to(vtl_ref, vl_ref)

    scale = DIFF_DK ** -0.5 * LOG2E
    segments = [_ref_segment(kc_ref, vtc_ref), _ref_segment(kr_ref, vtl_ref)]
    lam = _diff_lambda(lam_ref, lam_init)

    def body(t, carry):
        r0 = pl.multiple_of(t * sub, sub)
        rows = pl.ds(r0, sub)
        q = _rope(q_ref[rows, :].astype(F32) * scale, cosq_ref[rows, :], saq_ref[rows, :], sbq_ref[rows, :], quarter)
        lane = lax.broadcasted_iota(jnp.int32, q.shape, 1)
        q0 = jnp.where(lane < DIFF_DK, q, 0.0).astype(BF16)
        q1 = jnp.where(lane >= DIFF_DK, q, 0.0).astype(BF16)
        o0 = _two_pass_softmax_pv(q0, segments, s0_ref)
        o1 = _two_pass_softmax_pv(q1, segments, s1_ref)
        o_ref[rows, :] = _diff_finish(o0, o1, lam, g_ref[...], lam_init).astype(BF16)
        return carry

    lax.fori_loop(0, q_ref.shape[0] // sub, body, 0)


def _diff_attention(u, cols, rope64, diff_lambda_l, subln_g, lam_init, o, dims, branch):
    n_lat, seq, n_batch = dims
    n_ctx = (u.shape[0] - n_lat) // n_batch
    tq = _tile(seq, 1024, 16)
    sub = _tile(tq, DENSE_SUB, 16)
    nq = seq // tq
    ctx_blk0 = n_lat // n_ctx
    qb, kb, vb = cols["diff_q"] // LANES, cols["diff_k"] // LANES, cols["diff_v"] // LANES
    kern = functools.partial(_diff_attn_kernel, lam_init=lam_init, sub=sub)

    def tab_q():
        return pl.BlockSpec((tq, LANES), lambda b, h, i: (i, 0))

    def tab_k():
        return pl.BlockSpec((seq, LANES), lambda b, h, i: (0, 0))

    return pl.pallas_call(
        kern, grid=(n_batch, DIFF_H, nq),
        in_specs=[pl.BlockSpec((tq, LANES), lambda b, h, i: (b * nq + i, qb + h)),
                  pl.BlockSpec((n_ctx, LANES), lambda b, h, i: (ctx_blk0 + b, kb + h)),
                  pl.BlockSpec((seq, LANES), lambda b, h, i: (b, kb + h)),
                  pl.BlockSpec((n_ctx, LANES), lambda b, h, i: (ctx_blk0 + b, vb + h)),
                  pl.BlockSpec((seq, LANES), lambda b, h, i: (b, vb + h)),
                  tab_q(), tab_q(), tab_q(), tab_k(), tab_k(), tab_k(),
                  pl.BlockSpec((4, DIFF_DK), lambda b, h, i: (0, 0)),
                  pl.BlockSpec((1, DIFF_DV), lambda b, h, i: (0, 0)),
                  pl.BlockSpec(memory_space=pl.ANY)],
        out_specs=pl.BlockSpec((tq, LANES), lambda b, h, i: (b * nq + i, branch * 4 + h)),
        out_shape=jax.ShapeDtypeStruct(o.shape, o.dtype),
        scratch_shapes=[pltpu.VMEM((seq, LANES), BF16),
                        pltpu.VMEM((DIFF_DV, n_ctx), BF16), pltpu.VMEM((DIFF_DV, seq), BF16),
                        pltpu.VMEM((n_ctx + seq, sub), F32), pltpu.VMEM((n_ctx + seq, sub), F32)],
        input_output_aliases={13: 0},
        compiler_params=_params(3), name="diff_attention",
    )(u, u, u, u, u, *rope64, *rope64, diff_lambda_l, subln_g.reshape(1, -1), o)


def _local_attn_kernel(*refs, tq, nk, back, seq, sub, use_rope, use_sink):
    it = iter(refs)
    q_ref, kl_ref, vl_ref, kc_ref, vc_ref, bias_ref = (next(it) for _ in range(6))
    if use_rope:
        cosq_ref, saq_ref, sbq_ref, cos_ref, sa_ref, sb_ref = (next(it) for _ in range(6))
    if use_sink:
        sink_ref = next(it)
    oin_ref, o_ref = next(it), next(it)
    del oin_ref
    if use_rope:
        kr_ref = next(it)
    vtc_ref, vtw_ref, s_ref = next(it), next(it), next(it)
    i = pl.program_id(2)
    scale = LANES ** -0.5 * LOG2E
    quarter = SWA_HD // 4

    @pl.when(i == 0)
    def _():
        _transpose_to(vtc_ref, vc_ref)
        if use_rope:
            kr_ref[...] = _rope(kl_ref[...].astype(F32), cos_ref[...], sa_ref[...], sb_ref[...],
                                quarter).astype(BF16)

    k_src = kr_ref if use_rope else kl_ref
    ks = pl.multiple_of(jnp.clip(i * tq - back, 0, seq - nk), LANES)
    vtw_ref[...] = vl_ref[pl.ds(ks, nk), :].astype(F32).T.astype(BF16)
    sink = sink_ref[pl.program_id(1)] * LOG2E if use_sink else None
    for r0 in range(0, tq, sub):
        q = q_ref[r0:r0 + sub, :].astype(F32) * scale
        if use_rope:
            q = _rope(q, cosq_ref[r0:r0 + sub, :], saq_ref[r0:r0 + sub, :], sbq_ref[r0:r0 + sub, :], quarter)
        window = (nk, lambda c0, w: k_src[pl.ds(ks + c0, w), :], lambda c0, w: vtw_ref[:, c0:c0 + w],
                  lambda c0, w, r0=r0: bias_ref[c0:c0 + w, r0:r0 + sub])
        o = _two_pass_softmax_pv(q.astype(BF16), [_ref_segment(kc_ref, vtc_ref), window], s_ref, sink=sink)
        o_ref[r0:r0 + sub, :] = o.astype(BF16)


def _local_attention(u, qcol, kcol, vcol, n_heads, n_kv, bias_plan, rope, sink, o, dims, branch, name):
    bias, tile_map, nk, back = bias_plan

    def bias_tile(i):
        idx = tile_map[-1]
        for t in range(len(tile_map) - 2, -1, -1):
            idx = jnp.where(i == t, tile_map[t], idx)
        return idx

    n_lat, seq, n_batch = dims
    n_ctx = (u.shape[0] - n_lat) // n_batch
    tq = bias.shape[3]
    sub = _tile(tq, DENSE_SUB, LANES)
    nq = seq // tq
    grp = n_heads // n_kv
    ctx_blk0 = n_lat // n_ctx
    qb, kb, vb = qcol // LANES, kcol // LANES, vcol // LANES
    per_head_bias = bias.shape[0] > 1
    use_rope, use_sink = rope is not None, sink is not None
    kern = functools.partial(_local_attn_kernel, tq=tq, nk=nk, back=back, seq=seq, sub=sub,
                             use_rope=use_rope, use_sink=use_sink)
    in_specs = [pl.BlockSpec((tq, LANES), lambda b, h, i: (b * nq + i, qb + h)),
                pl.BlockSpec((seq, LANES), lambda b, h, i: (b, kb + h // grp)),
                pl.BlockSpec((seq, LANES), lambda b, h, i: (b, vb + h // grp)),
                pl.BlockSpec((n_ctx, LANES), lambda b, h, i: (ctx_blk0 + b, kb + h // grp)),
                pl.BlockSpec((n_ctx, LANES), lambda b, h, i: (ctx_blk0 + b, vb + h // grp)),
                pl.BlockSpec((None, None, nk, tq),
                             lambda b, h, i: (h if per_head_bias else 0, bias_tile(i), 0, 0))]
    args = [u, u, u, u, u, bias]
    scratch = []
    if use_rope:
        in_specs += [pl.BlockSpec((tq, LANES), lambda b, h, i: (i, 0))] * 3
        in_specs += [pl.BlockSpec((seq, LANES), lambda b, h, i: (0, 0))] * 3
        args += [*rope, *rope]
        scratch.append(pltpu.VMEM((seq, LANES), BF16))
    scratch += [pltpu.VMEM((LANES, n_ctx), BF16), pltpu.VMEM((LANES, nk), BF16),
                pltpu.VMEM((n_ctx + nk, sub), F32)]
    if use_sink:
        in_specs.append(pl.BlockSpec(memory_space=pltpu.SMEM))
        args.append(sink)
    in_specs.append(pl.BlockSpec(memory_space=pl.ANY))
    args.append(o)
    return pl.pallas_call(
        kern, grid=(n_batch, n_heads, nq), in_specs=in_specs,
        out_specs=pl.BlockSpec((tq, LANES), lambda b, h, i: (b * nq + i, branch * 4 + h)),
        out_shape=jax.ShapeDtypeStruct(o.shape, o.dtype),
        scratch_shapes=scratch,
        input_output_aliases={len(args) - 1: 0},
        compiler_params=_params(3), name=name,
    )(*args)


def _local_tiling(seq, span):
    tq = _tile(seq, 512, 64)
    nk = min(seq, tq + 2 * span)
    return tq, nk


def _dedupe(keys):
    first, uniq, tile_map = {}, [], []
    for i, key in enumerate(keys):
        if key not in first:
            first[key] = len(uniq)
            uniq.append(i)
        tile_map.append(first[key])
    return uniq, tuple(tile_map)


def _swa_bias(seq):
    tq, nk = _local_tiling(seq, SWA_WINDOW)
    nq = seq // tq
    tiles = []
    for i in range(nq):
        ks = min(max(i * tq - SWA_WINDOW, 0), seq - nk)
        qp = i * tq + np.arange(tq)[:, None]
        kp = ks + np.arange(nk)[None, :]
        tiles.append(np.where(np.abs(qp - kp) <= SWA_WINDOW, 0.0, NEG_INF).astype(np.float32).T.copy())
    uniq, tile_map = _dedupe([t.tobytes() for t in tiles])
    return jnp.asarray(np.stack([tiles[i] for i in uniq])[None], F32), tile_map, nk, SWA_WINDOW


def _na_bias_kernel(rpb_ref, out_ref, toe_ref, *, row_offsets):
    n_d, n_c = 2 * NA_KH - 1, 2 * NA_KW - 1
    hd = pl.program_id(0)
    shape = (GRID_W, 2 * GRID_W)
    kc = lax.broadcasted_iota(jnp.int32, shape, 0)
    lane = lax.broadcasted_iota(jnp.int32, shape, 1)
    qc = lane & (GRID_W - 1)
    col_start = jnp.clip(qc - NA_KW // 2, 0, GRID_W - NA_KW)
    valid = (kc >= col_start) & (kc < col_start + NA_KW)
    dc = jnp.clip(kc - qc, -(NA_KW - 1), NA_KW - 1) + (NA_KW - 1)
    for d in range(n_d):
        blk = jnp.zeros(shape, F32)
        for j in range(n_c):
            blk = jnp.where(dc == j, rpb_ref[(hd * n_d + d) * n_c + j] * LOG2E, blk)
        toe_ref[d] = jnp.where(valid, blk, NEG_INF)
    toe_ref[n_d] = jnp.full(shape, NEG_INF, F32)
    left = lane < GRID_W
    for u, offs in enumerate(row_offsets):
        for a in range(0, len(offs), 2):
            for b in range(len(offs[a])):
                out_ref[u, b * GRID_W:(b + 1) * GRID_W, a * GRID_W:(a + 2) * GRID_W] = jnp.where(
                    left, toe_ref[offs[a][b]], toe_ref[offs[a + 1][b]])


def _na_bias(rpb, seq):
    rows = seq // GRID_W
    kh = min(NA_KH, rows)
    back_rows = kh // 2
    tq, nk = _local_tiling(seq, back_rows * GRID_W)
    nq, qr_n, kr_n = seq // tq, tq // GRID_W, nk // GRID_W
    assert qr_n % 2 == 0
    n_heads = rpb.shape[0]
    n_d = 2 * NA_KH - 1
    d_idx = []
    for i in range(nq):
        ks_row = min(max(i * qr_n - back_rows, 0), rows - kr_n)
        qr = i * qr_n + np.arange(qr_n)[:, None]
        kr = ks_row + np.arange(kr_n)[None, :]
        start = np.clip(qr - kh // 2, 0, rows - kh)
        ok = (kr >= start) & (kr < start + kh)
        d_idx.append(np.where(ok, kr - qr + (NA_KH - 1), n_d))
    uniq, tile_map = _dedupe([d.tobytes() for d in d_idx])
    row_offsets = tuple(tuple(tuple(int(v) for v in row) for row in d_idx[i]) for i in uniq)
    bias = pl.pallas_call(
        functools.partial(_na_bias_kernel, row_offsets=row_offsets), grid=(n_heads,),
        in_specs=[pl.BlockSpec(memory_space=pltpu.SMEM)],
        out_specs=pl.BlockSpec((None, len(uniq), nk, tq), lambda hd: (hd, 0, 0, 0)),
        out_shape=jax.ShapeDtypeStruct((n_heads, len(uniq), nk, tq), F32),
        scratch_shapes=[pltpu.VMEM((n_d + 1, GRID_W, 2 * GRID_W), F32)],
        compiler_params=_params(1), name="na_bias",
    )(rpb.reshape(-1))
    return bias, tile_map, nk, back_rows * GRID_W


def _ctx_attn_kernel(u_ref, qa_ref, ka_ref, va_ref, sink_ref, lam_ref, g_ref, oin_ref, o_ref, *, cols, lam_init):
    del oin_ref

    def col(name, hd, width=LANES):
        c0 = cols[name] + hd * width
        return u_ref[:, c0:c0 + width]

    for hd in range(MLA_H):
        q = qa_ref[:, hd * 2 * LANES:(hd + 1) * 2 * LANES]
        k = ka_ref[:, hd * 2 * LANES:(hd + 1) * 2 * LANES]
        v = va_ref[:, hd * LANES:(hd + 1) * LANES]
        o_ref[:, hd * LANES:(hd + 1) * LANES] = _softmax_pv([_dot_nt(q, k)], [v], base2=True).astype(BF16)
    grp = SWA_H // SWA_KV_H
    scale = SWA_HD ** -0.5
    for hd in range(SWA_H):
        q = (col("swa_q", hd).astype(F32) * scale).astype(BF16)
        o = _softmax_pv([_dot_nt(q, col("swa_k", hd // grp))], [col("swa_v", hd // grp)], sink=sink_ref[hd])
        o_ref[:, BRANCH_W + hd * LANES:BRANCH_W + (hd + 1) * LANES] = o.astype(BF16)
    scale = NA_HD ** -0.5
    for hd in range(NA_H):
        q = (col("na_q", hd).astype(F32) * scale).astype(BF16)
        o = _softmax_pv([_dot_nt(q, col("na_k", hd))], [col("na_v", hd)])
        o_ref[:, 2 * BRANCH_W + hd * LANES:2 * BRANCH_W + (hd + 1) * LANES] = o.astype(BF16)
    scale = DIFF_DK ** -0.5
    lam = _diff_lambda(lam_ref, lam_init)
    for hd in range(DIFF_H):
        q = col("diff_q", hd).astype(F32) * scale
        k = col("diff_k", hd)
        v = col("diff_v", hd)
        lane = lax.broadcasted_iota(jnp.int32, q.shape, 1)
        outs = []
        for j in range(2):
            qj = jnp.where((lane >= j * DIFF_DK) & (lane < (j + 1) * DIFF_DK), q, 0.0).astype(BF16)
            outs.append(_softmax_pv([_dot_nt(qj, k)], [v]))
        o = _diff_finish(outs[0], outs[1], lam, g_ref[...], lam_init)
        o_ref[:, 3 * BRANCH_W + hd * LANES:3 * BRANCH_W + (hd + 1) * LANES] = o.astype(BF16)


def _ctx_attention(u, cols, qa, ka, va, sink, diff_lambda_l, subln_g, lam_init, o, dims):
    n_lat, seq, n_batch = dims
    n_ctx = (u.shape[0] - n_lat) // n_batch
    blk0 = n_lat // n_ctx
    kern = functools.partial(_ctx_attn_kernel, cols=cols, lam_init=lam_init)
    return pl.pallas_call(
        kern, grid=(n_batch,),
        in_specs=[pl.BlockSpec((n_ctx, u.shape[1]), lambda b: (blk0 + b, 0)),
                  pl.BlockSpec((n_ctx, qa.shape[1]), lambda b: (blk0 + b, 0)),
                  pl.BlockSpec((n_ctx, ka.shape[1]), lambda b: (blk0 + b, 0)),
                  pl.BlockSpec((n_ctx, va.shape[1]), lambda b: (blk0 + b, 0)),
                  pl.BlockSpec(memory_space=pltpu.SMEM),
                  pl.BlockSpec((4, DIFF_DK), lambda b: (0, 0)),
                  pl.BlockSpec((1, DIFF_DV), lambda b: (0, 0)),
                  pl.BlockSpec(memory_space=pl.ANY)],
        out_specs=pl.BlockSpec((n_ctx, o.shape[1]), lambda b: (blk0 + b, 0)),
        out_shape=jax.ShapeDtypeStruct(o.shape, o.dtype),
        input_output_aliases={7: 0},
        compiler_params=_params(1), name="ctx_attention",
    )(u, qa, ka, va, sink, diff_lambda_l, subln_g.reshape(1, -1), o)


MOE_TM = 512
GATHER_ROWS = 256


def _route_plan(top_i, tm):
    t = top_i.shape[0]
    n_assign = t * TOP_K
    n_tiles = (n_assign + N_EXPERTS * (tm - 1)) // tm
    e_flat = top_i.reshape(-1)
    onehot = (e_flat[:, None] == jnp.arange(N_EXPERTS)[None, :]).astype(jnp.int32)
    csum = jnp.cumsum(onehot, axis=0)
    rank = jnp.sum(onehot * csum, axis=1) - 1
    counts = csum[-1]
    padded = ((counts + tm - 1) // tm) * tm
    ends = jnp.cumsum(padded)
    offs = ends - padded
    pos = (jnp.sum(onehot * offs[None, :], axis=1) + rank).astype(jnp.int32)
    src = jnp.zeros((n_tiles * tm,), jnp.int32).at[pos].set(
        jnp.arange(n_assign, dtype=jnp.int32) // TOP_K, unique_indices=True, mode="promise_in_bounds")
    tile_start = jnp.arange(n_tiles, dtype=jnp.int32) * tm
    tile_expert = jnp.minimum(jnp.sum((tile_start[:, None] >= ends[None, :]).astype(jnp.int32), axis=1),
                              N_EXPERTS - 1).astype(jnp.int32)
    tile_valid = (tile_start < ends[-1]).astype(jnp.int32)
    return pos, src, tile_expert, tile_valid


def _row_copies(idx_ref, base, src_hbm, bufs, sems, slot, n_rows, idx_stride):
    def body(r, carry):
        for s, buf in enumerate(bufs):
            row = idx_ref[(base + r) * idx_stride + s]
            pltpu.make_async_copy(src_hbm.at[pl.ds(row, 1), :], buf.at[slot, pl.ds(r, 1), :],
                                  sems.at[slot, s]).start()
        return carry
    lax.fori_loop(0, n_rows, body, 0, unroll=8)


def _wait_rows(src_hbm, bufs, sems, slot, n_rows):
    for s, buf in enumerate(bufs):
        pltpu.make_async_copy(src_hbm.at[pl.ds(0, n_rows), :], buf.at[slot], sems.at[slot, s]).wait()


def _gather_kernel(src_ref, tv_ref, x_hbm, o_ref, x_ref, buf_ref, sem, *, rows):
    i = pl.program_id(0)

    @pl.when(i == 0)
    def _():
        cp = pltpu.make_async_copy(x_hbm, x_ref, sem)
        cp.start()
        cp.wait()

    @pl.when(tv_ref[i] == 1)
    def _():
        def body(r, carry):
            buf_ref[pl.ds(r, 1), :] = x_ref[pl.ds(src_ref[i * rows + r], 1), :]
            return carry
        lax.fori_loop(0, rows, body, 0, unroll=8)
        w = buf_ref[...]
        half = w.shape[1]
        o_ref[:, :half] = lax.bitcast_convert_type(w << 16, F32).astype(BF16)
        o_ref[:, half:] = lax.bitcast_convert_type(w & jnp.uint32(0xFFFF0000), F32).astype(BF16)

    @pl.when(tv_ref[i] == 0)
    def _():
        o_ref[...] = jnp.zeros(o_ref.shape, BF16)


def _gather_rows(xp, src, tile_valid, rows):
    t, half = xp.shape
    n_rows = src.shape[0]
    kern = functools.partial(_gather_kernel, rows=rows)
    return pl.pallas_call(
        kern,
        grid_spec=pltpu.PrefetchScalarGridSpec(
            num_scalar_prefetch=2, grid=(n_rows // rows,),
            in_specs=[pl.BlockSpec(memory_space=pl.ANY)],
            out_specs=pl.BlockSpec((rows, 2 * half), lambda i, s, v: (i, 0)),
            scratch_shapes=[pltpu.VMEM((t, half), jnp.uint32), pltpu.VMEM((rows, half), jnp.uint32),
                            pltpu.SemaphoreType.DMA(())]),
        out_shape=jax.ShapeDtypeStruct((n_rows, 2 * half), BF16),
        compiler_params=_params(1), name="moe_gather",
    )(src, tile_valid, xp)


def _expert_changed(te_ref, i):
    return jnp.logical_or(i == 0, te_ref[i] != te_ref[jnp.maximum(i - 1, 0)])


def _gmm_up_kernel(te_ref, tv_ref, x_ref, w1_ref, w3_ref, o_ref, w1b_ref, w3b_ref):
    i = pl.program_id(1)
    _cast_weights(_expert_changed(te_ref, i), [w1_ref, w3_ref], [w1b_ref, w3b_ref])

    @pl.when(tv_ref[i] == 1)
    def _():
        x = x_ref[...]
        o_ref[...] = (_silu(_dot(x, w1b_ref[...])) * _dot(x, w3b_ref[...])).astype(BF16)

    @pl.when(tv_ref[i] == 0)
    def _():
        o_ref[...] = jnp.zeros(o_ref.shape, BF16)


def _gmm_up(xs, w1, w3, idx, tile_expert, tile_valid, tm):
    r, d = xs.shape
    f = w1.shape[3]
    tn = _tile(f, 512, LANES)
    return pl.pallas_call(
        _gmm_up_kernel,
        grid_spec=pltpu.PrefetchScalarGridSpec(
            num_scalar_prefetch=2, grid=(f // tn, r // tm),
            in_specs=[pl.BlockSpec((tm, d), lambda j, i, te, tv: (i, 0)),
                      pl.BlockSpec((None, None, d, tn), lambda j, i, te, tv: (idx, te[i], 0, j)),
                      pl.BlockSpec((None, None, d, tn), lambda j, i, te, tv: (idx, te[i], 0, j))],
            out_specs=pl.BlockSpec((tm, tn), lambda j, i, te, tv: (i, j)),
            scratch_shapes=[pltpu.VMEM((d, tn), BF16), pltpu.VMEM((d, tn), BF16)]),
        out_shape=jax.ShapeDtypeStruct((r, f), BF16),
        compiler_params=_params(2), name="moe_up",
    )(tile_expert, tile_valid, xs, w1, w3)


def _gmm_down_kernel(te_ref, tv_ref, a_ref, w_ref, o_ref, wb_ref):
    i = pl.program_id(1)
    _cast_weights(_expert_changed(te_ref, i), [w_ref], [wb_ref])

    @pl.when(tv_ref[i] == 1)
    def _():
        o_ref[...] = _dot(a_ref[...], wb_ref[...])

    @pl.when(tv_ref[i] == 0)
    def _():
        o_ref[...] = jnp.zeros(o_ref.shape, F32)


def _gmm_down(a, w2, idx, tile_expert, tile_valid, tm):
    r, f = a.shape
    d = w2.shape[3]
    tn = _tile(d, 512, LANES)
    return pl.pallas_call(
        _gmm_down_kernel,
        grid_spec=pltpu.PrefetchScalarGridSpec(
            num_scalar_prefetch=2, grid=(d // tn, r // tm),
            in_specs=[pl.BlockSpec((tm, f), lambda j, i, te, tv: (i, 0)),
                      pl.BlockSpec((None, None, f, tn), lambda j, i, te, tv: (idx, te[i], 0, j))],
            out_specs=pl.BlockSpec((tm, tn), lambda j, i, te, tv: (i, j)),
            scratch_shapes=[pltpu.VMEM((f, tn), BF16)]),
        out_shape=jax.ShapeDtypeStruct((r, d), F32),
        compiler_params=_params(2), name="moe_down",
    )(tile_expert, tile_valid, a, w2)


def _combine_kernel(pos_ref, y_hbm, h_ref, tg_ref, gate_ref, o_ref, buf0, buf1, sems, *, rows, n_lat, seq, n_batch):
    i = pl.program_id(0)
    n = pl.num_programs(0)
    slot = i % 2
    bufs = [buf0, buf1]

    @pl.when(i == 0)
    def _():
        _row_copies(pos_ref, 0, y_hbm, bufs, sems, 0, rows, TOP_K)

    @pl.when(i + 1 < n)
    def _():
        _row_copies(pos_ref, (i + 1) * rows, y_hbm, bufs, sems, 1 - slot, rows, TOP_K)

    _wait_rows(y_hbm, bufs, sems, slot, rows)
    grp = _group_of_tile(i, rows, n_lat, seq, n_batch)
    gate = gate_ref[pl.ds(grp, 1), :]
    tg = tg_ref[...]
    mix = tg[:, 0:1] * buf0[slot] + tg[:, 1:2] * buf1[slot]
    o_ref[...] = h_ref[...] + gate * mix


def _combine(h, y, pos, top_g, mod_l, gate_chunk, dims):
    n_lat, seq, n_batch = dims
    t, d = h.shape
    rows = _tile(math.gcd(seq, t - n_lat), GATHER_ROWS, 8)
    kern = functools.partial(_combine_kernel, rows=rows, n_lat=n_lat, seq=seq, n_batch=n_batch)
    return pl.pallas_call(
        kern,
        grid_spec=pltpu.PrefetchScalarGridSpec(
            num_scalar_prefetch=1, grid=(t // rows,),
            in_specs=[pl.BlockSpec(memory_space=pl.ANY),
                      pl.BlockSpec((rows, d), lambda i, p: (i, 0)),
                      pl.BlockSpec((rows, LANES), lambda i, p: (i, 0)),
                      pl.BlockSpec((8, d), lambda i, p: (0, gate_chunk))],
            out_specs=pl.BlockSpec((rows, d), lambda i, p: (i, 0)),
            scratch_shapes=[pltpu.VMEM((2, rows, d), F32), pltpu.VMEM((2, rows, d), F32),
                            pltpu.SemaphoreType.DMA((2, 2))]),
        out_shape=jax.ShapeDtypeStruct(h.shape, F32),
        input_output_aliases={2: 0},
        compiler_params=_params(1), name="moe_combine",
    )(pos, y, h, top_g, mod_l)


def kernel(x, c, ctx, c_ctx, mod_w, mod_b, norm1_g, norm2_g, w_in, mla_q_norm, mla_kv_norm, mla_w_uq, mla_w_ukv,
           swa_sink, na_rpb, diff_lambda, diff_subln_g, w_branch, w_out, ffn_w1, ffn_w3, ffn_w2, moe_router,
           moe_w1, moe_w3, moe_w2, final_norm_g):
    n_batch, seq, d = x.shape
    n_ctx = ctx.shape[1]
    depth = mod_w.shape[0]
    n_lat = n_batch * seq
    dims = (n_lat, seq, n_batch)

    t = n_lat + n_batch * n_ctx
    h = None
    c8 = jnp.concatenate([c, c_ctx[None, :], jnp.zeros((8 - n_batch - 1, d), F32)], axis=0)
    mod = _mod_all(c8, mod_w, mod_b)

    cols, width_u, in_runs = _in_layout(d)
    w_in_p = _pad_w_in(w_in, width_u, in_runs)
    w_uq = mla_w_uq.reshape(depth, -1, MLA_H, MLA_NOPE + MLA_ROPE)
    w_uq_p = jnp.pad(w_uq, ((0, 0), (0, 0), (0, 0), (0, 2 * LANES - MLA_NOPE - MLA_ROPE)))
    w_uq_p = w_uq_p.reshape(depth, -1, MLA_H * 2 * LANES)
    rope64 = _rope_tables(seq, MLA_ROPE)
    rope128 = _rope_tables(seq, SWA_HD)
    swa_plan = _swa_bias(seq)
    router_p = jnp.pad(moe_router, ((0, 0), (0, 0), (0, LANES - N_EXPERTS)))

    for l in range(depth):
        need_ctx = l < depth - 1
        lam_init = 0.8 - 0.6 * math.exp(-0.3 * l)
        mod_l = mod[l]

        if h is None:
            h, xn = _join_norm_mod(x.reshape(n_lat, d), ctx.reshape(n_batch * n_ctx, d), norm1_g[l], mod_l, 0, 1, dims)
        else:
            xn = _norm_mod(h, norm1_g[l], mod_l, 0, 1, dims)
        u = _project_in(xn, w_in_p, l)

        qa, ka, va = _mla_prep(u, cols, mla_q_norm[l], mla_kv_norm[l], w_uq_p[l], mla_w_ukv[l], rope64, dims)
        o = jnp.zeros((t, N_BRANCH * BRANCH_W), BF16)
        o = _mla_attention(qa, ka, va, o, dims, 0)
        o = _local_attention(u, cols["swa_q"], cols["swa_k"], cols["swa_v"], SWA_H, SWA_KV_H, swa_plan,
                             rope128, swa_sink[l], o, dims, 1, "swa_attention")
        o = _local_attention(u, cols["na_q"], cols["na_k"], cols["na_v"], NA_H, NA_H, _na_bias(na_rpb[l], seq),
                             None, None, o, dims, 2, "na_attention")
        o = _diff_attention(u, cols, rope64, diff_lambda[l], diff_subln_g[l], lam_init, o, dims, 3)
        if need_ctx:
            o = _ctx_attention(u, cols, qa, ka, va, swa_sink[l], diff_lambda[l], diff_subln_g[l], lam_init, o, dims)

        merged = _merge(o, u, cols["gates"], w_branch, l)
        h = _project_residual(merged, w_out, l, h, mod_l, 2, dims)

        i = l // 2
        if l % 2 == 0:
            xn = _norm_mod(h, norm2_g[l], mod_l, 3, 4, dims)
            a = _swiglu_up(xn, ffn_w1, ffn_w3, i)
            h = _project_residual(a, ffn_w2, i, h, mod_l, 5, dims)
        else:
            xn, top_g, top_i = _norm_mod(h, norm2_g[l], mod_l, 3, 4, dims, router=router_p[i])
            pos, src, tile_expert, tile_valid = _route_plan(top_i[:, :TOP_K], MOE_TM)
            xs = _gather_rows(xn, src, tile_valid, MOE_TM)
            a = _gmm_up(xs, moe_w1, moe_w3, i, tile_expert, tile_valid, MOE_TM)
            y = _gmm_down(a, moe_w2, i, tile_expert, tile_valid, MOE_TM)
            h = _combine(h, y, pos, top_g, mod_l, 5, dims)

    out = _final_norm(h, final_norm_g, n_lat)
    return out.reshape(n_batch, seq, d)
```

```python
import functools
import math

import jax
import jax.numpy as jnp
import numpy as np
from jax import lax
from jax.experimental import pallas as pl
from jax.experimental.pallas import tpu as pltpu

F32 = jnp.float32
BF16 = jnp.bfloat16

GRID_W = 64
ROPE_THETA = 10000.0
NORM_EPS = 1e-6
NEG_INF = -1e30
LOG2E = math.log2(math.e)

MLA_H, MLA_NOPE, MLA_ROPE, MLA_V = 4, 128, 64, 128
SWA_H, SWA_KV_H, SWA_HD, SWA_WINDOW = 4, 2, 128, 128
NA_H, NA_HD, NA_KH, NA_KW = 4, 128, 8, 16
DIFF_H, DIFF_DK, DIFF_DV = 4, 64, 128
N_BRANCH, BRANCH_W = 4, 512
N_EXPERTS, TOP_K = 8, 2
LANES = 128

VMEM_LIMIT = 56 * 1024 * 1024


def _params(n_axes):
    return pltpu.CompilerParams(dimension_semantics=("arbitrary",) * n_axes, vmem_limit_bytes=VMEM_LIMIT)


def _tile(n, pref, mult):
    if n <= pref:
        return n
    t = (pref // mult) * mult
    while t > mult and n % t:
        t -= mult
    assert n % t == 0, (n, pref, mult)
    return t


def _dot(a, b):
    return jnp.dot(a, b, preferred_element_type=F32)


def _dot_nt(a, b):
    return lax.dot_general(a, b, (((1,), (1,)), ((), ())), preferred_element_type=F32)


def _silu(x):
    return x * (1.0 / (1.0 + jnp.exp(-x)))


def _sigmoid(x):
    return 0.5 * (1.0 + jnp.tanh(0.5 * x))


def _mod_kernel(c_ref, w_ref, b_ref, o_ref):
    a = _silu(c_ref[...]).astype(BF16)
    o_ref[...] = _dot(a, w_ref[...].astype(BF16)) + b_ref[...]


def _mod_all(c8, mod_w, mod_b):
    n_l, d, n = mod_w.shape
    tn = _tile(n, 1024, LANES)
    return pl.pallas_call(
        _mod_kernel,
        grid=(n_l, n // tn),
        in_specs=[
            pl.BlockSpec((8, d), lambda l, j: (0, 0)),
            pl.BlockSpec((None, d, tn), lambda l, j: (l, 0, j)),
            pl.BlockSpec((None, 1, tn), lambda l, j: (l, 0, j)),
        ],
        out_specs=pl.BlockSpec((None, 8, tn), lambda l, j: (l, 0, j)),
        out_shape=jax.ShapeDtypeStruct((n_l, 8, n), F32),
        compiler_params=_params(2),
        name="mod_all",
    )(c8, mod_w, mod_b.reshape(n_l, 1, n))


def _group_of_tile(i, tm, n_lat, seq, n_batch):
    row0 = i * tm
    return jnp.where(row0 < n_lat, row0 // seq, n_batch)


def _norm_kernel(h_ref, g_ref, sh_ref, sc_ref, *rest, tm, n_lat, seq, n_batch, route):
    grp = _group_of_tile(pl.program_id(0), tm, n_lat, seq, n_batch)
    x = h_ref[...]
    y = x * lax.rsqrt(jnp.mean(x * x, axis=-1, keepdims=True) + NORM_EPS) * g_ref[...]
    sh = sh_ref[pl.ds(grp, 1), :]
    sc = sc_ref[pl.ds(grp, 1), :]
    xn = y * (1.0 + sc) + sh
    if not route:
        (o_ref,) = rest
        o_ref[...] = xn.astype(BF16)
        return
    r_ref, o_ref, tg_ref, ti_ref = rest
    bits = lax.bitcast_convert_type(xn.astype(BF16).astype(F32), jnp.uint32)
    half = bits.shape[1] // 2
    o_ref[...] = (bits[:, half:] & jnp.uint32(0xFFFF0000)) | (bits[:, :half] >> 16)
    logits = jnp.dot(xn, r_ref[...], preferred_element_type=F32, precision=lax.Precision.HIGHEST)
    lane = lax.broadcasted_iota(jnp.int32, logits.shape, 1).astype(F32)
    logits = jnp.where(lane < N_EXPERTS, logits, -jnp.inf)
    m1 = jnp.max(logits, axis=-1, keepdims=True)
    i1 = jnp.min(jnp.where(logits == m1, lane, float(LANES)), axis=-1, keepdims=True)
    rest_l = jnp.where(lane == i1, -jnp.inf, logits)
    m2 = jnp.max(rest_l, axis=-1, keepdims=True)
    i2 = jnp.min(jnp.where(rest_l == m2, lane, float(LANES)), axis=-1, keepdims=True)
    e2 = jnp.exp(m2 - m1)
    den = 1.0 + e2
    tg_ref[...] = jnp.where(lane == 0.0, 1.0 / den, jnp.where(lane == 1.0, e2 / den, 0.0))
    ti_ref[...] = jnp.where(lane == 0.0, i1, jnp.where(lane == 1.0, i2, 0.0)).astype(jnp.int32)


def _norm_mod(h, g, mod_l, shift_chunk, scale_chunk, dims, router=None):
    n_lat, seq, n_batch = dims
    t, d = h.shape
    tm = _tile(math.gcd(seq, t - n_lat), 512, 8)
    route = router is not None
    kern = functools.partial(_norm_kernel, tm=tm, n_lat=n_lat, seq=seq, n_batch=n_batch, route=route)
    in_specs = [
        pl.BlockSpec((tm, d), lambda i: (i, 0)),
        pl.BlockSpec((1, d), lambda i: (0, 0)),
        pl.BlockSpec((8, d), lambda i: (0, shift_chunk)),
        pl.BlockSpec((8, d), lambda i: (0, scale_chunk)),
    ]
    args = [h, g.reshape(1, d), mod_l, mod_l]
    if route:
        in_specs.append(pl.BlockSpec((d, LANES), lambda i: (0, 0)))
        args.append(router)
        out_specs = [pl.BlockSpec((tm, d // 2), lambda i: (i, 0)),
                     pl.BlockSpec((tm, LANES), lambda i: (i, 0)),
                     pl.BlockSpec((tm, LANES), lambda i: (i, 0))]
        out_shape = [jax.ShapeDtypeStruct((t, d // 2), jnp.uint32),
                     jax.ShapeDtypeStruct((t, LANES), F32),
                     jax.ShapeDtypeStruct((t, LANES), jnp.int32)]
    else:
        out_specs = pl.BlockSpec((tm, d), lambda i: (i, 0))
        out_shape = jax.ShapeDtypeStruct((t, d), BF16)
    return pl.pallas_call(
        kern, grid=(t // tm,), in_specs=in_specs, out_specs=out_specs, out_shape=out_shape,
        compiler_params=_params(1), name="norm_route" if route else "norm_mod",
    )(*args)


def _join_norm_kernel(x_ref, c_ref, g_ref, sh_ref, sc_ref, h_ref, o_ref, *, tm, n_lat, seq, n_batch):
    i = pl.program_id(0)
    grp = _group_of_tile(i, tm, n_lat, seq, n_batch)
    x = jnp.where(i * tm < n_lat, x_ref[...], c_ref[...])
    h_ref[...] = x
    y = x * lax.rsqrt(jnp.mean(x * x, axis=-1, keepdims=True) + NORM_EPS) * g_ref[...]
    o_ref[...] = (y * (1.0 + sc_ref[pl.ds(grp, 1), :]) + sh_ref[pl.ds(grp, 1), :]).astype(BF16)


def _join_norm_mod(x_lat, x_ctx, g, mod_l, shift_chunk, scale_chunk, dims):
    n_lat, seq, n_batch = dims
    d = x_lat.shape[1]
    t = n_lat + x_ctx.shape[0]
    tm = _tile(math.gcd(seq, t - n_lat), 512, 8)
    lat_tiles = n_lat // tm
    kern = functools.partial(_join_norm_kernel, tm=tm, n_lat=n_lat, seq=seq, n_batch=n_batch)
    return pl.pallas_call(
        kern, grid=(t // tm,),
        in_specs=[pl.BlockSpec((tm, d), lambda i: (jnp.minimum(i, lat_tiles - 1), 0)),
                  pl.BlockSpec((tm, d), lambda i: (jnp.maximum(i - lat_tiles, 0), 0)),
                  pl.BlockSpec((1, d), lambda i: (0, 0)),
                  pl.BlockSpec((8, d), lambda i: (0, shift_chunk)),
                  pl.BlockSpec((8, d), lambda i: (0, scale_chunk))],
        out_specs=[pl.BlockSpec((tm, d), lambda i: (i, 0)), pl.BlockSpec((tm, d), lambda i: (i, 0))],
        out_shape=[jax.ShapeDtypeStruct((t, d), F32), jax.ShapeDtypeStruct((t, d), BF16)],
        compiler_params=_params(1), name="join_norm_mod",
    )(x_lat, x_ctx, g.reshape(1, d), mod_l, mod_l)


def _final_norm_kernel(h_ref, g_ref, o_ref):
    x = h_ref[...]
    o_ref[...] = x * lax.rsqrt(jnp.mean(x * x, axis=-1, keepdims=True) + NORM_EPS) * g_ref[...]


def _final_norm(h, g, n_rows):
    d = h.shape[1]
    tm = _tile(n_rows, 256, 8)
    return pl.pallas_call(
        _final_norm_kernel, grid=(n_rows // tm,),
        in_specs=[pl.BlockSpec((tm, d), lambda i: (i, 0)), pl.BlockSpec((1, d), lambda i: (0, 0))],
        out_specs=pl.BlockSpec((tm, d), lambda i: (i, 0)),
        out_shape=jax.ShapeDtypeStruct((n_rows, d), F32),
        compiler_params=_params(1), name="final_norm",
    )(h, g.reshape(1, d))


def _cast_weights(first, w_refs, wb_refs):
    @pl.when(first)
    def _():
        for w_ref, wb_ref in zip(w_refs, wb_refs):
            wb_ref[...] = w_ref[...].astype(BF16)


IN_TN = 1024


def _in_layout(d_model):
    pieces = [("cq", 512, 512), ("ckv", 256, 256), ("krope", MLA_ROPE, LANES),
              ("swa_q", SWA_H * SWA_HD, LANES), ("swa_k", SWA_KV_H * SWA_HD, LANES),
              ("swa_v", SWA_KV_H * SWA_HD, LANES), ("na_q", NA_H * NA_HD, LANES), ("na_k", NA_H * NA_HD, LANES),
              ("na_v", NA_H * NA_HD, LANES), ("diff_q", DIFF_H * 2 * DIFF_DK, LANES),
              ("diff_k", DIFF_H * 2 * DIFF_DK, LANES), ("diff_v", DIFF_H * DIFF_DV, LANES),
              ("gates", N_BRANCH * d_model, 512)]
    natural, nat = {}, 0
    for name, width, _ in pieces:
        natural[name] = nat
        nat += width
    cols, runs, u = {}, [], 0
    for name, width, align in pieces[-1:] + pieces[:-1]:
        u = -(-u // align) * align
        cols[name] = u
        runs.append((u, natural[name], width))
        u += width
    width_u = -(-u // IN_TN) * IN_TN
    return cols, width_u, tuple(runs)


def _pad_w_in(w_in, width_u, runs):
    assert runs[0][0] == 0
    ends = [r[0] for r in runs[1:]] + [width_u]
    parts = []
    for (u0, n0, w), end in zip(runs, ends):
        piece = w_in[:, :, n0:n0 + w].astype(BF16)
        if end > u0 + w:
            piece = jnp.pad(piece, ((0, 0), (0, 0), (0, end - u0 - w)))
        parts.append(piece)
    return jnp.concatenate(parts, axis=2)


def _proj_kernel(x_ref, w_ref, o_ref):
    o_ref[...] = _dot(x_ref[...], w_ref[...]).astype(o_ref.dtype)


def _project_in(xn, w_p, layer):
    t, d = xn.shape
    n = w_p.shape[2]
    tm = _tile(t, 512, 8)
    return pl.pallas_call(
        _proj_kernel, grid=(n // IN_TN, t // tm),
        in_specs=[pl.BlockSpec((tm, d), lambda j, i: (i, 0)),
                  pl.BlockSpec((None, d, IN_TN), lambda j, i: (layer, 0, j))],
        out_specs=pl.BlockSpec((tm, IN_TN), lambda j, i: (i, j)),
        out_shape=jax.ShapeDtypeStruct((t, n), BF16),
        compiler_params=_params(2), name="project_in",
    )(xn, w_p)


def _proj_res_kernel(x_ref, w_ref, h_ref, gate_ref, o_ref, wb_ref, *, tm, n_lat, seq, n_batch):
    i = pl.program_id(1)
    _cast_weights(i == 0, [w_ref], [wb_ref])
    grp = _group_of_tile(i, tm, n_lat, seq, n_batch)
    gate = gate_ref[pl.ds(grp, 1), :]
    o_ref[...] = h_ref[...] + gate * _dot(x_ref[...], wb_ref[...])


def _project_residual(x, w_l, layer, h, mod_l, gate_chunk, dims):
    n_lat, seq, n_batch = dims
    t, k = x.shape
    n = w_l.shape[2]
    tm = _tile(math.gcd(seq, t - n_lat), 512, 8)
    tn = _tile(n, 512, LANES)
    kern = functools.partial(_proj_res_kernel, tm=tm, n_lat=n_lat, seq=seq, n_batch=n_batch)
    gate_blk = gate_chunk * (n // tn)
    return pl.pallas_call(
        kern, grid=(n // tn, t // tm),
        in_specs=[pl.BlockSpec((tm, k), lambda j, i: (i, 0)),
                  pl.BlockSpec((None, k, tn), lambda j, i: (layer, 0, j)),
                  pl.BlockSpec((tm, tn), lambda j, i: (i, j)),
                  pl.BlockSpec((8, tn), lambda j, i: (0, gate_blk + j))],
        out_specs=pl.BlockSpec((tm, tn), lambda j, i: (i, j)),
        out_shape=jax.ShapeDtypeStruct(h.shape, F32),
        scratch_shapes=[pltpu.VMEM((k, tn), BF16)],
        input_output_aliases={2: 0},
        compiler_params=_params(2), name="project_residual",
    )(x, w_l, h, mod_l)


def _swiglu_up_kernel(x_ref, w1_ref, w3_ref, o_ref, w1b_ref, w3b_ref):
    _cast_weights(pl.program_id(1) == 0, [w1_ref, w3_ref], [w1b_ref, w3b_ref])
    x = x_ref[...]
    a = _dot(x, w1b_ref[...])
    b = _dot(x, w3b_ref[...])
    o_ref[...] = (_silu(a) * b).astype(BF16)


def _swiglu_up(xn, w1, w3, idx):
    t, d = xn.shape
    f = w1.shape[2]
    tm = _tile(t, 1024, 8) if t % 1024 == 0 else _tile(t, 512, 8)
    tn = _tile(f, 512, LANES)
    return pl.pallas_call(
        _swiglu_up_kernel, grid=(f // tn, t // tm),
        in_specs=[pl.BlockSpec((tm, d), lambda j, i: (i, 0)),
                  pl.BlockSpec((None, d, tn), lambda j, i: (idx, 0, j)),
                  pl.BlockSpec((None, d, tn), lambda j, i: (idx, 0, j))],
        out_specs=pl.BlockSpec((tm, tn), lambda j, i: (i, j)),
        out_shape=jax.ShapeDtypeStruct((t, f), BF16),
        scratch_shapes=[pltpu.VMEM((d, tn), BF16), pltpu.VMEM((d, tn), BF16)],
        compiler_params=_params(2), name="swiglu_up",
    )(xn, w1, w3)


def _merge_kernel(o_ref, g0_ref, g1_ref, g2_ref, g3_ref, w_ref, out_ref, wb_ref):
    _cast_weights(pl.program_id(1) == 0, [w_ref], [wb_ref])
    acc = None
    for m, g_ref in enumerate((g0_ref, g1_ref, g2_ref, g3_ref)):
        y = _sigmoid(g_ref[...].astype(F32)) * _dot(o_ref[:, m * BRANCH_W:(m + 1) * BRANCH_W], wb_ref[m])
        acc = y if acc is None else acc + y
    out_ref[...] = acc.astype(BF16)


def _merge(o, u, gate_col, w_branch, layer):
    t = o.shape[0]
    d = w_branch.shape[3]
    tm = _tile(t, 512, 8)
    tn = _tile(d, 512, LANES)
    nj = d // tn
    assert gate_col % tn == 0
    gate_blk = gate_col // tn

    def gate_spec(m):
        return pl.BlockSpec((tm, tn), lambda j, i: (i, gate_blk + m * nj + j))

    return pl.pallas_call(
        _merge_kernel, grid=(nj, t // tm),
        in_specs=[pl.BlockSpec((tm, N_BRANCH * BRANCH_W), lambda j, i: (i, 0)),
                  gate_spec(0), gate_spec(1), gate_spec(2), gate_spec(3),
                  pl.BlockSpec((None, N_BRANCH, BRANCH_W, tn), lambda j, i: (layer, 0, 0, j))],
        out_specs=pl.BlockSpec((tm, tn), lambda j, i: (i, j)),
        out_shape=jax.ShapeDtypeStruct((t, d), BF16),
        scratch_shapes=[pltpu.VMEM((N_BRANCH, BRANCH_W, tn), BF16)],
        compiler_params=_params(2), name="merge",
    )(o, u, u, u, u, w_branch)


def _rope_tables(n, dim):
    pos = jnp.arange(n, dtype=jnp.int32)
    row = (pos // GRID_W).astype(F32)
    col = (pos % GRID_W).astype(F32)
    quarter = dim // 4
    inv_freq = ROPE_THETA ** (-jnp.arange(quarter, dtype=F32) / quarter)
    ang_r = row[:, None] * inv_freq
    ang_c = col[:, None] * inv_freq
    ang = jnp.concatenate([ang_r, ang_r, ang_c, ang_c], axis=-1)
    reps = LANES // dim
    cos = jnp.tile(jnp.cos(ang), (1, reps))
    sin = jnp.tile(jnp.sin(ang), (1, reps))
    lane = jnp.arange(LANES)[None, :]
    first = (lane & quarter) == 0
    return cos, jnp.where(first, -sin, 0.0), jnp.where(first, 0.0, sin)


def _rope(x, cos, sin_a, sin_b, quarter):
    return (x * cos + pltpu.roll(x, LANES - quarter, 1) * sin_a + pltpu.roll(x, quarter, 1) * sin_b)


def _mla_prep_kernel(cq_ref, ckv_ref, kr_ref, qg_ref, kvg_ref, wq_ref, wkv_ref, cos_ref, sa_ref, sb_ref,
                     q_ref, k_ref, v_ref, *, n_lat_tiles):
    is_lat = pl.program_id(0) < n_lat_tiles
    cos = jnp.where(is_lat, cos_ref[...], 1.0)
    sa = jnp.where(is_lat, sa_ref[...], 0.0)
    sb = jnp.where(is_lat, sb_ref[...], 0.0)
    quarter = MLA_ROPE // 4

    def rms(x_ref, g_ref):
        x = x_ref[...].astype(F32)
        return (x * lax.rsqrt(jnp.mean(x * x, axis=-1, keepdims=True) + NORM_EPS) * g_ref[...]).astype(BF16)

    scale = (MLA_NOPE + MLA_ROPE) ** -0.5 * LOG2E
    q = _dot(rms(cq_ref, qg_ref), wq_ref[...].astype(BF16)) * scale
    kv = _dot(rms(ckv_ref, kvg_ref), wkv_ref[...].astype(BF16))
    kr = _rope(kr_ref[...].astype(F32), cos, sa, sb, quarter).astype(BF16)
    for hd in range(MLA_H):
        b = hd * 2 * LANES
        q_ref[:, b:b + LANES] = q[:, b:b + LANES].astype(BF16)
        q_ref[:, b + LANES:b + 2 * LANES] = _rope(q[:, b + LANES:b + 2 * LANES], cos, sa, sb, quarter).astype(BF16)
        k_ref[:, b:b + LANES] = kv[:, b:b + LANES].astype(BF16)
        k_ref[:, b + LANES:b + 2 * LANES] = kr
        v_ref[:, hd * LANES:(hd + 1) * LANES] = kv[:, b + LANES:b + 2 * LANES].astype(BF16)


def _mla_prep(u, cols, q_norm, kv_norm, w_uq_p, w_ukv, rope64, dims):
    n_lat, seq, _ = dims
    t = u.shape[0]
    tm = _tile(math.gcd(seq, t - n_lat), 256, 16)
    n_lat_tiles = n_lat // tm
    seq_tiles = seq // tm
    qw = MLA_H * 2 * LANES

    def tab_spec():
        return pl.BlockSpec((tm, LANES), lambda i: (jnp.where(i < n_lat_tiles, i % seq_tiles, 0), 0))

    kern = functools.partial(_mla_prep_kernel, n_lat_tiles=n_lat_tiles)
    return pl.pallas_call(
        kern, grid=(t // tm,),
        in_specs=[pl.BlockSpec((tm, 512), lambda i: (i, cols["cq"] // 512)),
                  pl.BlockSpec((tm, 256), lambda i: (i, cols["ckv"] // 256)),
                  pl.BlockSpec((tm, LANES), lambda i: (i, cols["krope"] // LANES)),
                  pl.BlockSpec((1, 512), lambda i: (0, 0)),
                  pl.BlockSpec((1, 256), lambda i: (0, 0)),
                  pl.BlockSpec((512, qw), lambda i: (0, 0)),
                  pl.BlockSpec((256, qw), lambda i: (0, 0)),
                  tab_spec(), tab_spec(), tab_spec()],
        out_specs=[pl.BlockSpec((tm, qw), lambda i: (i, 0)),
                   pl.BlockSpec((tm, qw), lambda i: (i, 0)),
                   pl.BlockSpec((tm, MLA_H * MLA_V), lambda i: (i, 0))],
        out_shape=[jax.ShapeDtypeStruct((t, qw), BF16),
                   jax.ShapeDtypeStruct((t, qw), BF16),
                   jax.ShapeDtypeStruct((t, MLA_H * MLA_V), BF16)],
        compiler_params=_params(1), name="mla_prep",
    )(u, u, u, q_norm.reshape(1, -1), kv_norm.reshape(1, -1), w_uq_p, w_ukv, *rope64)


def _softmax_pv(scores, values, sink=None, base2=False):
    ex = jnp.exp2 if base2 else jnp.exp
    m = None
    for s in scores:
        mi = jnp.max(s, axis=-1, keepdims=True)
        m = mi if m is None else jnp.maximum(m, mi)
    if sink is not None:
        m = jnp.maximum(m, sink)
    den = None
    acc = None
    for s, v in zip(scores, values):
        p = ex(s - m)
        li = jnp.sum(p, axis=-1, keepdims=True)
        den = li if den is None else den + li
        o = _dot(p.astype(BF16), v)
        acc = o if acc is None else acc + o
    if sink is not None:
        den = den + ex(sink - m)
    return acc / den


def _diff_lambda(lam_ref, lam_init):
    lp = lam_ref[...]
    return (jnp.exp(jnp.sum(lp[0:1, :] * lp[1:2, :], axis=-1, keepdims=True))
            - jnp.exp(jnp.sum(lp[2:3, :] * lp[3:4, :], axis=-1, keepdims=True)) + lam_init)


def _diff_finish(o1, o2, lam, g, lam_init):
    o = o1 - lam * o2
    y = o * lax.rsqrt(jnp.mean(o * o, axis=-1, keepdims=True) + NORM_EPS) * g
    return y * (1.0 - lam_init)


DENSE_SUB = 256
DENSE_CHUNK = 256


def _transpose_to(dst_ref, src_ref):
    dst_ref[...] = src_ref[...].astype(F32).T.astype(BF16)


def _ref_segment(k_ref, vt_ref):
    return (k_ref.shape[0], lambda c0, w: k_ref[c0:c0 + w, :], lambda c0, w: vt_ref[:, c0:c0 + w], None)


def _scores_pass(q, segments, s_ref, sink=None):
    rows = q.shape[0]
    m_run, off = None, 0
    for n, keys, _, bias in segments:
        for c0 in range(0, n, DENSE_CHUNK):
            w = min(DENSE_CHUNK, n - c0)
            s = _dot_nt(keys(c0, w), q)
            if bias is not None:
                s = s + bias(c0, w)
            s_ref[off + c0:off + c0 + w, :] = s
            part = jnp.max(s.reshape(w // 8, 8, rows), axis=0)
            m_run = part if m_run is None else jnp.maximum(m_run, part)
        off += n
    m = jnp.max(m_run, axis=0, keepdims=True)
    return m if sink is None else jnp.maximum(m, sink)


def _pv_pass(segments, s_ref, m, sink=None):
    rows = m.shape[1]
    l_run, acc, off = None, None, 0
    for n, _, values_t, _ in segments:
        for c0 in range(0, n, DENSE_CHUNK):
            w = min(DENSE_CHUNK, n - c0)
            p = jnp.exp2(s_ref[off + c0:off + c0 + w, :] - m)
            part = jnp.sum(p.reshape(w // 8, 8, rows), axis=0)
            l_run = part if l_run is None else l_run + part
            pv = _dot(values_t(c0, w), p.astype(BF16))
            acc = pv if acc is None else acc + pv
        off += n
    den = jnp.sum(l_run, axis=0, keepdims=True)
    if sink is not None:
        den = den + jnp.exp2(sink - m)
    return (acc * (1.0 / den)).T


def _pipelined(units, first_pass, second_pass):
    state = first_pass(units[0], 0)
    for k, unit in enumerate(units):
        nxt = first_pass(units[k + 1], (k + 1) % 2) if k + 1 < len(units) else None
        second_pass(unit, k % 2, state)
        state = nxt


def _mla_attn_kernel(q_ref, kc_ref, kl_ref, vc_ref, vl_ref, oin_ref, o_ref, vtc_ref, vtl_ref, sa_ref, sb_ref, *, sub):
    del oin_ref
    s_refs = (sa_ref, sb_ref)

    @pl.when(pl.program_id(2) == 0)
    def _():
        _transpose_to(vtc_ref, vc_ref)
        _transpose_to(vtl_ref, vl_ref)

    segments = [_ref_segment(kc_ref, vtc_ref), _ref_segment(kl_ref, vtl_ref)]

    def first_pass(r0, slot):
        return _scores_pass(q_ref[r0:r0 + sub, :], segments, s_refs[slot])

    def second_pass(r0, slot, m):
        o_ref[r0:r0 + sub, :] = _pv_pass(segments, s_refs[slot], m).astype(BF16)

    _pipelined(list(range(0, q_ref.shape[0], sub)), first_pass, second_pass)


def _mla_attention(qa, ka, va, o, dims, branch):
    n_lat, seq, n_batch = dims
    n_ctx = (qa.shape[0] - n_lat) // n_batch
    tq = _tile(seq, 1024, 16)
    sub = _tile(tq, DENSE_SUB, 16)
    nq = seq // tq
    ctx_blk0 = n_lat // n_ctx
    kw = 2 * LANES
    return pl.pallas_call(
        functools.partial(_mla_attn_kernel, sub=sub), grid=(n_batch, MLA_H, nq),
        scratch_shapes=[pltpu.VMEM((MLA_V, n_ctx), BF16), pltpu.VMEM((MLA_V, seq), BF16),
                        pltpu.VMEM((n_ctx + seq, sub), F32), pltpu.VMEM((n_ctx + seq, sub), F32)],
        in_specs=[pl.BlockSpec((tq, kw), lambda b, h, i: (b * nq + i, h)),
                  pl.BlockSpec((n_ctx, kw), lambda b, h, i: (ctx_blk0 + b, h)),
                  pl.BlockSpec((seq, kw), lambda b, h, i: (b, h)),
                  pl.BlockSpec((n_ctx, LANES), lambda b, h, i: (ctx_blk0 + b, h)),
                  pl.BlockSpec((seq, LANES), lambda b, h, i: (b, h)),
                  pl.BlockSpec(memory_space=pl.ANY)],
        out_specs=pl.BlockSpec((tq, LANES), lambda b, h, i: (b * nq + i, branch * 4 + h)),
        out_shape=jax.ShapeDtypeStruct(o.shape, o.dtype),
        input_output_aliases={5: 0},
        compiler_params=_params(3), name="mla_attention",
    )(qa, ka, ka, va, va, o)


def _diff_attn_kernel(q_ref, kc_ref, kl_ref, vc_ref, vl_ref, cosq_ref, saq_ref, sbq_ref, cos_ref, sa_ref, sb_ref,
                      lam_ref, g_ref, oin_ref, o_ref, kr_ref, vtc_ref, vtl_ref, s0_ref, s1_ref, *, lam_init, sub):
    del oin_ref
    quarter = DIFF_DK // 4
    s_refs = (s0_ref, s1_ref)

    @pl.when(pl.program_id(2) == 0)
    def _():
        kr_ref[...] = _rope(kl_ref[...].astype(F32), cos_ref[...], sa_ref[...], sb_ref[...], quarter).astype(BF16)
        _transpose_to(vtc_ref, vc_ref)
        _transpose_to(vtl_ref, vl_ref)

    scale = DIFF_DK ** -0.5 * LOG2E
    segments = [_ref_segment(kc_ref, vtc_ref), _ref_segment(kr_ref, vtl_ref)]
    lam = _diff_lambda(lam_ref, lam_init)

    def first_pass(unit, slot):
        r0, j = unit
        rows = slice(r0, r0 + sub)
        q = _rope(q_ref[rows, :].astype(F32) * scale, cosq_ref[rows, :], saq_ref[rows, :], sbq_ref[rows, :], quarter)
        lane = lax.broadcasted_iota(jnp.int32, q.shape, 1)
        qj = jnp.where((lane >= j * DIFF_DK) & (lane < (j + 1) * DIFF_DK), q, 0.0).astype(BF16)
        return _scores_pass(qj, segments, s_refs[slot])

    first_map = {}

    def second_pass(unit, slot, m):
        r0, j = unit
        o = _pv_pass(segments, s_refs[slot], m)
        if j == 0:
            first_map[r0] = o
        else:
            o_ref[r0:r0 + sub, :] = _diff_finish(first_map.pop(r0), o, lam, g_ref[...], lam_init).astype(BF16)

    _pipelined([(r0, j) for r0 in range(0, q_ref.shape[0], sub) for j in range(2)], first_pass, second_pass)


def _diff_attention(u, cols, rope64, diff_lambda_l, subln_g, lam_init, o, dims, branch):
    n_lat, seq, n_batch = dims
    n_ctx = (u.shape[0] - n_lat) // n_batch
    tq = _tile(seq, 1024, 16)
    sub = _tile(tq, DENSE_SUB, 16)
    nq = seq // tq
    ctx_blk0 = n_lat // n_ctx
    qb, kb, vb = cols["diff_q"] // LANES, cols["diff_k"] // LANES, cols["diff_v"] // LANES
    kern = functools.partial(_diff_attn_kernel, lam_init=lam_init, sub=sub)

    def tab_q():
        return pl.BlockSpec((tq, LANES), lambda b, h, i: (i, 0))

    def tab_k():
        return pl.BlockSpec((seq, LANES), lambda b, h, i: (0, 0))

    return pl.pallas_call(
        kern, grid=(n_batch, DIFF_H, nq),
        in_specs=[pl.BlockSpec((tq, LANES), lambda b, h, i: (b * nq + i, qb + h)),
                  pl.BlockSpec((n_ctx, LANES), lambda b, h, i: (ctx_blk0 + b, kb + h)),
                  pl.BlockSpec((seq, LANES), lambda b, h, i: (b, kb + h)),
                  pl.BlockSpec((n_ctx, LANES), lambda b, h, i: (ctx_blk0 + b, vb + h)),
                  pl.BlockSpec((seq, LANES), lambda b, h, i: (b, vb + h)),
                  tab_q(), tab_q(), tab_q(), tab_k(), tab_k(), tab_k(),
                  pl.BlockSpec((4, DIFF_DK), lambda b, h, i: (0, 0)),
                  pl.BlockSpec((1, DIFF_DV), lambda b, h, i: (0, 0)),
                  pl.BlockSpec(memory_space=pl.ANY)],
        out_specs=pl.BlockSpec((tq, LANES), lambda b, h, i: (b * nq + i, branch * 4 + h)),
        out_shape=jax.ShapeDtypeStruct(o.shape, o.dtype),
        scratch_shapes=[pltpu.VMEM((seq, LANES), BF16),
                        pltpu.VMEM((DIFF_DV, n_ctx), BF16), pltpu.VMEM((DIFF_DV, seq), BF16),
                        pltpu.VMEM((n_ctx + seq, sub), F32), pltpu.VMEM((n_ctx + seq, sub), F32)],
        input_output_aliases={13: 0},
        compiler_params=_params(3), name="diff_attention",
    )(u, u, u, u, u, *rope64, *rope64, diff_lambda_l, subln_g.reshape(1, -1), o)


def _local_attn_kernel(*refs, tq, nk, back, seq, sub, use_rope, use_sink):
    it = iter(refs)
    q_ref, kl_ref, vl_ref, kc_ref, vc_ref, bias_ref = (next(it) for _ in range(6))
    if use_rope:
        cosq_ref, saq_ref, sbq_ref, cos_ref, sa_ref, sb_ref = (next(it) for _ in range(6))
    if use_sink:
        sink_ref = next(it)
    oin_ref, o_ref = next(it), next(it)
    del oin_ref
    if use_rope:
        kr_ref = next(it)
    vtc_ref, vtw_ref, s_ref = next(it), next(it), next(it)
    i = pl.program_id(2)
    scale = LANES ** -0.5 * LOG2E
    quarter = SWA_HD // 4

    @pl.when(i == 0)
    def _():
        _transpose_to(vtc_ref, vc_ref)
        if use_rope:
            kr_ref[...] = _rope(kl_ref[...].astype(F32), cos_ref[...], sa_ref[...], sb_ref[...],
                                quarter).astype(BF16)

    k_src = kr_ref if use_rope else kl_ref
    ks = pl.multiple_of(jnp.clip(i * tq - back, 0, seq - nk), LANES)
    vtw_ref[...] = vl_ref[pl.ds(ks, nk), :].astype(F32).T.astype(BF16)
    sink = sink_ref[pl.program_id(1)] * LOG2E if use_sink else None
    def segments(r0):
        window = (nk, lambda c0, w: k_src[pl.ds(ks + c0, w), :], lambda c0, w: vtw_ref[:, c0:c0 + w],
                  lambda c0, w: bias_ref[c0:c0 + w, r0:r0 + sub])
        return [_ref_segment(kc_ref, vtc_ref), window]

    maxima = []
    for t, r0 in enumerate(range(0, tq, sub)):
        q = q_ref[r0:r0 + sub, :].astype(F32) * scale
        if use_rope:
            q = _rope(q, cosq_ref[r0:r0 + sub, :], saq_ref[r0:r0 + sub, :], sbq_ref[r0:r0 + sub, :], quarter)
        maxima.append(_scores_pass(q.astype(BF16), segments(r0), s_ref.at[t], sink=sink))
    for t, r0 in enumerate(range(0, tq, sub)):
        o_ref[r0:r0 + sub, :] = _pv_pass(segments(r0), s_ref.at[t], maxima[t], sink=sink).astype(BF16)


def _local_attention(u, qcol, kcol, vcol, n_heads, n_kv, bias_plan, rope, sink, o, dims, branch, name):
    bias, tile_map, nk, back = bias_plan

    def bias_tile(i):
        idx = tile_map[-1]
        for t in range(len(tile_map) - 2, -1, -1):
            idx = jnp.where(i == t, tile_map[t], idx)
        return idx

    n_lat, seq, n_batch = dims
    n_ctx = (u.shape[0] - n_lat) // n_batch
    tq = bias.shape[3]
    sub = _tile(tq, DENSE_SUB, LANES)
    nq = seq // tq
    grp = n_heads // n_kv
    ctx_blk0 = n_lat // n_ctx
    qb, kb, vb = qcol // LANES, kcol // LANES, vcol // LANES
    per_head_bias = bias.shape[0] > 1
    use_rope, use_sink = rope is not None, sink is not None
    kern = functools.partial(_local_attn_kernel, tq=tq, nk=nk, back=back, seq=seq, sub=sub,
                             use_rope=use_rope, use_sink=use_sink)
    in_specs = [pl.BlockSpec((tq, LANES), lambda b, h, i: (b * nq + i, qb + h)),
                pl.BlockSpec((seq, LANES), lambda b, h, i: (b, kb + h // grp)),
                pl.BlockSpec((seq, LANES), lambda b, h, i: (b, vb + h // grp)),
                pl.BlockSpec((n_ctx, LANES), lambda b, h, i: (ctx_blk0 + b, kb + h // grp)),
                pl.BlockSpec((n_ctx, LANES), lambda b, h, i: (ctx_blk0 + b, vb + h // grp)),
                pl.BlockSpec((None, None, nk, tq),
                             lambda b, h, i: (h if per_head_bias else 0, bias_tile(i), 0, 0))]
    args = [u, u, u, u, u, bias]
    scratch = []
    if use_rope:
        in_specs += [pl.BlockSpec((tq, LANES), lambda b, h, i: (i, 0))] * 3
        in_specs += [pl.BlockSpec((seq, LANES), lambda b, h, i: (0, 0))] * 3
        args += [*rope, *rope]
        scratch.append(pltpu.VMEM((seq, LANES), BF16))
    scratch += [pltpu.VMEM((LANES, n_ctx), BF16), pltpu.VMEM((LANES, nk), BF16),
                pltpu.VMEM((tq // sub, n_ctx + nk, sub), F32)]
    if use_sink:
        in_specs.append(pl.BlockSpec(memory_space=pltpu.SMEM))
        args.append(sink)
    in_specs.append(pl.BlockSpec(memory_space=pl.ANY))
    args.append(o)
    return pl.pallas_call(
        kern, grid=(n_batch, n_heads, nq), in_specs=in_specs,
        out_specs=pl.BlockSpec((tq, LANES), lambda b, h, i: (b * nq + i, branch * 4 + h)),
        out_shape=jax.ShapeDtypeStruct(o.shape, o.dtype),
        scratch_shapes=scratch,
        input_output_aliases={len(args) - 1: 0},
        compiler_params=_params(3), name=name,
    )(*args)


def _local_tiling(seq, span):
    tq = _tile(seq, 512, 64)
    nk = min(seq, tq + 2 * span)
    return tq, nk


def _dedupe(keys):
    first, uniq, tile_map = {}, [], []
    for i, key in enumerate(keys):
        if key not in first:
            first[key] = len(uniq)
            uniq.append(i)
        tile_map.append(first[key])
    return uniq, tuple(tile_map)


def _swa_bias(seq):
    tq, nk = _local_tiling(seq, SWA_WINDOW)
    nq = seq // tq
    tiles = []
    for i in range(nq):
        ks = min(max(i * tq - SWA_WINDOW, 0), seq - nk)
        qp = i * tq + np.arange(tq)[:, None]
        kp = ks + np.arange(nk)[None, :]
        tiles.append(np.where(np.abs(qp - kp) <= SWA_WINDOW, 0.0, NEG_INF).astype(np.float32).T.copy())
    uniq, tile_map = _dedupe([t.tobytes() for t in tiles])
    return jnp.asarray(np.stack([tiles[i] for i in uniq])[None], F32), tile_map, nk, SWA_WINDOW


def _na_bias_kernel(rpb_ref, out_ref, toe_ref, *, row_offsets):
    n_d, n_c = 2 * NA_KH - 1, 2 * NA_KW - 1
    hd = pl.program_id(0)
    shape = (GRID_W, 2 * GRID_W)
    kc = lax.broadcasted_iota(jnp.int32, shape, 0)
    lane = lax.broadcasted_iota(jnp.int32, shape, 1)
    qc = lane & (GRID_W - 1)
    col_start = jnp.clip(qc - NA_KW // 2, 0, GRID_W - NA_KW)
    valid = (kc >= col_start) & (kc < col_start + NA_KW)
    dc = jnp.clip(kc - qc, -(NA_KW - 1), NA_KW - 1) + (NA_KW - 1)
    for d in range(n_d):
        blk = jnp.zeros(shape, F32)
        for j in range(n_c):
            blk = jnp.where(dc == j, rpb_ref[(hd * n_d + d) * n_c + j] * LOG2E, blk)
        toe_ref[d] = jnp.where(valid, blk, NEG_INF)
    toe_ref[n_d] = jnp.full(shape, NEG_INF, F32)
    left = lane < GRID_W
    for u, offs in enumerate(row_offsets):
        for a in range(0, len(offs), 2):
            for b in range(len(offs[a])):
                out_ref[u, b * GRID_W:(b + 1) * GRID_W, a * GRID_W:(a + 2) * GRID_W] = jnp.where(
                    left, toe_ref[offs[a][b]], toe_ref[offs[a + 1][b]])


def _na_bias(rpb, seq):
    rows = seq // GRID_W
    kh = min(NA_KH, rows)
    back_rows = kh // 2
    tq, nk = _local_tiling(seq, back_rows * GRID_W)
    nq, qr_n, kr_n = seq // tq, tq // GRID_W, nk // GRID_W
    assert qr_n % 2 == 0
    n_heads = rpb.shape[0]
    n_d = 2 * NA_KH - 1
    d_idx = []
    for i in range(nq):
        ks_row = min(max(i * qr_n - back_rows, 0), rows - kr_n)
        qr = i * qr_n + np.arange(qr_n)[:, None]
        kr = ks_row + np.arange(kr_n)[None, :]
        start = np.clip(qr - kh // 2, 0, rows - kh)
        ok = (kr >= start) & (kr < start + kh)
        d_idx.append(np.where(ok, kr - qr + (NA_KH - 1), n_d))
    uniq, tile_map = _dedupe([d.tobytes() for d in d_idx])
    row_offsets = tuple(tuple(tuple(int(v) for v in row) for row in d_idx[i]) for i in uniq)
    bias = pl.pallas_call(
        functools.partial(_na_bias_kernel, row_offsets=row_offsets), grid=(n_heads,),
        in_specs=[pl.BlockSpec(memory_space=pltpu.SMEM)],
        out_specs=pl.BlockSpec((None, len(uniq), nk, tq), lambda hd: (hd, 0, 0, 0)),
        out_shape=jax.ShapeDtypeStruct((n_heads, len(uniq), nk, tq), F32),
        scratch_shapes=[pltpu.VMEM((n_d + 1, GRID_W, 2 * GRID_W), F32)],
        compiler_params=_params(1), name="na_bias",
    )(rpb.reshape(-1))
    return bias, tile_map, nk, back_rows * GRID_W


def _ctx_attn_kernel(u_ref, qa_ref, ka_ref, va_ref, sink_ref, lam_ref, g_ref, oin_ref, o_ref, *, cols, lam_init):
    del oin_ref

    def col(name, hd, width=LANES):
        c0 = cols[name] + hd * width
        return u_ref[:, c0:c0 + width]

    for hd in range(MLA_H):
        q = qa_ref[:, hd * 2 * LANES:(hd + 1) * 2 * LANES]
        k = ka_ref[:, hd * 2 * LANES:(hd + 1) * 2 * LANES]
        v = va_ref[:, hd * LANES:(hd + 1) * LANES]
        o_ref[:, hd * LANES:(hd + 1) * LANES] = _softmax_pv([_dot_nt(q, k)], [v], base2=True).astype(BF16)
    grp = SWA_H // SWA_KV_H
    scale = SWA_HD ** -0.5
    for hd in range(SWA_H):
        q = (col("swa_q", hd).astype(F32) * scale).astype(BF16)
        o = _softmax_pv([_dot_nt(q, col("swa_k", hd // grp))], [col("swa_v", hd // grp)], sink=sink_ref[hd])
        o_ref[:, BRANCH_W + hd * LANES:BRANCH_W + (hd + 1) * LANES] = o.astype(BF16)
    scale = NA_HD ** -0.5
    for hd in range(NA_H):
        q = (col("na_q", hd).astype(F32) * scale).astype(BF16)
        o = _softmax_pv([_dot_nt(q, col("na_k", hd))], [col("na_v", hd)])
        o_ref[:, 2 * BRANCH_W + hd * LANES:2 * BRANCH_W + (hd + 1) * LANES] = o.astype(BF16)
    scale = DIFF_DK ** -0.5
    lam = _diff_lambda(lam_ref, lam_init)
    for hd in range(DIFF_H):
        q = col("diff_q", hd).astype(F32) * scale
        k = col("diff_k", hd)
        v = col("diff_v", hd)
        lane = lax.broadcasted_iota(jnp.int32, q.shape, 1)
        outs = []
        for j in range(2):
            qj = jnp.where((lane >= j * DIFF_DK) & (lane < (j + 1) * DIFF_DK), q, 0.0).astype(BF16)
            outs.append(_softmax_pv([_dot_nt(qj, k)], [v]))
        o = _diff_finish(outs[0], outs[1], lam, g_ref[...], lam_init)
        o_ref[:, 3 * BRANCH_W + hd * LANES:3 * BRANCH_W + (hd + 1) * LANES] = o.astype(BF16)


def _ctx_attention(u, cols, qa, ka, va, sink, diff_lambda_l, subln_g, lam_init, o, dims):
    n_lat, seq, n_batch = dims
    n_ctx = (u.shape[0] - n_lat) // n_batch
    blk0 = n_lat // n_ctx
    kern = functools.partial(_ctx_attn_kernel, cols=cols, lam_init=lam_init)
    return pl.pallas_call(
        kern, grid=(n_batch,),
        in_specs=[pl.BlockSpec((n_ctx, u.shape[1]), lambda b: (blk0 + b, 0)),
                  pl.BlockSpec((n_ctx, qa.shape[1]), lambda b: (blk0 + b, 0)),
                  pl.BlockSpec((n_ctx, ka.shape[1]), lambda b: (blk0 + b, 0)),
                  pl.BlockSpec((n_ctx, va.shape[1]), lambda b: (blk0 + b, 0)),
                  pl.BlockSpec(memory_space=pltpu.SMEM),
                  pl.BlockSpec((4, DIFF_DK), lambda b: (0, 0)),
                  pl.BlockSpec((1, DIFF_DV), lambda b: (0, 0)),
                  pl.BlockSpec(memory_space=pl.ANY)],
        out_specs=pl.BlockSpec((n_ctx, o.shape[1]), lambda b: (blk0 + b, 0)),
        out_shape=jax.ShapeDtypeStruct(o.shape, o.dtype),
        input_output_aliases={7: 0},
        compiler_params=_params(1), name="ctx_attention",
    )(u, qa, ka, va, sink, diff_lambda_l, subln_g.reshape(1, -1), o)


MOE_TM = 512
GATHER_ROWS = 256


def _route_plan(top_i, tm):
    t = top_i.shape[0]
    n_assign = t * TOP_K
    n_tiles = (n_assign + N_EXPERTS * (tm - 1)) // tm
    e_flat = top_i.reshape(-1)
    onehot = (e_flat[:, None] == jnp.arange(N_EXPERTS)[None, :]).astype(jnp.int32)
    csum = jnp.cumsum(onehot, axis=0)
    rank = jnp.sum(onehot * csum, axis=1) - 1
    counts = csum[-1]
    padded = ((counts + tm - 1) // tm) * tm
    ends = jnp.cumsum(padded)
    offs = ends - padded
    pos = (jnp.sum(onehot * offs[None, :], axis=1) + rank).astype(jnp.int32)
    src = jnp.zeros((n_tiles * tm,), jnp.int32).at[pos].set(
        jnp.arange(n_assign, dtype=jnp.int32) // TOP_K, unique_indices=True, mode="promise_in_bounds")
    tile_start = jnp.arange(n_tiles, dtype=jnp.int32) * tm
    tile_expert = jnp.minimum(jnp.sum((tile_start[:, None] >= ends[None, :]).astype(jnp.int32), axis=1),
                              N_EXPERTS - 1).astype(jnp.int32)
    tile_valid = (tile_start < ends[-1]).astype(jnp.int32)
    return pos, src, tile_expert, tile_valid


def _row_copies(idx_ref, base, src_hbm, bufs, sems, slot, n_rows, idx_stride):
    def body(r, carry):
        for s, buf in enumerate(bufs):
            row = idx_ref[(base + r) * idx_stride + s]
            pltpu.make_async_copy(src_hbm.at[pl.ds(row, 1), :], buf.at[slot, pl.ds(r, 1), :],
                                  sems.at[slot, s]).start()
        return carry
    lax.fori_loop(0, n_rows, body, 0, unroll=8)


def _wait_rows(src_hbm, bufs, sems, slot, n_rows):
    for s, buf in enumerate(bufs):
        pltpu.make_async_copy(src_hbm.at[pl.ds(0, n_rows), :], buf.at[slot], sems.at[slot, s]).wait()


def _gather_kernel(src_ref, tv_ref, x_hbm, o_ref, x_ref, buf_ref, sem, *, rows):
    i = pl.program_id(0)

    @pl.when(i == 0)
    def _():
        cp = pltpu.make_async_copy(x_hbm, x_ref, sem)
        cp.start()
        cp.wait()

    @pl.when(tv_ref[i] == 1)
    def _():
        def body(r, carry):
            buf_ref[pl.ds(r, 1), :] = x_ref[pl.ds(src_ref[i * rows + r], 1), :]
            return carry
        lax.fori_loop(0, rows, body, 0, unroll=8)
        w = buf_ref[...]
        half = w.shape[1]
        o_ref[:, :half] = lax.bitcast_convert_type(w << 16, F32).astype(BF16)
        o_ref[:, half:] = lax.bitcast_convert_type(w & jnp.uint32(0xFFFF0000), F32).astype(BF16)

    @pl.when(tv_ref[i] == 0)
    def _():
        o_ref[...] = jnp.zeros(o_ref.shape, BF16)


def _gather_rows(xp, src, tile_valid, rows):
    t, half = xp.shape
    n_rows = src.shape[0]
    kern = functools.partial(_gather_kernel, rows=rows)
    return pl.pallas_call(
        kern,
        grid_spec=pltpu.PrefetchScalarGridSpec(
            num_scalar_prefetch=2, grid=(n_rows // rows,),
            in_specs=[pl.BlockSpec(memory_space=pl.ANY)],
            out_specs=pl.BlockSpec((rows, 2 * half), lambda i, s, v: (i, 0)),
            scratch_shapes=[pltpu.VMEM((t, half), jnp.uint32), pltpu.VMEM((rows, half), jnp.uint32),
                            pltpu.SemaphoreType.DMA(())]),
        out_shape=jax.ShapeDtypeStruct((n_rows, 2 * half), BF16),
        compiler_params=_params(1), name="moe_gather",
    )(src, tile_valid, xp)


def _expert_changed(te_ref, i):
    return jnp.logical_or(i == 0, te_ref[i] != te_ref[jnp.maximum(i - 1, 0)])


def _gmm_up_kernel(te_ref, tv_ref, x_ref, w1_ref, w3_ref, o_ref, w1b_ref, w3b_ref):
    i = pl.program_id(1)
    _cast_weights(_expert_changed(te_ref, i), [w1_ref, w3_ref], [w1b_ref, w3b_ref])

    @pl.when(tv_ref[i] == 1)
    def _():
        x = x_ref[...]
        o_ref[...] = (_silu(_dot(x, w1b_ref[...])) * _dot(x, w3b_ref[...])).astype(BF16)

    @pl.when(tv_ref[i] == 0)
    def _():
        o_ref[...] = jnp.zeros(o_ref.shape, BF16)


def _gmm_up(xs, w1, w3, idx, tile_expert, tile_valid, tm):
    r, d = xs.shape
    f = w1.shape[3]
    tn = _tile(f, 512, LANES)
    return pl.pallas_call(
        _gmm_up_kernel,
        grid_spec=pltpu.PrefetchScalarGridSpec(
            num_scalar_prefetch=2, grid=(f // tn, r // tm),
            in_specs=[pl.BlockSpec((tm, d), lambda j, i, te, tv: (i, 0)),
                      pl.BlockSpec((None, None, d, tn), lambda j, i, te, tv: (idx, te[i], 0, j)),
                      pl.BlockSpec((None, None, d, tn), lambda j, i, te, tv: (idx, te[i], 0, j))],
            out_specs=pl.BlockSpec((tm, tn), lambda j, i, te, tv: (i, j)),
            scratch_shapes=[pltpu.VMEM((d, tn), BF16), pltpu.VMEM((d, tn), BF16)]),
        out_shape=jax.ShapeDtypeStruct((r, f), BF16),
        compiler_params=_params(2), name="moe_up",
    )(tile_expert, tile_valid, xs, w1, w3)


def _gmm_down_kernel(te_ref, tv_ref, a_ref, w_ref, o_ref, wb_ref):
    i = pl.program_id(1)
    _cast_weights(_expert_changed(te_ref, i), [w_ref], [wb_ref])

    @pl.when(tv_ref[i] == 1)
    def _():
        o_ref[...] = _dot(a_ref[...], wb_ref[...])

    @pl.when(tv_ref[i] == 0)
    def _():
        o_ref[...] = jnp.zeros(o_ref.shape, F32)


def _gmm_down(a, w2, idx, tile_expert, tile_valid, tm):
    r, f = a.shape
    d = w2.shape[3]
    tn = _tile(d, 1024, LANES)
    return pl.pallas_call(
        _gmm_down_kernel,
        grid_spec=pltpu.PrefetchScalarGridSpec(
            num_scalar_prefetch=2, grid=(d // tn, r // tm),
            in_specs=[pl.BlockSpec((tm, f), lambda j, i, te, tv: (i, 0)),
                      pl.BlockSpec((None, None, f, tn), lambda j, i, te, tv: (idx, te[i], 0, j),
                                   pipeline_mode=pl.Buffered(1))],
            out_specs=pl.BlockSpec((tm, tn), lambda j, i, te, tv: (i, j)),
            scratch_shapes=[pltpu.VMEM((f, tn), BF16)]),
        out_shape=jax.ShapeDtypeStruct((r, d), F32),
        compiler_params=_params(2), name="moe_down",
    )(tile_expert, tile_valid, a, w2)


def _combine_kernel(pos_ref, y_hbm, h_ref, tg_ref, gate_ref, o_ref, buf0, buf1, sems, *, rows, n_lat, seq, n_batch):
    i = pl.program_id(0)
    n = pl.num_programs(0)
    slot = i % 2
    bufs = [buf0, buf1]

    @pl.when(i == 0)
    def _():
        _row_copies(pos_ref, 0, y_hbm, bufs, sems, 0, rows, TOP_K)

    @pl.when(i + 1 < n)
    def _():
        _row_copies(pos_ref, (i + 1) * rows, y_hbm, bufs, sems, 1 - slot, rows, TOP_K)

    _wait_rows(y_hbm, bufs, sems, slot, rows)
    grp = _group_of_tile(i, rows, n_lat, seq, n_batch)
    gate = gate_ref[pl.ds(grp, 1), :]
    tg = tg_ref[...]
    mix = tg[:, 0:1] * buf0[slot] + tg[:, 1:2] * buf1[slot]
    o_ref[...] = h_ref[...] + gate * mix


def _combine(h, y, pos, top_g, mod_l, gate_chunk, dims):
    n_lat, seq, n_batch = dims
    t, d = h.shape
    rows = _tile(math.gcd(seq, t - n_lat), GATHER_ROWS, 8)
    kern = functools.partial(_combine_kernel, rows=rows, n_lat=n_lat, seq=seq, n_batch=n_batch)
    return pl.pallas_call(
        kern,
        grid_spec=pltpu.PrefetchScalarGridSpec(
            num_scalar_prefetch=1, grid=(t // rows,),
            in_specs=[pl.BlockSpec(memory_space=pl.ANY),
                      pl.BlockSpec((rows, d), lambda i, p: (i, 0)),
                      pl.BlockSpec((rows, LANES), lambda i, p: (i, 0)),
                      pl.BlockSpec((8, d), lambda i, p: (0, gate_chunk))],
            out_specs=pl.BlockSpec((rows, d), lambda i, p: (i, 0)),
            scratch_shapes=[pltpu.VMEM((2, rows, d), F32), pltpu.VMEM((2, rows, d), F32),
                            pltpu.SemaphoreType.DMA((2, 2))]),
        out_shape=jax.ShapeDtypeStruct(h.shape, F32),
        input_output_aliases={2: 0},
        compiler_params=_params(1), name="moe_combine",
    )(pos, y, h, top_g, mod_l)


def kernel(x, c, ctx, c_ctx, mod_w, mod_b, norm1_g, norm2_g, w_in, mla_q_norm, mla_kv_norm, mla_w_uq, mla_w_ukv,
           swa_sink, na_rpb, diff_lambda, diff_subln_g, w_branch, w_out, ffn_w1, ffn_w3, ffn_w2, moe_router,
           moe_w1, moe_w3, moe_w2, final_norm_g):
    n_batch, seq, d = x.shape
    n_ctx = ctx.shape[1]
    depth = mod_w.shape[0]
    n_lat = n_batch * seq
    dims = (n_lat, seq, n_batch)

    t = n_lat + n_batch * n_ctx
    h = None
    c8 = jnp.concatenate([c, c_ctx[None, :], jnp.zeros((8 - n_batch - 1, d), F32)], axis=0)
    mod = _mod_all(c8, mod_w, mod_b)

    cols, width_u, in_runs = _in_layout(d)
    w_in_p = _pad_w_in(w_in, width_u, in_runs)
    w_uq = mla_w_uq.reshape(depth, -1, MLA_H, MLA_NOPE + MLA_ROPE)
    w_uq_p = jnp.pad(w_uq, ((0, 0), (0, 0), (0, 0), (0, 2 * LANES - MLA_NOPE - MLA_ROPE)))
    w_uq_p = w_uq_p.reshape(depth, -1, MLA_H * 2 * LANES)
    rope64 = _rope_tables(seq, MLA_ROPE)
    rope128 = _rope_tables(seq, SWA_HD)
    swa_plan = _swa_bias(seq)
    router_p = jnp.pad(moe_router, ((0, 0), (0, 0), (0, LANES - N_EXPERTS)))

    for l in range(depth):
        need_ctx = l < depth - 1
        lam_init = 0.8 - 0.6 * math.exp(-0.3 * l)
        mod_l = mod[l]

        if h is None:
            h, xn = _join_norm_mod(x.reshape(n_lat, d), ctx.reshape(n_batch * n_ctx, d), norm1_g[l], mod_l, 0, 1, dims)
        else:
            xn = _norm_mod(h, norm1_g[l], mod_l, 0, 1, dims)
        u = _project_in(xn, w_in_p, l)

        qa, ka, va = _mla_prep(u, cols, mla_q_norm[l], mla_kv_norm[l], w_uq_p[l], mla_w_ukv[l], rope64, dims)
        o = jnp.zeros((t, N_BRANCH * BRANCH_W), BF16)
        o = _mla_attention(qa, ka, va, o, dims, 0)
        o = _local_attention(u, cols["swa_q"], cols["swa_k"], cols["swa_v"], SWA_H, SWA_KV_H, swa_plan,
                             rope128, swa_sink[l], o, dims, 1, "swa_attention")
        o = _local_attention(u, cols["na_q"], cols["na_k"], cols["na_v"], NA_H, NA_H, _na_bias(na_rpb[l], seq),
                             None, None, o, dims, 2, "na_attention")
        o = _diff_attention(u, cols, rope64, diff_lambda[l], diff_subln_g[l], lam_init, o, dims, 3)
        if need_ctx:
            o = _ctx_attention(u, cols, qa, ka, va, swa_sink[l], diff_lambda[l], diff_subln_g[l], lam_init, o, dims)

        merged = _merge(o, u, cols["gates"], w_branch, l)
        h = _project_residual(merged, w_out, l, h, mod_l, 2, dims)

        i = l // 2
        if l % 2 == 0:
            xn = _norm_mod(h, norm2_g[l], mod_l, 3, 4, dims)
            a = _swiglu_up(xn, ffn_w1, ffn_w3, i)
            h = _project_residual(a, ffn_w2, i, h, mod_l, 5, dims)
        else:
            xn, top_g, top_i = _norm_mod(h, norm2_g[l], mod_l, 3, 4, dims, router=router_p[i])
            pos, src, tile_expert, tile_valid = _route_plan(top_i[:, :TOP_K], MOE_TM)
            xs = _gather_rows(xn, src, tile_valid, MOE_TM)
            a = _gmm_up(xs, moe_w1, moe_w3, i, tile_expert, tile_valid, MOE_TM)
            y = _gmm_down(a, moe_w2, i, tile_expert, tile_valid, MOE_TM)
            h = _combine(h, y, pos, top_g, mod_l, 5, dims)

    out = _final_norm(h, final_norm_g, n_lat)
    return out.reshape(n_batch, seq, d)
```

```python
import functools
import math

import jax
import jax.numpy as jnp
import numpy as np
from jax import lax
from jax.experimental import pallas as pl
from jax.experimental.pallas import tpu as pltpu

F32 = jnp.float32
BF16 = jnp.bfloat16

GRID_W = 64
ROPE_THETA = 10000.0
NORM_EPS = 1e-6
NEG_INF = -1e30
LOG2E = math.log2(math.e)

MLA_H, MLA_NOPE, MLA_ROPE, MLA_V = 4, 128, 64, 128
SWA_H, SWA_KV_H, SWA_HD, SWA_WINDOW = 4, 2, 128, 128
NA_H, NA_HD, NA_KH, NA_KW = 4, 128, 8, 16
DIFF_H, DIFF_DK, DIFF_DV = 4, 64, 128
N_BRANCH, BRANCH_W = 4, 512
N_EXPERTS, TOP_K = 8, 2
LANES = 128

VMEM_LIMIT = 56 * 1024 * 1024


def _params(n_axes):
    return pltpu.CompilerParams(dimension_semantics=("arbitrary",) * n_axes, vmem_limit_bytes=VMEM_LIMIT)


def _tile(n, pref, mult):
    if n <= pref:
        return n
    t = (pref // mult) * mult
    while t > mult and n % t:
        t -= mult
    assert n % t == 0, (n, pref, mult)
    return t


def _dot(a, b):
    return jnp.dot(a, b, preferred_element_type=F32)


def _dot_nt(a, b):
    return lax.dot_general(a, b, (((1,), (1,)), ((), ())), preferred_element_type=F32)


def _silu(x):
    return x * (1.0 / (1.0 + jnp.exp(-x)))


def _sigmoid(x):
    return 0.5 * (1.0 + jnp.tanh(0.5 * x))


def _mod_kernel(c_ref, w_ref, b_ref, o_ref):
    a = _silu(c_ref[...]).astype(BF16)
    o_ref[...] = _dot(a, w_ref[...].astype(BF16)) + b_ref[...]


def _mod_all(c8, mod_w, mod_b):
    n_l, d, n = mod_w.shape
    tn = _tile(n, 1024, LANES)
    return pl.pallas_call(
        _mod_kernel,
        grid=(n_l, n // tn),
        in_specs=[
            pl.BlockSpec((8, d), lambda l, j: (0, 0)),
            pl.BlockSpec((None, d, tn), lambda l, j: (l, 0, j)),
            pl.BlockSpec((None, 1, tn), lambda l, j: (l, 0, j)),
        ],
        out_specs=pl.BlockSpec((None, 8, tn), lambda l, j: (l, 0, j)),
        out_shape=jax.ShapeDtypeStruct((n_l, 8, n), F32),
        compiler_params=_params(2),
        name="mod_all",
    )(c8, mod_w, mod_b.reshape(n_l, 1, n))


def _group_of_tile(i, tm, n_lat, seq, n_batch):
    row0 = i * tm
    return jnp.where(row0 < n_lat, row0 // seq, n_batch)


def _rms(x, g):
    return x * lax.rsqrt(jnp.mean(x * x, axis=-1, keepdims=True) + NORM_EPS) * g


def _pack_bf16_pairs(xn):
    bits = lax.bitcast_convert_type(xn.astype(BF16).astype(F32), jnp.uint32)
    half = bits.shape[1] // 2
    return (bits[:, half:] & jnp.uint32(0xFFFF0000)) | (bits[:, :half] >> 16)


def _route_top2(xn, router_t):
    lane = lax.broadcasted_iota(jnp.int32, (xn.shape[0], LANES), 1).astype(F32)
    logits = jnp.full((xn.shape[0], LANES), -jnp.inf, F32)
    for e in range(N_EXPERTS):
        logit_e = jnp.sum(xn * router_t[e:e + 1, :], axis=-1, keepdims=True)
        logits = jnp.where(lane == float(e), logit_e, logits)
    m1 = jnp.max(logits, axis=-1, keepdims=True)
    i1 = jnp.min(jnp.where(logits == m1, lane, float(LANES)), axis=-1, keepdims=True)
    rest_l = jnp.where(lane == i1, -jnp.inf, logits)
    m2 = jnp.max(rest_l, axis=-1, keepdims=True)
    i2 = jnp.min(jnp.where(rest_l == m2, lane, float(LANES)), axis=-1, keepdims=True)
    e2 = jnp.exp(m2 - m1)
    den = 1.0 + e2
    top_g = jnp.where(lane == 0.0, 1.0 / den, jnp.where(lane == 1.0, e2 / den, 0.0))
    top_i = jnp.where(lane == 0.0, i1, jnp.where(lane == 1.0, i2, 0.0)).astype(jnp.int32)
    return top_g, top_i


def _norm_kernel(h_ref, g_ref, sh_ref, sc_ref, o_ref, *, tm, n_lat, seq, n_batch):
    grp = _group_of_tile(pl.program_id(0), tm, n_lat, seq, n_batch)
    xn = _rms(h_ref[...], g_ref[...]) * (1.0 + sc_ref[pl.ds(grp, 1), :]) + sh_ref[pl.ds(grp, 1), :]
    o_ref[...] = xn.astype(BF16)


def _norm_mod(h, g, mod_l, shift_chunk, scale_chunk, dims):
    n_lat, seq, n_batch = dims
    t, d = h.shape
    tm = _tile(math.gcd(seq, t - n_lat), 512, 8)
    kern = functools.partial(_norm_kernel, tm=tm, n_lat=n_lat, seq=seq, n_batch=n_batch)
    return pl.pallas_call(
        kern, grid=(t // tm,),
        in_specs=[pl.BlockSpec((tm, d), lambda i: (i, 0)),
                  pl.BlockSpec((1, d), lambda i: (0, 0)),
                  pl.BlockSpec((8, d), lambda i: (0, shift_chunk)),
                  pl.BlockSpec((8, d), lambda i: (0, scale_chunk))],
        out_specs=pl.BlockSpec((tm, d), lambda i: (i, 0)),
        out_shape=jax.ShapeDtypeStruct((t, d), BF16),
        compiler_params=_params(1), name="norm_mod",
    )(h, g.reshape(1, d), mod_l, mod_l)


def _join_norm_kernel(x_ref, c_ref, g_ref, sh_ref, sc_ref, h_ref, o_ref, *, tm, n_lat, seq, n_batch):
    i = pl.program_id(0)
    grp = _group_of_tile(i, tm, n_lat, seq, n_batch)
    x = jnp.where(i * tm < n_lat, x_ref[...], c_ref[...])
    h_ref[...] = x
    y = x * lax.rsqrt(jnp.mean(x * x, axis=-1, keepdims=True) + NORM_EPS) * g_ref[...]
    o_ref[...] = (y * (1.0 + sc_ref[pl.ds(grp, 1), :]) + sh_ref[pl.ds(grp, 1), :]).astype(BF16)


def _join_norm_mod(x_lat, x_ctx, g, mod_l, shift_chunk, scale_chunk, dims):
    n_lat, seq, n_batch = dims
    d = x_lat.shape[1]
    t = n_lat + x_ctx.shape[0]
    tm = _tile(math.gcd(seq, t - n_lat), 512, 8)
    lat_tiles = n_lat // tm
    kern = functools.partial(_join_norm_kernel, tm=tm, n_lat=n_lat, seq=seq, n_batch=n_batch)
    return pl.pallas_call(
        kern, grid=(t // tm,),
        in_specs=[pl.BlockSpec((tm, d), lambda i: (jnp.minimum(i, lat_tiles - 1), 0)),
                  pl.BlockSpec((tm, d), lambda i: (jnp.maximum(i - lat_tiles, 0), 0)),
                  pl.BlockSpec((1, d), lambda i: (0, 0)),
                  pl.BlockSpec((8, d), lambda i: (0, shift_chunk)),
                  pl.BlockSpec((8, d), lambda i: (0, scale_chunk))],
        out_specs=[pl.BlockSpec((tm, d), lambda i: (i, 0)), pl.BlockSpec((tm, d), lambda i: (i, 0))],
        out_shape=[jax.ShapeDtypeStruct((t, d), F32), jax.ShapeDtypeStruct((t, d), BF16)],
        compiler_params=_params(1), name="join_norm_mod",
    )(x_lat, x_ctx, g.reshape(1, d), mod_l, mod_l)


def _final_norm_kernel(h_ref, g_ref, o_ref):
    x = h_ref[...]
    o_ref[...] = x * lax.rsqrt(jnp.mean(x * x, axis=-1, keepdims=True) + NORM_EPS) * g_ref[...]


def _final_norm(h, g, n_rows):
    d = h.shape[1]
    tm = _tile(n_rows, 256, 8)
    return pl.pallas_call(
        _final_norm_kernel, grid=(n_rows // tm,),
        in_specs=[pl.BlockSpec((tm, d), lambda i: (i, 0)), pl.BlockSpec((1, d), lambda i: (0, 0))],
        out_specs=pl.BlockSpec((tm, d), lambda i: (i, 0)),
        out_shape=jax.ShapeDtypeStruct((n_rows, d), F32),
        compiler_params=_params(1), name="final_norm",
    )(h, g.reshape(1, d))


def _cast_weights(first, w_refs, wb_refs):
    @pl.when(first)
    def _():
        for w_ref, wb_ref in zip(w_refs, wb_refs):
            wb_ref[...] = w_ref[...].astype(BF16)


IN_TN = 1024


def _in_layout(d_model):
    pieces = [("cq", 512, 512), ("ckv", 256, 256), ("krope", MLA_ROPE, LANES),
              ("swa_q", SWA_H * SWA_HD, LANES), ("swa_k", SWA_KV_H * SWA_HD, LANES),
              ("swa_v", SWA_KV_H * SWA_HD, LANES), ("na_q", NA_H * NA_HD, LANES), ("na_k", NA_H * NA_HD, LANES),
              ("na_v", NA_H * NA_HD, LANES), ("diff_q", DIFF_H * 2 * DIFF_DK, LANES),
              ("diff_k", DIFF_H * 2 * DIFF_DK, LANES), ("diff_v", DIFF_H * DIFF_DV, LANES),
              ("gates", N_BRANCH * d_model, 512)]
    natural, nat = {}, 0
    for name, width, _ in pieces:
        natural[name] = nat
        nat += width
    cols, runs, u = {}, [], 0
    for name, width, align in pieces[-1:] + pieces[:-1]:
        u = -(-u // align) * align
        cols[name] = u
        runs.append((u, natural[name], width))
        u += width
    width_u = -(-u // IN_TN) * IN_TN
    return cols, width_u, tuple(runs)


def _pad_w_in(w_in, width_u, runs):
    assert runs[0][0] == 0
    ends = [r[0] for r in runs[1:]] + [width_u]
    parts = []
    for (u0, n0, w), end in zip(runs, ends):
        piece = w_in[:, :, n0:n0 + w].astype(BF16)
        if end > u0 + w:
            piece = jnp.pad(piece, ((0, 0), (0, 0), (0, end - u0 - w)))
        parts.append(piece)
    return jnp.concatenate(parts, axis=2)


def _proj_kernel(x_ref, w_ref, o_ref):
    o_ref[...] = _dot(x_ref[...], w_ref[...]).astype(o_ref.dtype)


def _project_in(xn, w_p, layer):
    t, d = xn.shape
    n = w_p.shape[2]
    tm = _tile(t, 512, 8)
    return pl.pallas_call(
        _proj_kernel, grid=(n // IN_TN, t // tm),
        in_specs=[pl.BlockSpec((tm, d), lambda j, i: (i, 0)),
                  pl.BlockSpec((None, d, IN_TN), lambda j, i: (layer, 0, j))],
        out_specs=pl.BlockSpec((tm, IN_TN), lambda j, i: (i, j)),
        out_shape=jax.ShapeDtypeStruct((t, n), BF16),
        compiler_params=_params(2), name="project_in",
    )(xn, w_p)


def _proj_res_kernel(x_ref, w_ref, h_ref, gate_ref, o_ref, wb_ref, *, tm, n_lat, seq, n_batch):
    i = pl.program_id(1)
    _cast_weights(i == 0, [w_ref], [wb_ref])
    grp = _group_of_tile(i, tm, n_lat, seq, n_batch)
    gate = gate_ref[pl.ds(grp, 1), :]
    o_ref[...] = h_ref[...] + gate * _dot(x_ref[...], wb_ref[...])


def _project_residual(x, w_l, layer, h, mod_l, gate_chunk, dims):
    n_lat, seq, n_batch = dims
    t, k = x.shape
    n = w_l.shape[2]
    tm = _tile(math.gcd(seq, t - n_lat), 512, 8)
    tn = _tile(n, 512, LANES)
    kern = functools.partial(_proj_res_kernel, tm=tm, n_lat=n_lat, seq=seq, n_batch=n_batch)
    gate_blk = gate_chunk * (n // tn)
    return pl.pallas_call(
        kern, grid=(n // tn, t // tm),
        in_specs=[pl.BlockSpec((tm, k), lambda j, i: (i, 0)),
                  pl.BlockSpec((None, k, tn), lambda j, i: (layer, 0, j)),
                  pl.BlockSpec((tm, tn), lambda j, i: (i, j)),
                  pl.BlockSpec((8, tn), lambda j, i: (0, gate_blk + j))],
        out_specs=pl.BlockSpec((tm, tn), lambda j, i: (i, j)),
        out_shape=jax.ShapeDtypeStruct(h.shape, F32),
        scratch_shapes=[pltpu.VMEM((k, tn), BF16)],
        input_output_aliases={2: 0},
        compiler_params=_params(2), name="project_residual",
    )(x, w_l, h, mod_l)


def _out_norm_kernel(x_ref, w_ref, h_ref, mod_ref, g_ref, *rest, tm, n_lat, seq, n_batch, chunks, route):
    d = h_ref.shape[1]
    if route:
        r_ref, o_ref, xn_ref, tg_ref, ti_ref, wb_ref = rest
    else:
        o_ref, xn_ref, wb_ref = rest
    i = pl.program_id(0)
    _cast_weights(i == 0, [w_ref], [wb_ref])
    grp = _group_of_tile(i, tm, n_lat, seq, n_batch)
    gate, shift, scale = (mod_ref[pl.ds(grp, 1), c * d:(c + 1) * d] for c in chunks)
    h_new = h_ref[...] + gate * _dot(x_ref[...], wb_ref[...])
    o_ref[...] = h_new
    xn = _rms(h_new, g_ref[...]) * (1.0 + scale) + shift
    if route:
        xn_ref[...] = _pack_bf16_pairs(xn)
        tg_ref[...], ti_ref[...] = _route_top2(xn, r_ref[...])
    else:
        xn_ref[...] = xn.astype(BF16)


def _project_out_norm(x, w_l, layer, h, mod_l, g, dims, n_rows, router=None):
    n_lat, seq, n_batch = dims
    k = x.shape[1]
    d = h.shape[1]
    assert w_l.shape[2] == d
    tm = _tile(math.gcd(seq, h.shape[0] - n_lat), 256, 8)
    route = router is not None
    kern = functools.partial(_out_norm_kernel, tm=tm, n_lat=n_lat, seq=seq, n_batch=n_batch,
                             chunks=(2, 3, 4), route=route)
    row = lambda i: (i, 0)
    fixed = lambda i: (0, 0)
    in_specs = [pl.BlockSpec((tm, k), row),
                pl.BlockSpec((None, k, d), lambda i: (layer, 0, 0), pipeline_mode=pl.Buffered(1)),
                pl.BlockSpec((tm, d), row),
                pl.BlockSpec(mod_l.shape, fixed),
                pl.BlockSpec((1, d), fixed)]
    args = [x, w_l, h, mod_l, g.reshape(1, d)]
    out_specs = [pl.BlockSpec((tm, d), row)]
    out_shape = [jax.ShapeDtypeStruct(h.shape, F32)]
    if route:
        in_specs.append(pl.BlockSpec((N_EXPERTS, d), fixed))
        args.append(router)
        out_specs += [pl.BlockSpec((tm, d // 2), row), pl.BlockSpec((tm, LANES), row), pl.BlockSpec((tm, LANES), row)]
        out_shape += [jax.ShapeDtypeStruct((n_rows, d // 2), jnp.uint32),
                      jax.ShapeDtypeStruct((n_rows, LANES), F32),
                      jax.ShapeDtypeStruct((n_rows, LANES), jnp.int32)]
    else:
        out_specs.append(pl.BlockSpec((tm, d), row))
        out_shape.append(jax.ShapeDtypeStruct((n_rows, d), BF16))
    return pl.pallas_call(
        kern, grid=(n_rows // tm,), in_specs=in_specs, out_specs=out_specs, out_shape=out_shape,
        scratch_shapes=[pltpu.VMEM((k, d), BF16)],
        input_output_aliases={2: 0},
        compiler_params=_params(1), name="project_out_norm",
    )(*args)


def _swiglu_up_kernel(x_ref, w1_ref, w3_ref, o_ref, w1b_ref, w3b_ref):
    _cast_weights(pl.program_id(1) == 0, [w1_ref, w3_ref], [w1b_ref, w3b_ref])
    x = x_ref[...]
    a = _dot(x, w1b_ref[...])
    b = _dot(x, w3b_ref[...])
    o_ref[...] = (_silu(a) * b).astype(BF16)


def _swiglu_up(xn, w1, w3, idx):
    t, d = xn.shape
    f = w1.shape[2]
    tm = _tile(t, 1024, 8) if t % 1024 == 0 else _tile(t, 512, 8)
    tn = _tile(f, 512, LANES)
    return pl.pallas_call(
        _swiglu_up_kernel, grid=(f // tn, t // tm),
        in_specs=[pl.BlockSpec((tm, d), lambda j, i: (i, 0)),
                  pl.BlockSpec((None, d, tn), lambda j, i: (idx, 0, j)),
                  pl.BlockSpec((None, d, tn), lambda j, i: (idx, 0, j))],
        out_specs=pl.BlockSpec((tm, tn), lambda j, i: (i, j)),
        out_shape=jax.ShapeDtypeStruct((t, f), BF16),
        scratch_shapes=[pltpu.VMEM((d, tn), BF16), pltpu.VMEM((d, tn), BF16)],
        compiler_params=_params(2), name="swiglu_up",
    )(xn, w1, w3)


def _merge_kernel(o_ref, g0_ref, g1_ref, g2_ref, g3_ref, w_ref, out_ref, wb_ref):
    _cast_weights(pl.program_id(1) == 0, [w_ref], [wb_ref])
    acc = None
    for m, g_ref in enumerate((g0_ref, g1_ref, g2_ref, g3_ref)):
        y = _sigmoid(g_ref[...]).astype(F32) * _dot(o_ref[:, m * BRANCH_W:(m + 1) * BRANCH_W], wb_ref[m])
        acc = y if acc is None else acc + y
    out_ref[...] = acc.astype(BF16)


def _merge(o, u, gate_col, w_branch, layer, n_rows):
    t = n_rows
    d = w_branch.shape[3]
    tm = _tile(math.gcd(t, o.shape[0]), 512, 8)
    tn = _tile(d, 512, LANES)
    nj = d // tn
    assert gate_col % tn == 0
    gate_blk = gate_col // tn

    def gate_spec(m):
        return pl.BlockSpec((tm, tn), lambda j, i: (i, gate_blk + m * nj + j))

    return pl.pallas_call(
        _merge_kernel, grid=(nj, t // tm),
        in_specs=[pl.BlockSpec((tm, N_BRANCH * BRANCH_W), lambda j, i: (i, 0)),
                  gate_spec(0), gate_spec(1), gate_spec(2), gate_spec(3),
                  pl.BlockSpec((None, N_BRANCH, BRANCH_W, tn), lambda j, i: (layer, 0, 0, j))],
        out_specs=pl.BlockSpec((tm, tn), lambda j, i: (i, j)),
        out_shape=jax.ShapeDtypeStruct((t, d), BF16),
        scratch_shapes=[pltpu.VMEM((N_BRANCH, BRANCH_W, tn), BF16)],
        compiler_params=_params(2), name="merge",
    )(o, u, u, u, u, w_branch)


def _rope_tables(n, dim):
    pos = jnp.arange(n, dtype=jnp.int32)
    row = (pos // GRID_W).astype(F32)
    col = (pos % GRID_W).astype(F32)
    quarter = dim // 4
    inv_freq = ROPE_THETA ** (-jnp.arange(quarter, dtype=F32) / quarter)
    ang_r = row[:, None] * inv_freq
    ang_c = col[:, None] * inv_freq
    ang = jnp.concatenate([ang_r, ang_r, ang_c, ang_c], axis=-1)
    reps = LANES // dim
    cos = jnp.tile(jnp.cos(ang), (1, reps))
    sin = jnp.tile(jnp.sin(ang), (1, reps))
    lane = jnp.arange(LANES)[None, :]
    first = (lane & quarter) == 0
    return cos, jnp.where(first, -sin, 0.0), jnp.where(first, 0.0, sin)


def _rope(x, cos, sin_a, sin_b, quarter):
    return (x * cos + pltpu.roll(x, LANES - quarter, 1) * sin_a + pltpu.roll(x, quarter, 1) * sin_b)


def _mla_prep_kernel(cq_ref, ckv_ref, kr_ref, qg_ref, kvg_ref, wq_ref, wkv_ref, cos_ref, sa_ref, sb_ref,
                     q_ref, k_ref, v_ref, *, n_lat_tiles):
    is_lat = pl.program_id(0) < n_lat_tiles
    cos = jnp.where(is_lat, cos_ref[...], 1.0)
    sa = jnp.where(is_lat, sa_ref[...], 0.0)
    sb = jnp.where(is_lat, sb_ref[...], 0.0)
    quarter = MLA_ROPE // 4

    def rms(x_ref, g_ref):
        x = x_ref[...].astype(F32)
        return (x * lax.rsqrt(jnp.mean(x * x, axis=-1, keepdims=True) + NORM_EPS) * g_ref[...]).astype(BF16)

    scale = (MLA_NOPE + MLA_ROPE) ** -0.5 * LOG2E
    q = _dot(rms(cq_ref, qg_ref), wq_ref[...].astype(BF16)) * scale
    kv = _dot(rms(ckv_ref, kvg_ref), wkv_ref[...].astype(BF16))
    kr = _rope(kr_ref[...].astype(F32), cos, sa, sb, quarter).astype(BF16)
    for hd in range(MLA_H):
        b = hd * 2 * LANES
        q_ref[:, b:b + LANES] = q[:, b:b + LANES].astype(BF16)
        q_ref[:, b + LANES:b + 2 * LANES] = _rope(q[:, b + LANES:b + 2 * LANES], cos, sa, sb, quarter).astype(BF16)
        k_ref[:, b:b + LANES] = kv[:, b:b + LANES].astype(BF16)
        k_ref[:, b + LANES:b + 2 * LANES] = kr
        v_ref[:, hd * LANES:(hd + 1) * LANES] = kv[:, b + LANES:b + 2 * LANES].astype(BF16)


def _mla_prep(u, cols, q_norm, kv_norm, w_uq_p, w_ukv, rope64, dims):
    n_lat, seq, _ = dims
    t = u.shape[0]
    tm = _tile(math.gcd(seq, t - n_lat), 256, 16)
    n_lat_tiles = n_lat // tm
    seq_tiles = seq // tm
    qw = MLA_H * 2 * LANES

    def tab_spec():
        return pl.BlockSpec((tm, LANES), lambda i: (jnp.where(i < n_lat_tiles, i % seq_tiles, 0), 0))

    kern = functools.partial(_mla_prep_kernel, n_lat_tiles=n_lat_tiles)
    return pl.pallas_call(
        kern, grid=(t // tm,),
        in_specs=[pl.BlockSpec((tm, 512), lambda i: (i, cols["cq"] // 512)),
                  pl.BlockSpec((tm, 256), lambda i: (i, cols["ckv"] // 256)),
                  pl.BlockSpec((tm, LANES), lambda i: (i, cols["krope"] // LANES)),
                  pl.BlockSpec((1, 512), lambda i: (0, 0)),
                  pl.BlockSpec((1, 256), lambda i: (0, 0)),
                  pl.BlockSpec((512, qw), lambda i: (0, 0)),
                  pl.BlockSpec((256, qw), lambda i: (0, 0)),
                  tab_spec(), tab_spec(), tab_spec()],
        out_specs=[pl.BlockSpec((tm, qw), lambda i: (i, 0)),
                   pl.BlockSpec((tm, qw), lambda i: (i, 0)),
                   pl.BlockSpec((tm, MLA_H * MLA_V), lambda i: (i, 0))],
        out_shape=[jax.ShapeDtypeStruct((t, qw), BF16),
                   jax.ShapeDtypeStruct((t, qw), BF16),
                   jax.ShapeDtypeStruct((t, MLA_H * MLA_V), BF16)],
        compiler_params=_params(1), name="mla_prep",
    )(u, u, u, q_norm.reshape(1, -1), kv_norm.reshape(1, -1), w_uq_p, w_ukv, *rope64)


def _softmax_pv(scores, values, sink=None, base2=False):
    ex = jnp.exp2 if base2 else jnp.exp
    m = None
    for s in scores:
        mi = jnp.max(s, axis=-1, keepdims=True)
        m = mi if m is None else jnp.maximum(m, mi)
    if sink is not None:
        m = jnp.maximum(m, sink)
    den = None
    acc = None
    for s, v in zip(scores, values):
        p = ex(s - m)
        li = jnp.sum(p, axis=-1, keepdims=True)
        den = li if den is None else den + li
        o = _dot(p.astype(BF16), v)
        acc = o if acc is None else acc + o
    if sink is not None:
        den = den + ex(sink - m)
    return acc / den


def _diff_lambda(lam_ref, lam_init):
    lp = lam_ref[...]
    return (jnp.exp(jnp.sum(lp[0:1, :] * lp[1:2, :], axis=-1, keepdims=True))
            - jnp.exp(jnp.sum(lp[2:3, :] * lp[3:4, :], axis=-1, keepdims=True)) + lam_init)


def _diff_finish(o1, o2, lam, g, lam_init):
    o = o1 - lam * o2
    y = o * lax.rsqrt(jnp.mean(o * o, axis=-1, keepdims=True) + NORM_EPS) * g
    return y * (1.0 - lam_init)


DENSE_SUB = 256
DENSE_CHUNK = 256


def _transpose_to(dst_ref, src_ref):
    dst_ref[...] = src_ref[...].astype(F32).T.astype(BF16)


def _ref_segment(k_ref, vt_ref):
    return (k_ref.shape[0], lambda c0, w: k_ref[c0:c0 + w, :], lambda c0, w: vt_ref[:, c0:c0 + w], None)


def _scores_pass(q, segments, s_ref, sink=None):
    rows = q.shape[0]
    m_run, off = None, 0
    for n, keys, _, bias in segments:
        for c0 in range(0, n, DENSE_CHUNK):
            w = min(DENSE_CHUNK, n - c0)
            s = _dot_nt(keys(c0, w), q)
            if bias is not None:
                s = s + bias(c0, w)
            s_ref[off + c0:off + c0 + w, :] = s
            part = jnp.max(s.reshape(w // 8, 8, rows), axis=0)
            m_run = part if m_run is None else jnp.maximum(m_run, part)
        off += n
    m = jnp.max(m_run, axis=0, keepdims=True)
    return m if sink is None else jnp.maximum(m, sink)


def _pv_pass(segments, s_ref, m, sink=None):
    rows = m.shape[1]
    l_run, acc, off = None, None, 0
    for n, _, values_t, _ in segments:
        for c0 in range(0, n, DENSE_CHUNK):
            w = min(DENSE_CHUNK, n - c0)
            p = jnp.exp2(s_ref[off + c0:off + c0 + w, :] - m)
            part = jnp.sum(p.reshape(w // 8, 8, rows), axis=0)
            l_run = part if l_run is None else l_run + part
            pv = _dot(values_t(c0, w), p.astype(BF16))
            acc = pv if acc is None else acc + pv
        off += n
    den = jnp.sum(l_run, axis=0, keepdims=True)
    if sink is not None:
        den = den + jnp.exp2(sink - m)
    return (acc * (1.0 / den)).T


def _pipelined(units, first_pass, second_pass):
    state = first_pass(units[0], 0)
    for k, unit in enumerate(units):
        nxt = first_pass(units[k + 1], (k + 1) % 2) if k + 1 < len(units) else None
        second_pass(unit, k % 2, state)
        state = nxt


def _mla_attn_kernel(q_ref, kc_ref, kl_ref, vc_ref, vl_ref, oin_ref, o_ref, vtc_ref, vtl_ref, sa_ref, sb_ref, *, sub):
    del oin_ref
    s_refs = (sa_ref, sb_ref)

    @pl.when(pl.program_id(2) == 0)
    def _():
        _transpose_to(vtc_ref, vc_ref)
        _transpose_to(vtl_ref, vl_ref)

    segments = [_ref_segment(kc_ref, vtc_ref), _ref_segment(kl_ref, vtl_ref)]

    def first_pass(r0, slot):
        return _scores_pass(q_ref[r0:r0 + sub, :], segments, s_refs[slot])

    def second_pass(r0, slot, m):
        o_ref[r0:r0 + sub, :] = _pv_pass(segments, s_refs[slot], m).astype(BF16)

    _pipelined(list(range(0, q_ref.shape[0], sub)), first_pass, second_pass)


def _mla_attention(qa, ka, va, o, dims, branch):
    n_lat, seq, n_batch = dims
    n_ctx = (qa.shape[0] - n_lat) // n_batch
    tq = _tile(seq, 1024, 16)
    sub = _tile(tq, DENSE_SUB, 16)
    nq = seq // tq
    ctx_blk0 = n_lat // n_ctx
    kw = 2 * LANES
    return pl.pallas_call(
        functools.partial(_mla_attn_kernel, sub=sub), grid=(n_batch, MLA_H, nq),
        scratch_shapes=[pltpu.VMEM((MLA_V, n_ctx), BF16), pltpu.VMEM((MLA_V, seq), BF16),
                        pltpu.VMEM((n_ctx + seq, sub), F32), pltpu.VMEM((n_ctx + seq, sub), F32)],
        in_specs=[pl.BlockSpec((tq, kw), lambda b, h, i: (b * nq + i, h)),
                  pl.BlockSpec((n_ctx, kw), lambda b, h, i: (ctx_blk0 + b, h)),
                  pl.BlockSpec((seq, kw), lambda b, h, i: (b, h)),
                  pl.BlockSpec((n_ctx, LANES), lambda b, h, i: (ctx_blk0 + b, h)),
                  pl.BlockSpec((seq, LANES), lambda b, h, i: (b, h)),
                  pl.BlockSpec(memory_space=pl.ANY)],
        out_specs=pl.BlockSpec((tq, LANES), lambda b, h, i: (b * nq + i, branch * 4 + h)),
        out_shape=jax.ShapeDtypeStruct(o.shape, o.dtype),
        input_output_aliases={5: 0},
        compiler_params=_params(3), name="mla_attention",
    )(qa, ka, ka, va, va, o)


def _diff_attn_kernel(q_ref, kc_ref, kl_ref, vc_ref, vl_ref, cosq_ref, saq_ref, sbq_ref, cos_ref, sa_ref, sb_ref,
                      lam_ref, g_ref, oin_ref, o_ref, kr_ref, vtc_ref, vtl_ref, s0_ref, s1_ref, *, lam_init, sub):
    del oin_ref
    quarter = DIFF_DK // 4
    s_refs = (s0_ref, s1_ref)

    @pl.when(pl.program_id(2) == 0)
    def _():
        kr_ref[...] = _rope(kl_ref[...].astype(F32), cos_ref[...], sa_ref[...], sb_ref[...], quarter).astype(BF16)
        _transpose_to(vtc_ref, vc_ref)
        _transpose_to(vtl_ref, vl_ref)

    scale = DIFF_DK ** -0.5 * LOG2E
    segments = [_ref_segment(kc_ref, vtc_ref), _ref_segment(kr_ref, vtl_ref)]
    lam = _diff_lambda(lam_ref, lam_init)

    def first_pass(unit, slot):
        r0, j = unit
        rows = slice(r0, r0 + sub)
        q = _rope(q_ref[rows, :].astype(F32) * scale, cosq_ref[rows, :], saq_ref[rows, :], sbq_ref[rows, :], quarter)
        lane = lax.broadcasted_iota(jnp.int32, q.shape, 1)
        qj = jnp.where((lane >= j * DIFF_DK) & (lane < (j + 1) * DIFF_DK), q, 0.0).astype(BF16)
        return _scores_pass(qj, segments, s_refs[slot])

    first_map = {}

    def second_pass(unit, slot, m):
        r0, j = unit
        o = _pv_pass(segments, s_refs[slot], m)
        if j == 0:
            first_map[r0] = o
        else:
            o_ref[r0:r0 + sub, :] = _diff_finish(first_map.pop(r0), o, lam, g_ref[...], lam_init).astype(BF16)

    _pipelined([(r0, j) for r0 in range(0, q_ref.shape[0], sub) for j in range(2)], first_pass, second_pass)


def _diff_attention(u, cols, rope64, diff_lambda_l, subln_g, lam_init, o, dims, branch):
    n_lat, seq, n_batch = dims
    n_ctx = (u.shape[0] - n_lat) // n_batch
    tq = _tile(seq, 1024, 16)
    sub = _tile(tq, DENSE_SUB, 16)
    nq = seq // tq
    ctx_blk0 = n_lat // n_ctx
    qb, kb, vb = cols["diff_q"] // LANES, cols["diff_k"] // LANES, cols["diff_v"] // LANES
    kern = functools.partial(_diff_attn_kernel, lam_init=lam_init, sub=sub)

    def tab_q():
        return pl.BlockSpec((tq, LANES), lambda b, h, i: (i, 0))

    def tab_k():
        return pl.BlockSpec((seq, LANES), lambda b, h, i: (0, 0))

    return pl.pallas_call(
        kern, grid=(n_batch, DIFF_H, nq),
        in_specs=[pl.BlockSpec((tq, LANES), lambda b, h, i: (b * nq + i, qb + h)),
                  pl.BlockSpec((n_ctx, LANES), lambda b, h, i: (ctx_blk0 + b, kb + h)),
                  pl.BlockSpec((seq, LANES), lambda b, h, i: (b, kb + h)),
                  pl.BlockSpec((n_ctx, LANES), lambda b, h, i: (ctx_blk0 + b, vb + h)),
                  pl.BlockSpec((seq, LANES), lambda b, h, i: (b, vb + h)),
                  tab_q(), tab_q(), tab_q(), tab_k(), tab_k(), tab_k(),
                  pl.BlockSpec((4, DIFF_DK), lambda b, h, i: (0, 0)),
                  pl.BlockSpec((1, DIFF_DV), lambda b, h, i: (0, 0)),
                  pl.BlockSpec(memory_space=pl.ANY)],
        out_specs=pl.BlockSpec((tq, LANES), lambda b, h, i: (b * nq + i, branch * 4 + h)),
        out_shape=jax.ShapeDtypeStruct(o.shape, o.dtype),
        scratch_shapes=[pltpu.VMEM((seq, LANES), BF16),
                        pltpu.VMEM((DIFF_DV, n_ctx), BF16), pltpu.VMEM((DIFF_DV, seq), BF16),
                        pltpu.VMEM((n_ctx + seq, sub), F32), pltpu.VMEM((n_ctx + seq, sub), F32)],
        input_output_aliases={13: 0},
        compiler_params=_params(3), name="diff_attention",
    )(u, u, u, u, u, *rope64, *rope64, diff_lambda_l, subln_g.reshape(1, -1), o)


def _local_attn_kernel(*refs, tq, nk, back, seq, sub, use_rope, use_sink):
    it = iter(refs)
    q_ref, kl_ref, vl_ref, kc_ref, vc_ref, bias_ref = (next(it) for _ in range(6))
    if use_rope:
        cosq_ref, saq_ref, sbq_ref, cos_ref, sa_ref, sb_ref = (next(it) for _ in range(6))
    if use_sink:
        sink_ref = next(it)
    oin_ref, o_ref = next(it), next(it)
    del oin_ref
    if use_rope:
        kr_ref = next(it)
    vtc_ref, vtw_ref, s_ref = next(it), next(it), next(it)
    i = pl.program_id(2)
    scale = LANES ** -0.5 * LOG2E
    quarter = SWA_HD // 4

    @pl.when(i == 0)
    def _():
        _transpose_to(vtc_ref, vc_ref)
        if use_rope:
            kr_ref[...] = _rope(kl_ref[...].astype(F32), cos_ref[...], sa_ref[...], sb_ref[...],
                                quarter).astype(BF16)

    k_src = kr_ref if use_rope else kl_ref
    ks = pl.multiple_of(jnp.clip(i * tq - back, 0, seq - nk), LANES)
    vtw_ref[...] = vl_ref[pl.ds(ks, nk), :].astype(F32).T.astype(BF16)
    sink = sink_ref[pl.program_id(1)] * LOG2E if use_sink else None
    def segments(r0):
        window = (nk, lambda c0, w: k_src[pl.ds(ks + c0, w), :], lambda c0, w: vtw_ref[:, c0:c0 + w],
                  lambda c0, w: bias_ref[c0:c0 + w, r0:r0 + sub])
        return [_ref_segment(kc_ref, vtc_ref), window]

    maxima = []
    for t, r0 in enumerate(range(0, tq, sub)):
        q = q_ref[r0:r0 + sub, :].astype(F32) * scale
        if use_rope:
            q = _rope(q, cosq_ref[r0:r0 + sub, :], saq_ref[r0:r0 + sub, :], sbq_ref[r0:r0 + sub, :], quarter)
        maxima.append(_scores_pass(q.astype(BF16), segments(r0), s_ref.at[t], sink=sink))
    for t, r0 in enumerate(range(0, tq, sub)):
        o_ref[r0:r0 + sub, :] = _pv_pass(segments(r0), s_ref.at[t], maxima[t], sink=sink).astype(BF16)


def _local_attention(u, qcol, kcol, vcol, n_heads, n_kv, bias_plan, rope, sink, o, dims, branch, name):
    bias, tile_map, nk, back = bias_plan

    def bias_tile(i):
        idx = tile_map[-1]
        for t in range(len(tile_map) - 2, -1, -1):
            idx = jnp.where(i == t, tile_map[t], idx)
        return idx

    n_lat, seq, n_batch = dims
    n_ctx = (u.shape[0] - n_lat) // n_batch
    tq = bias.shape[3]
    sub = _tile(tq, DENSE_SUB, LANES)
    nq = seq // tq
    grp = n_heads // n_kv
    ctx_blk0 = n_lat // n_ctx
    qb, kb, vb = qcol // LANES, kcol // LANES, vcol // LANES
    per_head_bias = bias.shape[0] > 1
    use_rope, use_sink = rope is not None, sink is not None
    kern = functools.partial(_local_attn_kernel, tq=tq, nk=nk, back=back, seq=seq, sub=sub,
                             use_rope=use_rope, use_sink=use_sink)
    in_specs = [pl.BlockSpec((tq, LANES), lambda b, h, i: (b * nq + i, qb + h)),
                pl.BlockSpec((seq, LANES), lambda b, h, i: (b, kb + h // grp)),
                pl.BlockSpec((seq, LANES), lambda b, h, i: (b, vb + h // grp)),
                pl.BlockSpec((n_ctx, LANES), lambda b, h, i: (ctx_blk0 + b, kb + h // grp)),
                pl.BlockSpec((n_ctx, LANES), lambda b, h, i: (ctx_blk0 + b, vb + h // grp)),
                pl.BlockSpec((None, None, nk, tq),
                             lambda b, h, i: (h if per_head_bias else 0, bias_tile(i), 0, 0))]
    args = [u, u, u, u, u, bias]
    scratch = []
    if use_rope:
        in_specs += [pl.BlockSpec((tq, LANES), lambda b, h, i: (i, 0))] * 3
        in_specs += [pl.BlockSpec((seq, LANES), lambda b, h, i: (0, 0))] * 3
        args += [*rope, *rope]
        scratch.append(pltpu.VMEM((seq, LANES), BF16))
    scratch += [pltpu.VMEM((LANES, n_ctx), BF16), pltpu.VMEM((LANES, nk), BF16),
                pltpu.VMEM((tq // sub, n_ctx + nk, sub), F32)]
    if use_sink:
        in_specs.append(pl.BlockSpec(memory_space=pltpu.SMEM))
        args.append(sink)
    in_specs.append(pl.BlockSpec(memory_space=pl.ANY))
    args.append(o)
    return pl.pallas_call(
        kern, grid=(n_batch, n_heads, nq), in_specs=in_specs,
        out_specs=pl.BlockSpec((tq, LANES), lambda b, h, i: (b * nq + i, branch * 4 + h)),
        out_shape=jax.ShapeDtypeStruct(o.shape, o.dtype),
        scratch_shapes=scratch,
        input_output_aliases={len(args) - 1: 0},
        compiler_params=_params(3), name=name,
    )(*args)


def _local_tiling(seq, span):
    tq = _tile(seq, 512, 64)
    nk = min(seq, tq + 2 * span)
    return tq, nk


def _dedupe(keys):
    first, uniq, tile_map = {}, [], []
    for i, key in enumerate(keys):
        if key not in first:
            first[key] = len(uniq)
            uniq.append(i)
        tile_map.append(first[key])
    return uniq, tuple(tile_map)


def _swa_bias(seq):
    tq, nk = _local_tiling(seq, SWA_WINDOW)
    nq = seq // tq
    tiles = []
    for i in range(nq):
        ks = min(max(i * tq - SWA_WINDOW, 0), seq - nk)
        qp = i * tq + np.arange(tq)[:, None]
        kp = ks + np.arange(nk)[None, :]
        tiles.append(np.where(np.abs(qp - kp) <= SWA_WINDOW, 0.0, NEG_INF).astype(np.float32).T.copy())
    uniq, tile_map = _dedupe([t.tobytes() for t in tiles])
    return jnp.asarray(np.stack([tiles[i] for i in uniq])[None], F32), tile_map, nk, SWA_WINDOW


def _na_bias_kernel(rpb_ref, out_ref, toe_ref, *, row_offsets):
    n_d, n_c = 2 * NA_KH - 1, 2 * NA_KW - 1
    hd = pl.program_id(0)
    shape = (GRID_W, 2 * GRID_W)
    kc = lax.broadcasted_iota(jnp.int32, shape, 0)
    lane = lax.broadcasted_iota(jnp.int32, shape, 1)
    qc = lane & (GRID_W - 1)
    col_start = jnp.clip(qc - NA_KW // 2, 0, GRID_W - NA_KW)
    valid = (kc >= col_start) & (kc < col_start + NA_KW)
    dc = jnp.clip(kc - qc, -(NA_KW - 1), NA_KW - 1) + (NA_KW - 1)
    for d in range(n_d):
        blk = jnp.zeros(shape, F32)
        for j in range(n_c):
            blk = jnp.where(dc == j, rpb_ref[(hd * n_d + d) * n_c + j] * LOG2E, blk)
        toe_ref[d] = jnp.where(valid, blk, NEG_INF)
    toe_ref[n_d] = jnp.full(shape, NEG_INF, F32)
    left = lane < GRID_W
    for u, offs in enumerate(row_offsets):
        for a in range(0, len(offs), 2):
            for b in range(len(offs[a])):
                out_ref[u, b * GRID_W:(b + 1) * GRID_W, a * GRID_W:(a + 2) * GRID_W] = jnp.where(
                    left, toe_ref[offs[a][b]], toe_ref[offs[a + 1][b]])


def _na_bias(rpb, seq):
    rows = seq // GRID_W
    kh = min(NA_KH, rows)
    back_rows = kh // 2
    tq, nk = _local_tiling(seq, back_rows * GRID_W)
    nq, qr_n, kr_n = seq // tq, tq // GRID_W, nk // GRID_W
    assert qr_n % 2 == 0
    n_heads = rpb.shape[0]
    n_d = 2 * NA_KH - 1
    d_idx = []
    for i in range(nq):
        ks_row = min(max(i * qr_n - back_rows, 0), rows - kr_n)
        qr = i * qr_n + np.arange(qr_n)[:, None]
        kr = ks_row + np.arange(kr_n)[None, :]
        start = np.clip(qr - kh // 2, 0, rows - kh)
        ok = (kr >= start) & (kr < start + kh)
        d_idx.append(np.where(ok, kr - qr + (NA_KH - 1), n_d))
    uniq, tile_map = _dedupe([d.tobytes() for d in d_idx])
    row_offsets = tuple(tuple(tuple(int(v) for v in row) for row in d_idx[i]) for i in uniq)
    bias = pl.pallas_call(
        functools.partial(_na_bias_kernel, row_offsets=row_offsets), grid=(n_heads,),
        in_specs=[pl.BlockSpec(memory_space=pltpu.SMEM)],
        out_specs=pl.BlockSpec((None, len(uniq), nk, tq), lambda hd: (hd, 0, 0, 0)),
        out_shape=jax.ShapeDtypeStruct((n_heads, len(uniq), nk, tq), F32),
        scratch_shapes=[pltpu.VMEM((n_d + 1, GRID_W, 2 * GRID_W), F32)],
        compiler_params=_params(1), name="na_bias",
    )(rpb.reshape(-1))
    return bias, tile_map, nk, back_rows * GRID_W


def _ctx_attn_kernel(u_ref, qa_ref, ka_ref, va_ref, sink_ref, lam_ref, g_ref, oin_ref, o_ref, *, cols, lam_init):
    del oin_ref

    def col(name, hd, width=LANES):
        c0 = cols[name] + hd * width
        return u_ref[:, c0:c0 + width]

    for hd in range(MLA_H):
        q = qa_ref[:, hd * 2 * LANES:(hd + 1) * 2 * LANES]
        k = ka_ref[:, hd * 2 * LANES:(hd + 1) * 2 * LANES]
        v = va_ref[:, hd * LANES:(hd + 1) * LANES]
        o_ref[:, hd * LANES:(hd + 1) * LANES] = _softmax_pv([_dot_nt(q, k)], [v], base2=True).astype(BF16)
    grp = SWA_H // SWA_KV_H
    scale = SWA_HD ** -0.5
    for hd in range(SWA_H):
        q = (col("swa_q", hd).astype(F32) * scale).astype(BF16)
        o = _softmax_pv([_dot_nt(q, col("swa_k", hd // grp))], [col("swa_v", hd // grp)], sink=sink_ref[hd])
        o_ref[:, BRANCH_W + hd * LANES:BRANCH_W + (hd + 1) * LANES] = o.astype(BF16)
    scale = NA_HD ** -0.5
    for hd in range(NA_H):
        q = (col("na_q", hd).astype(F32) * scale).astype(BF16)
        o = _softmax_pv([_dot_nt(q, col("na_k", hd))], [col("na_v", hd)])
        o_ref[:, 2 * BRANCH_W + hd * LANES:2 * BRANCH_W + (hd + 1) * LANES] = o.astype(BF16)
    scale = DIFF_DK ** -0.5
    lam = _diff_lambda(lam_ref, lam_init)
    for hd in range(DIFF_H):
        q = col("diff_q", hd).astype(F32) * scale
        k = col("diff_k", hd)
        v = col("diff_v", hd)
        lane = lax.broadcasted_iota(jnp.int32, q.shape, 1)
        outs = []
        for j in range(2):
            qj = jnp.where((lane >= j * DIFF_DK) & (lane < (j + 1) * DIFF_DK), q, 0.0).astype(BF16)
            outs.append(_softmax_pv([_dot_nt(qj, k)], [v]))
        o = _diff_finish(outs[0], outs[1], lam, g_ref[...], lam_init)
        o_ref[:, 3 * BRANCH_W + hd * LANES:3 * BRANCH_W + (hd + 1) * LANES] = o.astype(BF16)


def _ctx_attention(u, cols, qa, ka, va, sink, diff_lambda_l, subln_g, lam_init, o, dims):
    n_lat, seq, n_batch = dims
    n_ctx = (u.shape[0] - n_lat) // n_batch
    blk0 = n_lat // n_ctx
    kern = functools.partial(_ctx_attn_kernel, cols=cols, lam_init=lam_init)
    return pl.pallas_call(
        kern, grid=(n_batch,),
        in_specs=[pl.BlockSpec((n_ctx, u.shape[1]), lambda b: (blk0 + b, 0)),
                  pl.BlockSpec((n_ctx, qa.shape[1]), lambda b: (blk0 + b, 0)),
                  pl.BlockSpec((n_ctx, ka.shape[1]), lambda b: (blk0 + b, 0)),
                  pl.BlockSpec((n_ctx, va.shape[1]), lambda b: (blk0 + b, 0)),
                  pl.BlockSpec(memory_space=pltpu.SMEM),
                  pl.BlockSpec((4, DIFF_DK), lambda b: (0, 0)),
                  pl.BlockSpec((1, DIFF_DV), lambda b: (0, 0)),
                  pl.BlockSpec(memory_space=pl.ANY)],
        out_specs=pl.BlockSpec((n_ctx, o.shape[1]), lambda b: (blk0 + b, 0)),
        out_shape=jax.ShapeDtypeStruct(o.shape, o.dtype),
        input_output_aliases={7: 0},
        compiler_params=_params(1), name="ctx_attention",
    )(u, qa, ka, va, sink, diff_lambda_l, subln_g.reshape(1, -1), o)


MOE_TM = 512
GATHER_ROWS = 256


def _route_plan(top_i, tm):
    t = top_i.shape[0]
    n_assign = t * TOP_K
    n_tiles = (n_assign + N_EXPERTS * (tm - 1)) // tm
    e_flat = top_i.reshape(-1)
    onehot = (e_flat[:, None] == jnp.arange(N_EXPERTS)[None, :]).astype(jnp.int32)
    csum = jnp.cumsum(onehot, axis=0)
    rank = jnp.sum(onehot * csum, axis=1) - 1
    counts = csum[-1]
    padded = ((counts + tm - 1) // tm) * tm
    ends = jnp.cumsum(padded)
    offs = ends - padded
    pos = (jnp.sum(onehot * offs[None, :], axis=1) + rank).astype(jnp.int32)
    src = jnp.zeros((n_tiles * tm,), jnp.int32).at[pos].set(
        jnp.arange(n_assign, dtype=jnp.int32) // TOP_K, unique_indices=True, mode="promise_in_bounds")
    tile_start = jnp.arange(n_tiles, dtype=jnp.int32) * tm
    tile_expert = jnp.minimum(jnp.sum((tile_start[:, None] >= ends[None, :]).astype(jnp.int32), axis=1),
                              N_EXPERTS - 1).astype(jnp.int32)
    tile_valid = (tile_start < ends[-1]).astype(jnp.int32)
    return pos, src, tile_expert, tile_valid


def _row_copies(idx_ref, base, src_hbm, bufs, sems, slot, n_rows, idx_stride):
    def body(r, carry):
        for s, buf in enumerate(bufs):
            row = idx_ref[(base + r) * idx_stride + s]
            pltpu.make_async_copy(src_hbm.at[pl.ds(row, 1), :], buf.at[slot, pl.ds(r, 1), :],
                                  sems.at[slot, s]).start()
        return carry
    lax.fori_loop(0, n_rows, body, 0, unroll=8)


def _wait_rows(src_hbm, bufs, sems, slot, n_rows):
    for s, buf in enumerate(bufs):
        pltpu.make_async_copy(src_hbm.at[pl.ds(0, n_rows), :], buf.at[slot], sems.at[slot, s]).wait()


def _gather_kernel(src_ref, tv_ref, x_hbm, o_ref, x_ref, buf_ref, sem, *, rows):
    i = pl.program_id(0)

    @pl.when(i == 0)
    def _():
        cp = pltpu.make_async_copy(x_hbm, x_ref, sem)
        cp.start()
        cp.wait()

    @pl.when(tv_ref[i] == 1)
    def _():
        def body(r, carry):
            buf_ref[pl.ds(r, 1), :] = x_ref[pl.ds(src_ref[i * rows + r], 1), :]
            return carry
        lax.fori_loop(0, rows, body, 0, unroll=8)
        w = buf_ref[...]
        half = w.shape[1]
        o_ref[:, :half] = lax.bitcast_convert_type(w << 16, F32).astype(BF16)
        o_ref[:, half:] = lax.bitcast_convert_type(w & jnp.uint32(0xFFFF0000), F32).astype(BF16)

    @pl.when(tv_ref[i] == 0)
    def _():
        o_ref[...] = jnp.zeros(o_ref.shape, BF16)


def _gather_rows(xp, src, tile_valid, rows):
    t, half = xp.shape
    n_rows = src.shape[0]
    kern = functools.partial(_gather_kernel, rows=rows)
    return pl.pallas_call(
        kern,
        grid_spec=pltpu.PrefetchScalarGridSpec(
            num_scalar_prefetch=2, grid=(n_rows // rows,),
            in_specs=[pl.BlockSpec(memory_space=pl.ANY)],
            out_specs=pl.BlockSpec((rows, 2 * half), lambda i, s, v: (i, 0)),
            scratch_shapes=[pltpu.VMEM((t, half), jnp.uint32), pltpu.VMEM((rows, half), jnp.uint32),
                            pltpu.SemaphoreType.DMA(())]),
        out_shape=jax.ShapeDtypeStruct((n_rows, 2 * half), BF16),
        compiler_params=_params(1), name="moe_gather",
    )(src, tile_valid, xp)


def _expert_changed(te_ref, i):
    return jnp.logical_or(i == 0, te_ref[i] != te_ref[jnp.maximum(i - 1, 0)])


def _gmm_up_kernel(te_ref, tv_ref, x_ref, w1_ref, w3_ref, o_ref, w1b_ref, w3b_ref):
    i = pl.program_id(1)
    _cast_weights(_expert_changed(te_ref, i), [w1_ref, w3_ref], [w1b_ref, w3b_ref])

    @pl.when(tv_ref[i] == 1)
    def _():
        x = x_ref[...]
        o_ref[...] = (_silu(_dot(x, w1b_ref[...])) * _dot(x, w3b_ref[...])).astype(BF16)

    @pl.when(tv_ref[i] == 0)
    def _():
        o_ref[...] = jnp.zeros(o_ref.shape, BF16)


def _gmm_up(xs, w1, w3, idx, tile_expert, tile_valid, tm):
    r, d = xs.shape
    f = w1.shape[3]
    tn = _tile(f, 512, LANES)
    return pl.pallas_call(
        _gmm_up_kernel,
        grid_spec=pltpu.PrefetchScalarGridSpec(
            num_scalar_prefetch=2, grid=(f // tn, r // tm),
            in_specs=[pl.BlockSpec((tm, d), lambda j, i, te, tv: (i, 0)),
                      pl.BlockSpec((None, None, d, tn), lambda j, i, te, tv: (idx, te[i], 0, j)),
                      pl.BlockSpec((None, None, d, tn), lambda j, i, te, tv: (idx, te[i], 0, j))],
            out_specs=pl.BlockSpec((tm, tn), lambda j, i, te, tv: (i, j)),
            scratch_shapes=[pltpu.VMEM((d, tn), BF16), pltpu.VMEM((d, tn), BF16)]),
        out_shape=jax.ShapeDtypeStruct((r, f), BF16),
        compiler_params=_params(2), name="moe_up",
    )(tile_expert, tile_valid, xs, w1, w3)


def _gmm_down_kernel(te_ref, tv_ref, a_ref, w_ref, o_ref, wb_ref):
    i = pl.program_id(1)
    _cast_weights(_expert_changed(te_ref, i), [w_ref], [wb_ref])

    @pl.when(tv_ref[i] == 1)
    def _():
        o_ref[...] = _dot(a_ref[...], wb_ref[...])

    @pl.when(tv_ref[i] == 0)
    def _():
        o_ref[...] = jnp.zeros(o_ref.shape, F32)


def _gmm_down(a, w2, idx, tile_expert, tile_valid, tm):
    r, f = a.shape
    d = w2.shape[3]
    tn = _tile(d, 1024, LANES)
    return pl.pallas_call(
        _gmm_down_kernel,
        grid_spec=pltpu.PrefetchScalarGridSpec(
            num_scalar_prefetch=2, grid=(d // tn, r // tm),
            in_specs=[pl.BlockSpec((tm, f), lambda j, i, te, tv: (i, 0)),
                      pl.BlockSpec((None, None, f, tn), lambda j, i, te, tv: (idx, te[i], 0, j),
                                   pipeline_mode=pl.Buffered(1))],
            out_specs=pl.BlockSpec((tm, tn), lambda j, i, te, tv: (i, j)),
            scratch_shapes=[pltpu.VMEM((f, tn), BF16)]),
        out_shape=jax.ShapeDtypeStruct((r, d), F32),
        compiler_params=_params(2), name="moe_down",
    )(tile_expert, tile_valid, a, w2)


def _combine_kernel(pos_ref, y_hbm, h_ref, tg_ref, gate_ref, *rest, rows, n_lat, seq, n_batch, follow):
    if follow == "norm":
        g_ref, sh_ref, sc_ref, o_ref, xn_ref, buf0, buf1, sems = rest
    else:
        g_ref, o_ref, buf0, buf1, sems = rest
    i = pl.program_id(0)
    n = pl.num_programs(0)
    slot = i % 2
    bufs = [buf0, buf1]

    @pl.when(i == 0)
    def _():
        _row_copies(pos_ref, 0, y_hbm, bufs, sems, 0, rows, TOP_K)

    @pl.when(i + 1 < n)
    def _():
        _row_copies(pos_ref, (i + 1) * rows, y_hbm, bufs, sems, 1 - slot, rows, TOP_K)

    _wait_rows(y_hbm, bufs, sems, slot, rows)
    grp = _group_of_tile(i, rows, n_lat, seq, n_batch)
    gate = gate_ref[pl.ds(grp, 1), :]
    tg = tg_ref[...]
    mix = tg[:, 0:1] * buf0[slot] + tg[:, 1:2] * buf1[slot]
    h_new = h_ref[...] + gate * mix
    if follow == "norm":
        o_ref[...] = h_new
        xn = _rms(h_new, g_ref[...]) * (1.0 + sc_ref[pl.ds(grp, 1), :]) + sh_ref[pl.ds(grp, 1), :]
        xn_ref[...] = xn.astype(BF16)
    else:
        o_ref[...] = _rms(h_new, g_ref[...])


def _combine(h, y, pos, top_g, mod_l, gate_chunk, dims, n_rows, follow, g, mod_next=None):
    n_lat, seq, n_batch = dims
    d = h.shape[1]
    rows = _tile(math.gcd(seq, h.shape[0] - n_lat), GATHER_ROWS, 8)
    kern = functools.partial(_combine_kernel, rows=rows, n_lat=n_lat, seq=seq, n_batch=n_batch, follow=follow)
    row = lambda i, p: (i, 0)
    in_specs = [pl.BlockSpec(memory_space=pl.ANY),
                pl.BlockSpec((rows, d), row),
                pl.BlockSpec((rows, LANES), row),
                pl.BlockSpec((8, d), lambda i, p: (0, gate_chunk)),
                pl.BlockSpec((1, d), lambda i, p: (0, 0))]
    args = [pos, y, h, top_g, mod_l, g.reshape(1, d)]
    if follow == "norm":
        in_specs += [pl.BlockSpec((8, d), lambda i, p: (0, 0)), pl.BlockSpec((8, d), lambda i, p: (0, 1))]
        args += [mod_next, mod_next]
        out_specs = [pl.BlockSpec((rows, d), row), pl.BlockSpec((rows, d), row)]
        out_shape = [jax.ShapeDtypeStruct(h.shape, F32), jax.ShapeDtypeStruct((n_rows, d), BF16)]
        aliases = {2: 0}
    else:
        out_specs = pl.BlockSpec((rows, d), row)
        out_shape = jax.ShapeDtypeStruct((n_rows, d), F32)
        aliases = {}
    return pl.pallas_call(
        kern,
        grid_spec=pltpu.PrefetchScalarGridSpec(
            num_scalar_prefetch=1, grid=(n_rows // rows,), in_specs=in_specs, out_specs=out_specs,
            scratch_shapes=[pltpu.VMEM((2, rows, d), F32), pltpu.VMEM((2, rows, d), F32),
                            pltpu.SemaphoreType.DMA((2, 2))]),
        out_shape=out_shape,
        input_output_aliases=aliases,
        compiler_params=_params(1), name="moe_combine",
    )(*args)


def kernel(x, c, ctx, c_ctx, mod_w, mod_b, norm1_g, norm2_g, w_in, mla_q_norm, mla_kv_norm, mla_w_uq, mla_w_ukv,
           swa_sink, na_rpb, diff_lambda, diff_subln_g, w_branch, w_out, ffn_w1, ffn_w3, ffn_w2, moe_router,
           moe_w1, moe_w3, moe_w2, final_norm_g):
    n_batch, seq, d = x.shape
    n_ctx = ctx.shape[1]
    depth = mod_w.shape[0]
    n_lat = n_batch * seq
    dims = (n_lat, seq, n_batch)

    t = n_lat + n_batch * n_ctx
    h, xn, out = None, None, None
    c8 = jnp.concatenate([c, c_ctx[None, :], jnp.zeros((8 - n_batch - 1, d), F32)], axis=0)
    mod = _mod_all(c8, mod_w, mod_b)

    cols, width_u, in_runs = _in_layout(d)
    w_in_p = _pad_w_in(w_in, width_u, in_runs)
    w_uq = mla_w_uq.reshape(depth, -1, MLA_H, MLA_NOPE + MLA_ROPE)
    w_uq_p = jnp.pad(w_uq, ((0, 0), (0, 0), (0, 0), (0, 2 * LANES - MLA_NOPE - MLA_ROPE)))
    w_uq_p = w_uq_p.reshape(depth, -1, MLA_H * 2 * LANES)
    rope64 = _rope_tables(seq, MLA_ROPE)
    rope128 = _rope_tables(seq, SWA_HD)
    swa_plan = _swa_bias(seq)
    router_t = jnp.swapaxes(moe_router, 1, 2)

    for l in range(depth):
        need_ctx = l < depth - 1
        lam_init = 0.8 - 0.6 * math.exp(-0.3 * l)
        mod_l = mod[l]

        n_rows = t if need_ctx else n_lat
        if h is None:
            h, xn = _join_norm_mod(x.reshape(n_lat, d), ctx.reshape(n_batch * n_ctx, d), norm1_g[l], mod_l, 0, 1, dims)
        elif xn is None:
            xn = _norm_mod(h, norm1_g[l], mod_l, 0, 1, dims)
        u = _project_in(xn, w_in_p, l)

        qa, ka, va = _mla_prep(u, cols, mla_q_norm[l], mla_kv_norm[l], w_uq_p[l], mla_w_ukv[l], rope64, dims)
        o = jnp.zeros((t, N_BRANCH * BRANCH_W), BF16)
        o = _mla_attention(qa, ka, va, o, dims, 0)
        o = _local_attention(u, cols["swa_q"], cols["swa_k"], cols["swa_v"], SWA_H, SWA_KV_H, swa_plan,
                             rope128, swa_sink[l], o, dims, 1, "swa_attention")
        o = _local_attention(u, cols["na_q"], cols["na_k"], cols["na_v"], NA_H, NA_H, _na_bias(na_rpb[l], seq),
                             None, None, o, dims, 2, "na_attention")
        o = _diff_attention(u, cols, rope64, diff_lambda[l], diff_subln_g[l], lam_init, o, dims, 3)
        if need_ctx:
            o = _ctx_attention(u, cols, qa, ka, va, swa_sink[l], diff_lambda[l], diff_subln_g[l], lam_init, o, dims)

        merged = _merge(o, u, cols["gates"], w_branch, l, n_rows)

        i = l // 2
        xn = None
        if l % 2 == 0:
            h, xn2 = _project_out_norm(merged, w_out, l, h, mod_l, norm2_g[l], dims, n_rows)
            a = _swiglu_up(xn2, ffn_w1, ffn_w3, i)
            h = _project_residual(a, ffn_w2, i, h, mod_l, 5, dims)
        else:
            h, xp, top_g, top_i = _project_out_norm(merged, w_out, l, h, mod_l, norm2_g[l], dims, n_rows,
                                                    router=router_t[i])
            pos, src, tile_expert, tile_valid = _route_plan(top_i[:, :TOP_K], MOE_TM)
            xs = _gather_rows(xp, src, tile_valid, MOE_TM)
            a = _gmm_up(xs, moe_w1, moe_w3, i, tile_expert, tile_valid, MOE_TM)
            y = _gmm_down(a, moe_w2, i, tile_expert, tile_valid, MOE_TM)
            if need_ctx:
                h, xn = _combine(h, y, pos, top_g, mod_l, 5, dims, n_rows, "norm", norm1_g[l + 1], mod[l + 1])
            else:
                out = _combine(h, y, pos, top_g, mod_l, 5, dims, n_rows, "final", final_norm_g)

    if out is None:
        out = _final_norm(h, final_norm_g, n_lat)
    return out.reshape(n_batch, seq, d)
```

```python
import functools
import math

import jax
import jax.numpy as jnp
import numpy as np
from jax import lax
from jax.experimental import pallas as pl
from jax.experimental.pallas import tpu as pltpu

F32 = jnp.float32
BF16 = jnp.bfloat16

GRID_W = 64
ROPE_THETA = 10000.0
NORM_EPS = 1e-6
NEG_INF = -1e30
LOG2E = math.log2(math.e)

MLA_H, MLA_NOPE, MLA_ROPE, MLA_V = 4, 128, 64, 128
SWA_H, SWA_KV_H, SWA_HD, SWA_WINDOW = 4, 2, 128, 128
NA_H, NA_HD, NA_KH, NA_KW = 4, 128, 8, 16
DIFF_H, DIFF_DK, DIFF_DV = 4, 64, 128
N_BRANCH, BRANCH_W = 4, 512
N_EXPERTS, TOP_K = 8, 2
LANES = 128

VMEM_LIMIT = 56 * 1024 * 1024


def _params(n_axes):
    return pltpu.CompilerParams(dimension_semantics=("arbitrary",) * n_axes, vmem_limit_bytes=VMEM_LIMIT)


def _tile(n, pref, mult):
    if n <= pref:
        return n
    t = (pref // mult) * mult
    while t > mult and n % t:
        t -= mult
    assert n % t == 0, (n, pref, mult)
    return t


def _dot(a, b):
    return jnp.dot(a, b, preferred_element_type=F32)


def _dot_nt(a, b):
    return lax.dot_general(a, b, (((1,), (1,)), ((), ())), preferred_element_type=F32)


def _silu(x):
    return x * (1.0 / (1.0 + jnp.exp(-x)))


def _sigmoid(x):
    return 0.5 * (1.0 + jnp.tanh(0.5 * x))


def _mod_kernel(c_ref, w_ref, b_ref, o_ref):
    a = _silu(c_ref[...]).astype(BF16)
    o_ref[...] = _dot(a, w_ref[...].astype(BF16)) + b_ref[...]


def _mod_all(c8, mod_w, mod_b):
    n_l, d, n = mod_w.shape
    tn = _tile(n, 1024, LANES)
    return pl.pallas_call(
        _mod_kernel,
        grid=(n_l, n // tn),
        in_specs=[
            pl.BlockSpec((8, d), lambda l, j: (0, 0)),
            pl.BlockSpec((None, d, tn), lambda l, j: (l, 0, j)),
            pl.BlockSpec((None, 1, tn), lambda l, j: (l, 0, j)),
        ],
        out_specs=pl.BlockSpec((None, 8, tn), lambda l, j: (l, 0, j)),
        out_shape=jax.ShapeDtypeStruct((n_l, 8, n), F32),
        compiler_params=_params(2),
        name="mod_all",
    )(c8, mod_w, mod_b.reshape(n_l, 1, n))


def _group_of_tile(i, tm, n_lat, seq, n_batch):
    row0 = i * tm
    return jnp.where(row0 < n_lat, row0 // seq, n_batch)


def _rms(x, g):
    return x * lax.rsqrt(jnp.mean(x * x, axis=-1, keepdims=True) + NORM_EPS) * g


def _pack_bf16_pairs(xn):
    bits = lax.bitcast_convert_type(xn.astype(BF16).astype(F32), jnp.uint32)
    half = bits.shape[1] // 2
    return (bits[:, half:] & jnp.uint32(0xFFFF0000)) | (bits[:, :half] >> 16)


def _route_top2(xn, router_t):
    lane = lax.broadcasted_iota(jnp.int32, (xn.shape[0], LANES), 1).astype(F32)
    logits = jnp.full((xn.shape[0], LANES), -jnp.inf, F32)
    for e in range(N_EXPERTS):
        logit_e = jnp.sum(xn * router_t[e:e + 1, :], axis=-1, keepdims=True)
        logits = jnp.where(lane == float(e), logit_e, logits)
    m1 = jnp.max(logits, axis=-1, keepdims=True)
    i1 = jnp.min(jnp.where(logits == m1, lane, float(LANES)), axis=-1, keepdims=True)
    rest_l = jnp.where(lane == i1, -jnp.inf, logits)
    m2 = jnp.max(rest_l, axis=-1, keepdims=True)
    i2 = jnp.min(jnp.where(rest_l == m2, lane, float(LANES)), axis=-1, keepdims=True)
    e2 = jnp.exp(m2 - m1)
    den = 1.0 + e2
    top_g = jnp.where(lane == 0.0, 1.0 / den, jnp.where(lane == 1.0, e2 / den, 0.0))
    top_i = jnp.where(lane == 0.0, i1, jnp.where(lane == 1.0, i2, 0.0)).astype(jnp.int32)
    return top_g, top_i


def _norm_kernel(h_ref, g_ref, sh_ref, sc_ref, o_ref, *, tm, n_lat, seq, n_batch):
    grp = _group_of_tile(pl.program_id(0), tm, n_lat, seq, n_batch)
    xn = _rms(h_ref[...], g_ref[...]) * (1.0 + sc_ref[pl.ds(grp, 1), :]) + sh_ref[pl.ds(grp, 1), :]
    o_ref[...] = xn.astype(BF16)


def _norm_mod(h, g, mod_l, shift_chunk, scale_chunk, dims):
    n_lat, seq, n_batch = dims
    t, d = h.shape
    tm = _tile(math.gcd(seq, t - n_lat), 512, 8)
    kern = functools.partial(_norm_kernel, tm=tm, n_lat=n_lat, seq=seq, n_batch=n_batch)
    return pl.pallas_call(
        kern, grid=(t // tm,),
        in_specs=[pl.BlockSpec((tm, d), lambda i: (i, 0)),
                  pl.BlockSpec((1, d), lambda i: (0, 0)),
                  pl.BlockSpec((8, d), lambda i: (0, shift_chunk)),
                  pl.BlockSpec((8, d), lambda i: (0, scale_chunk))],
        out_specs=pl.BlockSpec((tm, d), lambda i: (i, 0)),
        out_shape=jax.ShapeDtypeStruct((t, d), BF16),
        compiler_params=_params(1), name="norm_mod",
    )(h, g.reshape(1, d), mod_l, mod_l)


def _join_norm_kernel(x_ref, c_ref, g_ref, sh_ref, sc_ref, h_ref, o_ref, *, tm, n_lat, seq, n_batch):
    i = pl.program_id(0)
    grp = _group_of_tile(i, tm, n_lat, seq, n_batch)
    x = jnp.where(i * tm < n_lat, x_ref[...], c_ref[...])
    h_ref[...] = x
    y = x * lax.rsqrt(jnp.mean(x * x, axis=-1, keepdims=True) + NORM_EPS) * g_ref[...]
    o_ref[...] = (y * (1.0 + sc_ref[pl.ds(grp, 1), :]) + sh_ref[pl.ds(grp, 1), :]).astype(BF16)


def _join_norm_mod(x_lat, x_ctx, g, mod_l, shift_chunk, scale_chunk, dims):
    n_lat, seq, n_batch = dims
    d = x_lat.shape[1]
    t = n_lat + x_ctx.shape[0]
    tm = _tile(math.gcd(seq, t - n_lat), 512, 8)
    lat_tiles = n_lat // tm
    kern = functools.partial(_join_norm_kernel, tm=tm, n_lat=n_lat, seq=seq, n_batch=n_batch)
    return pl.pallas_call(
        kern, grid=(t // tm,),
        in_specs=[pl.BlockSpec((tm, d), lambda i: (jnp.minimum(i, lat_tiles - 1), 0)),
                  pl.BlockSpec((tm, d), lambda i: (jnp.maximum(i - lat_tiles, 0), 0)),
                  pl.BlockSpec((1, d), lambda i: (0, 0)),
                  pl.BlockSpec((8, d), lambda i: (0, shift_chunk)),
                  pl.BlockSpec((8, d), lambda i: (0, scale_chunk))],
        out_specs=[pl.BlockSpec((tm, d), lambda i: (i, 0)), pl.BlockSpec((tm, d), lambda i: (i, 0))],
        out_shape=[jax.ShapeDtypeStruct((t, d), F32), jax.ShapeDtypeStruct((t, d), BF16)],
        compiler_params=_params(1), name="join_norm_mod",
    )(x_lat, x_ctx, g.reshape(1, d), mod_l, mod_l)


def _final_norm_kernel(h_ref, g_ref, o_ref):
    x = h_ref[...]
    o_ref[...] = x * lax.rsqrt(jnp.mean(x * x, axis=-1, keepdims=True) + NORM_EPS) * g_ref[...]


def _final_norm(h, g, n_rows):
    d = h.shape[1]
    tm = _tile(n_rows, 256, 8)
    return pl.pallas_call(
        _final_norm_kernel, grid=(n_rows // tm,),
        in_specs=[pl.BlockSpec((tm, d), lambda i: (i, 0)), pl.BlockSpec((1, d), lambda i: (0, 0))],
        out_specs=pl.BlockSpec((tm, d), lambda i: (i, 0)),
        out_shape=jax.ShapeDtypeStruct((n_rows, d), F32),
        compiler_params=_params(1), name="final_norm",
    )(h, g.reshape(1, d))


def _cast_weights(first, w_refs, wb_refs):
    @pl.when(first)
    def _():
        for w_ref, wb_ref in zip(w_refs, wb_refs):
            wb_ref[...] = w_ref[...].astype(BF16)


IN_TN = 1024
ROW_TILE = 1024


def _in_layout(d_model):
    pieces = [("cq", 512, 512), ("ckv", 256, 256), ("krope", MLA_ROPE, LANES),
              ("swa_q", SWA_H * SWA_HD, LANES), ("swa_k", SWA_KV_H * SWA_HD, LANES),
              ("swa_v", SWA_KV_H * SWA_HD, LANES), ("na_q", NA_H * NA_HD, LANES), ("na_k", NA_H * NA_HD, LANES),
              ("na_v", NA_H * NA_HD, LANES), ("diff_q", DIFF_H * 2 * DIFF_DK, LANES),
              ("diff_k", DIFF_H * 2 * DIFF_DK, LANES), ("diff_v", DIFF_H * DIFF_DV, LANES),
              ("gates", N_BRANCH * d_model, 512)]
    natural, nat = {}, 0
    for name, width, _ in pieces:
        natural[name] = nat
        nat += width
    cols, runs, u = {}, [], 0
    for name, width, align in pieces[-1:] + pieces[:-1]:
        u = -(-u // align) * align
        cols[name] = u
        runs.append((u, natural[name], width))
        u += width
    width_u = -(-u // IN_TN) * IN_TN
    return cols, width_u, tuple(runs)


def _pad_w_in(w_in, width_u, runs):
    assert runs[0][0] == 0
    ends = [r[0] for r in runs[1:]] + [width_u]
    parts = []
    for (u0, n0, w), end in zip(runs, ends):
        piece = w_in[:, :, n0:n0 + w].astype(BF16)
        if end > u0 + w:
            piece = jnp.pad(piece, ((0, 0), (0, 0), (0, end - u0 - w)))
        parts.append(piece)
    return jnp.concatenate(parts, axis=2)


def _proj_kernel(x_ref, w_ref, o_ref):
    o_ref[...] = _dot(x_ref[...], w_ref[...]).astype(o_ref.dtype)


def _project_in(xn, w_p, layer):
    t, d = xn.shape
    n = w_p.shape[2]
    tm = min(ROW_TILE, t)
    return pl.pallas_call(
        _proj_kernel, grid=(n // IN_TN, pl.cdiv(t, tm)),
        in_specs=[pl.BlockSpec((tm, d), lambda j, i: (i, 0)),
                  pl.BlockSpec((None, d, IN_TN), lambda j, i: (layer, 0, j))],
        out_specs=pl.BlockSpec((tm, IN_TN), lambda j, i: (i, j)),
        out_shape=jax.ShapeDtypeStruct((t, n), BF16),
        compiler_params=_params(2), name="project_in",
    )(xn, w_p)


def _proj_res_kernel(x_ref, w_ref, h_ref, gate_ref, o_ref, wb_ref, *, tm, n_lat, seq, n_batch):
    i = pl.program_id(1)
    _cast_weights(i == 0, [w_ref], [wb_ref])
    grp = _group_of_tile(i, tm, n_lat, seq, n_batch)
    gate = gate_ref[pl.ds(grp, 1), :]
    o_ref[...] = h_ref[...] + gate * _dot(x_ref[...], wb_ref[...])


def _project_residual(x, w_l, layer, h, mod_l, gate_chunk, dims):
    n_lat, seq, n_batch = dims
    t, k = x.shape
    n = w_l.shape[2]
    tm = _tile(math.gcd(seq, t - n_lat), 512, 8)
    tn = _tile(n, 512, LANES)
    kern = functools.partial(_proj_res_kernel, tm=tm, n_lat=n_lat, seq=seq, n_batch=n_batch)
    gate_blk = gate_chunk * (n // tn)
    return pl.pallas_call(
        kern, grid=(n // tn, t // tm),
        in_specs=[pl.BlockSpec((tm, k), lambda j, i: (i, 0)),
                  pl.BlockSpec((None, k, tn), lambda j, i: (layer, 0, j)),
                  pl.BlockSpec((tm, tn), lambda j, i: (i, j)),
                  pl.BlockSpec((8, tn), lambda j, i: (0, gate_blk + j))],
        out_specs=pl.BlockSpec((tm, tn), lambda j, i: (i, j)),
        out_shape=jax.ShapeDtypeStruct(h.shape, F32),
        scratch_shapes=[pltpu.VMEM((k, tn), BF16)],
        input_output_aliases={2: 0},
        compiler_params=_params(2), name="project_residual",
    )(x, w_l, h, mod_l)


def _out_norm_kernel(x_ref, w_ref, h_ref, mod_ref, g_ref, *rest, tm, n_lat, seq, n_batch, chunks, route):
    d = h_ref.shape[1]
    if route:
        r_ref, o_ref, xn_ref, tg_ref, ti_ref, wb_ref = rest
    else:
        o_ref, xn_ref, wb_ref = rest
    i = pl.program_id(0)
    _cast_weights(i == 0, [w_ref], [wb_ref])
    grp = _group_of_tile(i, tm, n_lat, seq, n_batch)
    gate, shift, scale = (mod_ref[pl.ds(grp, 1), c * d:(c + 1) * d] for c in chunks)
    h_new = h_ref[...] + gate * _dot(x_ref[...], wb_ref[...])
    o_ref[...] = h_new
    xn = _rms(h_new, g_ref[...]) * (1.0 + scale) + shift
    if route:
        xn_ref[...] = _pack_bf16_pairs(xn)
        tg_ref[...], ti_ref[...] = _route_top2(xn, r_ref[...])
    else:
        xn_ref[...] = xn.astype(BF16)


def _project_out_norm(x, w_l, layer, h, mod_l, g, dims, n_rows, router=None):
    n_lat, seq, n_batch = dims
    k = x.shape[1]
    d = h.shape[1]
    assert w_l.shape[2] == d
    tm = _tile(math.gcd(seq, h.shape[0] - n_lat), 256, 8)
    route = router is not None
    kern = functools.partial(_out_norm_kernel, tm=tm, n_lat=n_lat, seq=seq, n_batch=n_batch,
                             chunks=(2, 3, 4), route=route)
    row = lambda i: (i, 0)
    fixed = lambda i: (0, 0)
    in_specs = [pl.BlockSpec((tm, k), row),
                pl.BlockSpec((None, k, d), lambda i: (layer, 0, 0), pipeline_mode=pl.Buffered(1)),
                pl.BlockSpec((tm, d), row),
                pl.BlockSpec(mod_l.shape, fixed),
                pl.BlockSpec((1, d), fixed)]
    args = [x, w_l, h, mod_l, g.reshape(1, d)]
    out_specs = [pl.BlockSpec((tm, d), row)]
    out_shape = [jax.ShapeDtypeStruct(h.shape, F32)]
    if route:
        in_specs.append(pl.BlockSpec((N_EXPERTS, d), fixed))
        args.append(router)
        out_specs += [pl.BlockSpec((tm, d // 2), row), pl.BlockSpec((tm, LANES), row), pl.BlockSpec((tm, LANES), row)]
        out_shape += [jax.ShapeDtypeStruct((n_rows, d // 2), jnp.uint32),
                      jax.ShapeDtypeStruct((n_rows, LANES), F32),
                      jax.ShapeDtypeStruct((n_rows, LANES), jnp.int32)]
    else:
        out_specs.append(pl.BlockSpec((tm, d), row))
        out_shape.append(jax.ShapeDtypeStruct((n_rows, d), BF16))
    return pl.pallas_call(
        kern, grid=(n_rows // tm,), in_specs=in_specs, out_specs=out_specs, out_shape=out_shape,
        scratch_shapes=[pltpu.VMEM((k, d), BF16)],
        input_output_aliases={2: 0},
        compiler_params=_params(1), name="project_out_norm",
    )(*args)


def _swiglu_up_kernel(x_ref, w1_ref, w3_ref, o_ref, w1b_ref, w3b_ref):
    _cast_weights(pl.program_id(1) == 0, [w1_ref, w3_ref], [w1b_ref, w3b_ref])
    x = x_ref[...]
    a = _dot(x, w1b_ref[...])
    b = _dot(x, w3b_ref[...])
    o_ref[...] = (_silu(a) * b).astype(BF16)


def _swiglu_up(xn, w1, w3, idx):
    t, d = xn.shape
    f = w1.shape[2]
    tm = min(ROW_TILE, t)
    tn = _tile(f, 512, LANES)
    return pl.pallas_call(
        _swiglu_up_kernel, grid=(f // tn, pl.cdiv(t, tm)),
        in_specs=[pl.BlockSpec((tm, d), lambda j, i: (i, 0)),
                  pl.BlockSpec((None, d, tn), lambda j, i: (idx, 0, j)),
                  pl.BlockSpec((None, d, tn), lambda j, i: (idx, 0, j))],
        out_specs=pl.BlockSpec((tm, tn), lambda j, i: (i, j)),
        out_shape=jax.ShapeDtypeStruct((t, f), BF16),
        scratch_shapes=[pltpu.VMEM((d, tn), BF16), pltpu.VMEM((d, tn), BF16)],
        compiler_params=_params(2), name="swiglu_up",
    )(xn, w1, w3)


def _merge_kernel(o_ref, g0_ref, g1_ref, g2_ref, g3_ref, w_ref, out_ref, wb_ref):
    _cast_weights(pl.program_id(0) == 0, [w_ref], [wb_ref])
    acc = None
    for m, g_ref in enumerate((g0_ref, g1_ref, g2_ref, g3_ref)):
        y = _sigmoid(g_ref[...]).astype(F32) * _dot(o_ref[:, m * BRANCH_W:(m + 1) * BRANCH_W], wb_ref[m])
        acc = y if acc is None else acc + y
    out_ref[...] = acc.astype(BF16)


def _merge(o, u, gate_col, w_branch, layer, n_rows):
    t = n_rows
    d = w_branch.shape[3]
    tm = _tile(math.gcd(t, o.shape[0]), 256, 8)
    assert gate_col % d == 0
    gate_blk = gate_col // d

    def gate_spec(m):
        return pl.BlockSpec((tm, d), lambda i: (i, gate_blk + m))

    return pl.pallas_call(
        _merge_kernel, grid=(t // tm,),
        in_specs=[pl.BlockSpec((tm, N_BRANCH * BRANCH_W), lambda i: (i, 0)),
                  gate_spec(0), gate_spec(1), gate_spec(2), gate_spec(3),
                  pl.BlockSpec((None, N_BRANCH, BRANCH_W, d), lambda i: (layer, 0, 0, 0),
                               pipeline_mode=pl.Buffered(1))],
        out_specs=pl.BlockSpec((tm, d), lambda i: (i, 0)),
        out_shape=jax.ShapeDtypeStruct((t, d), BF16),
        scratch_shapes=[pltpu.VMEM((N_BRANCH, BRANCH_W, d), BF16)],
        compiler_params=_params(1), name="merge",
    )(o, u, u, u, u, w_branch)


def _rope_tables(n, dim):
    pos = jnp.arange(n, dtype=jnp.int32)
    row = (pos // GRID_W).astype(F32)
    col = (pos % GRID_W).astype(F32)
    quarter = dim // 4
    inv_freq = ROPE_THETA ** (-jnp.arange(quarter, dtype=F32) / quarter)
    ang_r = row[:, None] * inv_freq
    ang_c = col[:, None] * inv_freq
    ang = jnp.concatenate([ang_r, ang_r, ang_c, ang_c], axis=-1)
    reps = LANES // dim
    cos = jnp.tile(jnp.cos(ang), (1, reps))
    sin = jnp.tile(jnp.sin(ang), (1, reps))
    lane = jnp.arange(LANES)[None, :]
    first = (lane & quarter) == 0
    return cos, jnp.where(first, -sin, 0.0), jnp.where(first, 0.0, sin)


def _rope(x, cos, sin_a, sin_b, quarter):
    return (x * cos + pltpu.roll(x, LANES - quarter, 1) * sin_a + pltpu.roll(x, quarter, 1) * sin_b)


def _mla_prep_kernel(cq_ref, ckv_ref, kr_ref, qg_ref, kvg_ref, wq_ref, wkv_ref, cos_ref, sa_ref, sb_ref,
                     q_ref, k_ref, v_ref, *, n_lat_tiles):
    is_lat = pl.program_id(0) < n_lat_tiles
    cos = jnp.where(is_lat, cos_ref[...], 1.0)
    sa = jnp.where(is_lat, sa_ref[...], 0.0)
    sb = jnp.where(is_lat, sb_ref[...], 0.0)
    quarter = MLA_ROPE // 4

    def rms(x_ref, g_ref):
        x = x_ref[...].astype(F32)
        return (x * lax.rsqrt(jnp.mean(x * x, axis=-1, keepdims=True) + NORM_EPS) * g_ref[...]).astype(BF16)

    scale = (MLA_NOPE + MLA_ROPE) ** -0.5 * LOG2E
    q = _dot(rms(cq_ref, qg_ref), wq_ref[...].astype(BF16)) * scale
    kv = _dot(rms(ckv_ref, kvg_ref), wkv_ref[...].astype(BF16))
    kr = _rope(kr_ref[...].astype(F32), cos, sa, sb, quarter).astype(BF16)
    for hd in range(MLA_H):
        b = hd * 2 * LANES
        q_ref[:, b:b + LANES] = q[:, b:b + LANES].astype(BF16)
        q_ref[:, b + LANES:b + 2 * LANES] = _rope(q[:, b + LANES:b + 2 * LANES], cos, sa, sb, quarter).astype(BF16)
        k_ref[:, b:b + LANES] = kv[:, b:b + LANES].astype(BF16)
        k_ref[:, b + LANES:b + 2 * LANES] = kr
        v_ref[:, hd * LANES:(hd + 1) * LANES] = kv[:, b + LANES:b + 2 * LANES].astype(BF16)


def _mla_prep(u, cols, q_norm, kv_norm, w_uq_p, w_ukv, rope64, dims):
    n_lat, seq, _ = dims
    t = u.shape[0]
    tm = _tile(math.gcd(seq, t - n_lat), 256, 16)
    n_lat_tiles = n_lat // tm
    seq_tiles = seq // tm
    qw = MLA_H * 2 * LANES

    def tab_spec():
        return pl.BlockSpec((tm, LANES), lambda i: (jnp.where(i < n_lat_tiles, i % seq_tiles, 0), 0))

    kern = functools.partial(_mla_prep_kernel, n_lat_tiles=n_lat_tiles)
    return pl.pallas_call(
        kern, grid=(t // tm,),
        in_specs=[pl.BlockSpec((tm, 512), lambda i: (i, cols["cq"] // 512)),
                  pl.BlockSpec((tm, 256), lambda i: (i, cols["ckv"] // 256)),
                  pl.BlockSpec((tm, LANES), lambda i: (i, cols["krope"] // LANES)),
                  pl.BlockSpec((1, 512), lambda i: (0, 0)),
                  pl.BlockSpec((1, 256), lambda i: (0, 0)),
                  pl.BlockSpec((512, qw), lambda i: (0, 0)),
                  pl.BlockSpec((256, qw), lambda i: (0, 0)),
                  tab_spec(), tab_spec(), tab_spec()],
        out_specs=[pl.BlockSpec((tm, qw), lambda i: (i, 0)),
                   pl.BlockSpec((tm, qw), lambda i: (i, 0)),
                   pl.BlockSpec((tm, MLA_H * MLA_V), lambda i: (i, 0))],
        out_shape=[jax.ShapeDtypeStruct((t, qw), BF16),
                   jax.ShapeDtypeStruct((t, qw), BF16),
                   jax.ShapeDtypeStruct((t, MLA_H * MLA_V), BF16)],
        compiler_params=_params(1), name="mla_prep",
    )(u, u, u, q_norm.reshape(1, -1), kv_norm.reshape(1, -1), w_uq_p, w_ukv, *rope64)


def _softmax_pv(scores, values, sink=None, base2=False):
    ex = jnp.exp2 if base2 else jnp.exp
    m = None
    for s in scores:
        mi = jnp.max(s, axis=-1, keepdims=True)
        m = mi if m is None else jnp.maximum(m, mi)
    if sink is not None:
        m = jnp.maximum(m, sink)
    den = None
    acc = None
    for s, v in zip(scores, values):
        p = ex(s - m)
        li = jnp.sum(p, axis=-1, keepdims=True)
        den = li if den is None else den + li
        o = _dot(p.astype(BF16), v)
        acc = o if acc is None else acc + o
    if sink is not None:
        den = den + ex(sink - m)
    return acc / den


def _diff_lambda(lam_ref, lam_init):
    lp = lam_ref[...]
    return (jnp.exp(jnp.sum(lp[0:1, :] * lp[1:2, :], axis=-1, keepdims=True))
            - jnp.exp(jnp.sum(lp[2:3, :] * lp[3:4, :], axis=-1, keepdims=True)) + lam_init)


def _diff_finish(o1, o2, lam, g, lam_init):
    o = o1 - lam * o2
    y = o * lax.rsqrt(jnp.mean(o * o, axis=-1, keepdims=True) + NORM_EPS) * g
    return y * (1.0 - lam_init)


DENSE_SUB = 256
DENSE_CHUNK = 256


def _transpose_to(dst_ref, src_ref):
    dst_ref[...] = src_ref[...].astype(F32).T.astype(BF16)


def _ref_segment(k_ref, vt_ref):
    return (k_ref.shape[0], lambda c0, w: k_ref[c0:c0 + w, :], lambda c0, w: vt_ref[:, c0:c0 + w], None)


def _scores_pass(q, segments, s_ref, sink=None):
    rows = q.shape[0]
    m_run, off = None, 0
    for n, keys, _, bias in segments:
        for c0 in range(0, n, DENSE_CHUNK):
            w = min(DENSE_CHUNK, n - c0)
            s = _dot_nt(keys(c0, w), q)
            if bias is not None:
                s = s + bias(c0, w)
            s_ref[off + c0:off + c0 + w, :] = s
            part = jnp.max(s.reshape(w // 8, 8, rows), axis=0)
            m_run = part if m_run is None else jnp.maximum(m_run, part)
        off += n
    m = jnp.max(m_run, axis=0, keepdims=True)
    return m if sink is None else jnp.maximum(m, sink)


def _pv_pass(segments, s_ref, m, sink=None):
    rows = m.shape[1]
    l_run, acc, off = None, None, 0
    for n, _, values_t, _ in segments:
        for c0 in range(0, n, DENSE_CHUNK):
            w = min(DENSE_CHUNK, n - c0)
            p = jnp.exp2(s_ref[off + c0:off + c0 + w, :] - m)
            part = jnp.sum(p.reshape(w // 8, 8, rows), axis=0)
            l_run = part if l_run is None else l_run + part
            pv = _dot(values_t(c0, w), p.astype(BF16))
            acc = pv if acc is None else acc + pv
        off += n
    den = jnp.sum(l_run, axis=0, keepdims=True)
    if sink is not None:
        den = den + jnp.exp2(sink - m)
    return (acc * (1.0 / den)).T


def _pipelined(units, first_pass, second_pass):
    state = first_pass(units[0], 0)
    for k, unit in enumerate(units):
        nxt = first_pass(units[k + 1], (k + 1) % 2) if k + 1 < len(units) else None
        second_pass(unit, k % 2, state)
        state = nxt


def _mla_attn_kernel(q_ref, kc_ref, kl_ref, vc_ref, vl_ref, oin_ref, o_ref, vtc_ref, vtl_ref, sa_ref, sb_ref, *, sub):
    del oin_ref
    s_refs = (sa_ref, sb_ref)

    @pl.when(pl.program_id(2) == 0)
    def _():
        _transpose_to(vtc_ref, vc_ref)
        _transpose_to(vtl_ref, vl_ref)

    segments = [_ref_segment(kc_ref, vtc_ref), _ref_segment(kl_ref, vtl_ref)]

    def first_pass(r0, slot):
        return _scores_pass(q_ref[r0:r0 + sub, :], segments, s_refs[slot])

    def second_pass(r0, slot, m):
        o_ref[r0:r0 + sub, :] = _pv_pass(segments, s_refs[slot], m).astype(BF16)

    _pipelined(list(range(0, q_ref.shape[0], sub)), first_pass, second_pass)


def _mla_attention(qa, ka, va, o, dims, branch):
    n_lat, seq, n_batch = dims
    n_ctx = (qa.shape[0] - n_lat) // n_batch
    tq = _tile(seq, 1024, 16)
    sub = _tile(tq, DENSE_SUB, 16)
    nq = seq // tq
    ctx_blk0 = n_lat // n_ctx
    kw = 2 * LANES
    return pl.pallas_call(
        functools.partial(_mla_attn_kernel, sub=sub), grid=(n_batch, MLA_H, nq),
        scratch_shapes=[pltpu.VMEM((MLA_V, n_ctx), BF16), pltpu.VMEM((MLA_V, seq), BF16),
                        pltpu.VMEM((n_ctx + seq, sub), F32), pltpu.VMEM((n_ctx + seq, sub), F32)],
        in_specs=[pl.BlockSpec((tq, kw), lambda b, h, i: (b * nq + i, h)),
                  pl.BlockSpec((n_ctx, kw), lambda b, h, i: (ctx_blk0 + b, h)),
                  pl.BlockSpec((seq, kw), lambda b, h, i: (b, h)),
                  pl.BlockSpec((n_ctx, LANES), lambda b, h, i: (ctx_blk0 + b, h)),
                  pl.BlockSpec((seq, LANES), lambda b, h, i: (b, h)),
                  pl.BlockSpec(memory_space=pl.ANY)],
        out_specs=pl.BlockSpec((tq, LANES), lambda b, h, i: (b * nq + i, branch * 4 + h)),
        out_shape=jax.ShapeDtypeStruct(o.shape, o.dtype),
        input_output_aliases={5: 0},
        compiler_params=_params(3), name="mla_attention",
    )(qa, ka, ka, va, va, o)


def _diff_attn_kernel(q_ref, kc_ref, kl_ref, vc_ref, vl_ref, cosq_ref, saq_ref, sbq_ref, cos_ref, sa_ref, sb_ref,
                      lam_ref, g_ref, oin_ref, o_ref, kr_ref, vtc_ref, vtl_ref, s0_ref, s1_ref, *, lam_init, sub):
    del oin_ref
    quarter = DIFF_DK // 4
    s_refs = (s0_ref, s1_ref)

    @pl.when(pl.program_id(2) == 0)
    def _():
        kr_ref[...] = _rope(kl_ref[...].astype(F32), cos_ref[...], sa_ref[...], sb_ref[...], quarter).astype(BF16)
        _transpose_to(vtc_ref, vc_ref)
        _transpose_to(vtl_ref, vl_ref)

    scale = DIFF_DK ** -0.5 * LOG2E
    segments = [_ref_segment(kc_ref, vtc_ref), _ref_segment(kr_ref, vtl_ref)]
    lam = _diff_lambda(lam_ref, lam_init)

    def first_pass(unit, slot):
        r0, j = unit
        rows = slice(r0, r0 + sub)
        q = _rope(q_ref[rows, :].astype(F32) * scale, cosq_ref[rows, :], saq_ref[rows, :], sbq_ref[rows, :], quarter)
        lane = lax.broadcasted_iota(jnp.int32, q.shape, 1)
        qj = jnp.where((lane >= j * DIFF_DK) & (lane < (j + 1) * DIFF_DK), q, 0.0).astype(BF16)
        return _scores_pass(qj, segments, s_refs[slot])

    first_map = {}

    def second_pass(unit, slot, m):
        r0, j = unit
        o = _pv_pass(segments, s_refs[slot], m)
        if j == 0:
            first_map[r0] = o
        else:
            o_ref[r0:r0 + sub, :] = _diff_finish(first_map.pop(r0), o, lam, g_ref[...], lam_init).astype(BF16)

    _pipelined([(r0, j) for r0 in range(0, q_ref.shape[0], sub) for j in range(2)], first_pass, second_pass)


def _diff_attention(u, cols, rope64, diff_lambda_l, subln_g, lam_init, o, dims, branch):
    n_lat, seq, n_batch = dims
    n_ctx = (u.shape[0] - n_lat) // n_batch
    tq = _tile(seq, 1024, 16)
    sub = _tile(tq, DENSE_SUB, 16)
    nq = seq // tq
    ctx_blk0 = n_lat // n_ctx
    qb, kb, vb = cols["diff_q"] // LANES, cols["diff_k"] // LANES, cols["diff_v"] // LANES
    kern = functools.partial(_diff_attn_kernel, lam_init=lam_init, sub=sub)

    def tab_q():
        return pl.BlockSpec((tq, LANES), lambda b, h, i: (i, 0))

    def tab_k():
        return pl.BlockSpec((seq, LANES), lambda b, h, i: (0, 0))

    return pl.pallas_call(
        kern, grid=(n_batch, DIFF_H, nq),
        in_specs=[pl.BlockSpec((tq, LANES), lambda b, h, i: (b * nq + i, qb + h)),
                  pl.BlockSpec((n_ctx, LANES), lambda b, h, i: (ctx_blk0 + b, kb + h)),
                  pl.BlockSpec((seq, LANES), lambda b, h, i: (b, kb + h)),
                  pl.BlockSpec((n_ctx, LANES), lambda b, h, i: (ctx_blk0 + b, vb + h)),
                  pl.BlockSpec((seq, LANES), lambda b, h, i: (b, vb + h)),
                  tab_q(), tab_q(), tab_q(), tab_k(), tab_k(), tab_k(),
                  pl.BlockSpec((4, DIFF_DK), lambda b, h, i: (0, 0)),
                  pl.BlockSpec((1, DIFF_DV), lambda b, h, i: (0, 0)),
                  pl.BlockSpec(memory_space=pl.ANY)],
        out_specs=pl.BlockSpec((tq, LANES), lambda b, h, i: (b * nq + i, branch * 4 + h)),
        out_shape=jax.ShapeDtypeStruct(o.shape, o.dtype),
        scratch_shapes=[pltpu.VMEM((seq, LANES), BF16),
                        pltpu.VMEM((DIFF_DV, n_ctx), BF16), pltpu.VMEM((DIFF_DV, seq), BF16),
                        pltpu.VMEM((n_ctx + seq, sub), F32), pltpu.VMEM((n_ctx + seq, sub), F32)],
        input_output_aliases={13: 0},
        compiler_params=_params(3), name="diff_attention",
    )(u, u, u, u, u, *rope64, *rope64, diff_lambda_l, subln_g.reshape(1, -1), o)


def _local_attn_kernel(*refs, tq, nk, back, seq, sub, use_rope, use_sink):
    it = iter(refs)
    q_ref, kl_ref, vl_ref, kc_ref, vc_ref, bias_ref = (next(it) for _ in range(6))
    if use_rope:
        cosq_ref, saq_ref, sbq_ref, cos_ref, sa_ref, sb_ref = (next(it) for _ in range(6))
    if use_sink:
        sink_ref = next(it)
    oin_ref, o_ref = next(it), next(it)
    del oin_ref
    if use_rope:
        kr_ref = next(it)
    vtc_ref, vtw_ref, s_ref = next(it), next(it), next(it)
    i = pl.program_id(2)
    scale = LANES ** -0.5 * LOG2E
    quarter = SWA_HD // 4

    @pl.when(i == 0)
    def _():
        _transpose_to(vtc_ref, vc_ref)
        if use_rope:
            kr_ref[...] = _rope(kl_ref[...].astype(F32), cos_ref[...], sa_ref[...], sb_ref[...],
                                quarter).astype(BF16)

    k_src = kr_ref if use_rope else kl_ref
    ks = pl.multiple_of(jnp.clip(i * tq - back, 0, seq - nk), LANES)
    vtw_ref[...] = vl_ref[pl.ds(ks, nk), :].astype(F32).T.astype(BF16)
    sink = sink_ref[pl.program_id(1)] * LOG2E if use_sink else None
    def segments(r0):
        window = (nk, lambda c0, w: k_src[pl.ds(ks + c0, w), :], lambda c0, w: vtw_ref[:, c0:c0 + w],
                  lambda c0, w: bias_ref[c0:c0 + w, r0:r0 + sub])
        return [_ref_segment(kc_ref, vtc_ref), window]

    maxima = []
    for t, r0 in enumerate(range(0, tq, sub)):
        q = q_ref[r0:r0 + sub, :].astype(F32) * scale
        if use_rope:
            q = _rope(q, cosq_ref[r0:r0 + sub, :], saq_ref[r0:r0 + sub, :], sbq_ref[r0:r0 + sub, :], quarter)
        maxima.append(_scores_pass(q.astype(BF16), segments(r0), s_ref.at[t], sink=sink))
    for t, r0 in enumerate(range(0, tq, sub)):
        o_ref[r0:r0 + sub, :] = _pv_pass(segments(r0), s_ref.at[t], maxima[t], sink=sink).astype(BF16)


def _local_attention(u, qcol, kcol, vcol, n_heads, n_kv, bias_plan, rope, sink, o, dims, branch, name):
    bias, tile_map, nk, back = bias_plan

    def bias_tile(i):
        idx = tile_map[-1]
        for t in range(len(tile_map) - 2, -1, -1):
            idx = jnp.where(i == t, tile_map[t], idx)
        return idx

    n_lat, seq, n_batch = dims
    n_ctx = (u.shape[0] - n_lat) // n_batch
    tq = bias.shape[3]
    sub = _tile(tq, DENSE_SUB, LANES)
    nq = seq // tq
    grp = n_heads // n_kv
    ctx_blk0 = n_lat // n_ctx
    qb, kb, vb = qcol // LANES, kcol // LANES, vcol // LANES
    per_head_bias = bias.shape[0] > 1
    use_rope, use_sink = rope is not None, sink is not None
    kern = functools.partial(_local_attn_kernel, tq=tq, nk=nk, back=back, seq=seq, sub=sub,
                             use_rope=use_rope, use_sink=use_sink)
    in_specs = [pl.BlockSpec((tq, LANES), lambda b, h, i: (b * nq + i, qb + h)),
                pl.BlockSpec((seq, LANES), lambda b, h, i: (b, kb + h // grp)),
                pl.BlockSpec((seq, LANES), lambda b, h, i: (b, vb + h // grp)),
                pl.BlockSpec((n_ctx, LANES), lambda b, h, i: (ctx_blk0 + b, kb + h // grp)),
                pl.BlockSpec((n_ctx, LANES), lambda b, h, i: (ctx_blk0 + b, vb + h // grp)),
                pl.BlockSpec((None, None, nk, tq),
                             lambda b, h, i: (h if per_head_bias else 0, bias_tile(i), 0, 0))]
    args = [u, u, u, u, u, bias]
    scratch = []
    if use_rope:
        in_specs += [pl.BlockSpec((tq, LANES), lambda b, h, i: (i, 0))] * 3
        in_specs += [pl.BlockSpec((seq, LANES), lambda b, h, i: (0, 0))] * 3
        args += [*rope, *rope]
        scratch.append(pltpu.VMEM((seq, LANES), BF16))
    scratch += [pltpu.VMEM((LANES, n_ctx), BF16), pltpu.VMEM((LANES, nk), BF16),
                pltpu.VMEM((tq // sub, n_ctx + nk, sub), F32)]
    if use_sink:
        in_specs.append(pl.BlockSpec(memory_space=pltpu.SMEM))
        args.append(sink)
    in_specs.append(pl.BlockSpec(memory_space=pl.ANY))
    args.append(o)
    return pl.pallas_call(
        kern, grid=(n_batch, n_heads, nq), in_specs=in_specs,
        out_specs=pl.BlockSpec((tq, LANES), lambda b, h, i: (b * nq + i, branch * 4 + h)),
        out_shape=jax.ShapeDtypeStruct(o.shape, o.dtype),
        scratch_shapes=scratch,
        input_output_aliases={len(args) - 1: 0},
        compiler_params=_params(3), name=name,
    )(*args)


def _local_tiling(seq, span):
    tq = _tile(seq, 512, 64)
    nk = min(seq, tq + 2 * span)
    return tq, nk


def _dedupe(keys):
    first, uniq, tile_map = {}, [], []
    for i, key in enumerate(keys):
        if key not in first:
            first[key] = len(uniq)
            uniq.append(i)
        tile_map.append(first[key])
    return uniq, tuple(tile_map)


def _swa_bias(seq):
    tq, nk = _local_tiling(seq, SWA_WINDOW)
    nq = seq // tq
    tiles = []
    for i in range(nq):
        ks = min(max(i * tq - SWA_WINDOW, 0), seq - nk)
        qp = i * tq + np.arange(tq)[:, None]
        kp = ks + np.arange(nk)[None, :]
        tiles.append(np.where(np.abs(qp - kp) <= SWA_WINDOW, 0.0, NEG_INF).astype(np.float32).T.copy())
    uniq, tile_map = _dedupe([t.tobytes() for t in tiles])
    return jnp.asarray(np.stack([tiles[i] for i in uniq])[None], F32), tile_map, nk, SWA_WINDOW


def _na_bias_kernel(rpb_ref, out_ref, toe_ref, *, row_offsets):
    n_d, n_c = 2 * NA_KH - 1, 2 * NA_KW - 1
    hd = pl.program_id(0)
    shape = (GRID_W, 2 * GRID_W)
    kc = lax.broadcasted_iota(jnp.int32, shape, 0)
    lane = lax.broadcasted_iota(jnp.int32, shape, 1)
    qc = lane & (GRID_W - 1)
    col_start = jnp.clip(qc - NA_KW // 2, 0, GRID_W - NA_KW)
    valid = (kc >= col_start) & (kc < col_start + NA_KW)
    dc = jnp.clip(kc - qc, -(NA_KW - 1), NA_KW - 1) + (NA_KW - 1)
    for d in range(n_d):
        blk = jnp.zeros(shape, F32)
        for j in range(n_c):
            blk = jnp.where(dc == j, rpb_ref[(hd * n_d + d) * n_c + j] * LOG2E, blk)
        toe_ref[d] = jnp.where(valid, blk, NEG_INF)
    toe_ref[n_d] = jnp.full(shape, NEG_INF, F32)
    left = lane < GRID_W
    for u, offs in enumerate(row_offsets):
        for a in range(0, len(offs), 2):
            for b in range(len(offs[a])):
                out_ref[u, b * GRID_W:(b + 1) * GRID_W, a * GRID_W:(a + 2) * GRID_W] = jnp.where(
                    left, toe_ref[offs[a][b]], toe_ref[offs[a + 1][b]])


def _na_bias(rpb, seq):
    rows = seq // GRID_W
    kh = min(NA_KH, rows)
    back_rows = kh // 2
    tq, nk = _local_tiling(seq, back_rows * GRID_W)
    nq, qr_n, kr_n = seq // tq, tq // GRID_W, nk // GRID_W
    assert qr_n % 2 == 0
    n_heads = rpb.shape[0]
    n_d = 2 * NA_KH - 1
    d_idx = []
    for i in range(nq):
        ks_row = min(max(i * qr_n - back_rows, 0), rows - kr_n)
        qr = i * qr_n + np.arange(qr_n)[:, None]
        kr = ks_row + np.arange(kr_n)[None, :]
        start = np.clip(qr - kh // 2, 0, rows - kh)
        ok = (kr >= start) & (kr < start + kh)
        d_idx.append(np.where(ok, kr - qr + (NA_KH - 1), n_d))
    uniq, tile_map = _dedupe([d.tobytes() for d in d_idx])
    row_offsets = tuple(tuple(tuple(int(v) for v in row) for row in d_idx[i]) for i in uniq)
    bias = pl.pallas_call(
        functools.partial(_na_bias_kernel, row_offsets=row_offsets), grid=(n_heads,),
        in_specs=[pl.BlockSpec(memory_space=pltpu.SMEM)],
        out_specs=pl.BlockSpec((None, len(uniq), nk, tq), lambda hd: (hd, 0, 0, 0)),
        out_shape=jax.ShapeDtypeStruct((n_heads, len(uniq), nk, tq), F32),
        scratch_shapes=[pltpu.VMEM((n_d + 1, GRID_W, 2 * GRID_W), F32)],
        compiler_params=_params(1), name="na_bias",
    )(rpb.reshape(-1))
    return bias, tile_map, nk, back_rows * GRID_W


def _ctx_attn_kernel(u_ref, qa_ref, ka_ref, va_ref, sink_ref, lam_ref, g_ref, oin_ref, o_ref, *, cols, lam_init):
    del oin_ref

    def col(name, hd, width=LANES):
        c0 = cols[name] + hd * width
        return u_ref[:, c0:c0 + width]

    for hd in range(MLA_H):
        q = qa_ref[:, hd * 2 * LANES:(hd + 1) * 2 * LANES]
        k = ka_ref[:, hd * 2 * LANES:(hd + 1) * 2 * LANES]
        v = va_ref[:, hd * LANES:(hd + 1) * LANES]
        o_ref[:, hd * LANES:(hd + 1) * LANES] = _softmax_pv([_dot_nt(q, k)], [v], base2=True).astype(BF16)
    grp = SWA_H // SWA_KV_H
    scale = SWA_HD ** -0.5
    for hd in range(SWA_H):
        q = (col("swa_q", hd).astype(F32) * scale).astype(BF16)
        o = _softmax_pv([_dot_nt(q, col("swa_k", hd // grp))], [col("swa_v", hd // grp)], sink=sink_ref[hd])
        o_ref[:, BRANCH_W + hd * LANES:BRANCH_W + (hd + 1) * LANES] = o.astype(BF16)
    scale = NA_HD ** -0.5
    for hd in range(NA_H):
        q = (col("na_q", hd).astype(F32) * scale).astype(BF16)
        o = _softmax_pv([_dot_nt(q, col("na_k", hd))], [col("na_v", hd)])
        o_ref[:, 2 * BRANCH_W + hd * LANES:2 * BRANCH_W + (hd + 1) * LANES] = o.astype(BF16)
    scale = DIFF_DK ** -0.5
    lam = _diff_lambda(lam_ref, lam_init)
    for hd in range(DIFF_H):
        q = col("diff_q", hd).astype(F32) * scale
        k = col("diff_k", hd)
        v = col("diff_v", hd)
        lane = lax.broadcasted_iota(jnp.int32, q.shape, 1)
        outs = []
        for j in range(2):
            qj = jnp.where((lane >= j * DIFF_DK) & (lane < (j + 1) * DIFF_DK), q, 0.0).astype(BF16)
            outs.append(_softmax_pv([_dot_nt(qj, k)], [v]))
        o = _diff_finish(outs[0], outs[1], lam, g_ref[...], lam_init)
        o_ref[:, 3 * BRANCH_W + hd * LANES:3 * BRANCH_W + (hd + 1) * LANES] = o.astype(BF16)


def _ctx_attention(u, cols, qa, ka, va, sink, diff_lambda_l, subln_g, lam_init, o, dims):
    n_lat, seq, n_batch = dims
    n_ctx = (u.shape[0] - n_lat) // n_batch
    blk0 = n_lat // n_ctx
    kern = functools.partial(_ctx_attn_kernel, cols=cols, lam_init=lam_init)
    return pl.pallas_call(
        kern, grid=(n_batch,),
        in_specs=[pl.BlockSpec((n_ctx, u.shape[1]), lambda b: (blk0 + b, 0)),
                  pl.BlockSpec((n_ctx, qa.shape[1]), lambda b: (blk0 + b, 0)),
                  pl.BlockSpec((n_ctx, ka.shape[1]), lambda b: (blk0 + b, 0)),
                  pl.BlockSpec((n_ctx, va.shape[1]), lambda b: (blk0 + b, 0)),
                  pl.BlockSpec(memory_space=pltpu.SMEM),
                  pl.BlockSpec((4, DIFF_DK), lambda b: (0, 0)),
                  pl.BlockSpec((1, DIFF_DV), lambda b: (0, 0)),
                  pl.BlockSpec(memory_space=pl.ANY)],
        out_specs=pl.BlockSpec((n_ctx, o.shape[1]), lambda b: (blk0 + b, 0)),
        out_shape=jax.ShapeDtypeStruct(o.shape, o.dtype),
        input_output_aliases={7: 0},
        compiler_params=_params(1), name="ctx_attention",
    )(u, qa, ka, va, sink, diff_lambda_l, subln_g.reshape(1, -1), o)


MOE_TM = 512
GATHER_ROWS = 256


def _route_plan(top_i, tm):
    t = top_i.shape[0]
    n_assign = t * TOP_K
    n_tiles = (n_assign + N_EXPERTS * (tm - 1)) // tm
    e_flat = top_i.reshape(-1)
    onehot = (e_flat[:, None] == jnp.arange(N_EXPERTS)[None, :]).astype(jnp.int32)
    csum = jnp.cumsum(onehot, axis=0)
    rank = jnp.sum(onehot * csum, axis=1) - 1
    counts = csum[-1]
    padded = ((counts + tm - 1) // tm) * tm
    ends = jnp.cumsum(padded)
    offs = ends - padded
    pos = (jnp.sum(onehot * offs[None, :], axis=1) + rank).astype(jnp.int32)
    src = jnp.zeros((n_tiles * tm,), jnp.int32).at[pos].set(
        jnp.arange(n_assign, dtype=jnp.int32) // TOP_K, unique_indices=True, mode="promise_in_bounds")
    tile_start = jnp.arange(n_tiles, dtype=jnp.int32) * tm
    tile_expert = jnp.minimum(jnp.sum((tile_start[:, None] >= ends[None, :]).astype(jnp.int32), axis=1),
                              N_EXPERTS - 1).astype(jnp.int32)
    tile_valid = (tile_start < ends[-1]).astype(jnp.int32)
    return pos, src, tile_expert, tile_valid


def _row_copies(idx_ref, base, src_hbm, bufs, sems, slot, n_rows, idx_stride):
    def body(r, carry):
        for s, buf in enumerate(bufs):
            row = idx_ref[(base + r) * idx_stride + s]
            pltpu.make_async_copy(src_hbm.at[pl.ds(row, 1), :], buf.at[slot, pl.ds(r, 1), :],
                                  sems.at[slot, s]).start()
        return carry
    lax.fori_loop(0, n_rows, body, 0, unroll=8)


def _wait_rows(src_hbm, bufs, sems, slot, n_rows):
    for s, buf in enumerate(bufs):
        pltpu.make_async_copy(src_hbm.at[pl.ds(0, n_rows), :], buf.at[slot], sems.at[slot, s]).wait()


def _gather_kernel(src_ref, tv_ref, x_hbm, o_ref, x_ref, buf_ref, sem, *, rows):
    i = pl.program_id(0)

    @pl.when(i == 0)
    def _():
        cp = pltpu.make_async_copy(x_hbm, x_ref, sem)
        cp.start()
        cp.wait()

    @pl.when(tv_ref[i] == 1)
    def _():
        def body(r, carry):
            buf_ref[pl.ds(r, 1), :] = x_ref[pl.ds(src_ref[i * rows + r], 1), :]
            return carry
        lax.fori_loop(0, rows, body, 0, unroll=8)
        w = buf_ref[...]
        half = w.shape[1]
        o_ref[:, :half] = lax.bitcast_convert_type(w << 16, F32).astype(BF16)
        o_ref[:, half:] = lax.bitcast_convert_type(w & jnp.uint32(0xFFFF0000), F32).astype(BF16)

    @pl.when(tv_ref[i] == 0)
    def _():
        o_ref[...] = jnp.zeros(o_ref.shape, BF16)


def _gather_rows(xp, src, tile_valid, rows):
    t, half = xp.shape
    n_rows = src.shape[0]
    kern = functools.partial(_gather_kernel, rows=rows)
    return pl.pallas_call(
        kern,
        grid_spec=pltpu.PrefetchScalarGridSpec(
            num_scalar_prefetch=2, grid=(n_rows // rows,),
            in_specs=[pl.BlockSpec(memory_space=pl.ANY)],
            out_specs=pl.BlockSpec((rows, 2 * half), lambda i, s, v: (i, 0)),
            scratch_shapes=[pltpu.VMEM((t, half), jnp.uint32), pltpu.VMEM((rows, half), jnp.uint32),
                            pltpu.SemaphoreType.DMA(())]),
        out_shape=jax.ShapeDtypeStruct((n_rows, 2 * half), BF16),
        compiler_params=_params(1), name="moe_gather",
    )(src, tile_valid, xp)


def _expert_changed(te_ref, i):
    return jnp.logical_or(i == 0, te_ref[i] != te_ref[jnp.maximum(i - 1, 0)])


def _gmm_up_kernel(te_ref, tv_ref, x_ref, w1_ref, w3_ref, o_ref, w1b_ref, w3b_ref):
    i = pl.program_id(1)
    _cast_weights(_expert_changed(te_ref, i), [w1_ref, w3_ref], [w1b_ref, w3b_ref])

    @pl.when(tv_ref[i] == 1)
    def _():
        x = x_ref[...]
        o_ref[...] = (_silu(_dot(x, w1b_ref[...])) * _dot(x, w3b_ref[...])).astype(BF16)

    @pl.when(tv_ref[i] == 0)
    def _():
        o_ref[...] = jnp.zeros(o_ref.shape, BF16)


def _gmm_up(xs, w1, w3, idx, tile_expert, tile_valid, tm):
    r, d = xs.shape
    f = w1.shape[3]
    tn = _tile(f, 512, LANES)
    return pl.pallas_call(
        _gmm_up_kernel,
        grid_spec=pltpu.PrefetchScalarGridSpec(
            num_scalar_prefetch=2, grid=(f // tn, r // tm),
            in_specs=[pl.BlockSpec((tm, d), lambda j, i, te, tv: (i, 0)),
                      pl.BlockSpec((None, None, d, tn), lambda j, i, te, tv: (idx, te[i], 0, j)),
                      pl.BlockSpec((None, None, d, tn), lambda j, i, te, tv: (idx, te[i], 0, j))],
            out_specs=pl.BlockSpec((tm, tn), lambda j, i, te, tv: (i, j)),
            scratch_shapes=[pltpu.VMEM((d, tn), BF16), pltpu.VMEM((d, tn), BF16)]),
        out_shape=jax.ShapeDtypeStruct((r, f), BF16),
        compiler_params=_params(2), name="moe_up",
    )(tile_expert, tile_valid, xs, w1, w3)


def _gmm_down_kernel(te_ref, tv_ref, a_ref, w_ref, o_ref, wb_ref):
    i = pl.program_id(1)
    _cast_weights(_expert_changed(te_ref, i), [w_ref], [wb_ref])

    @pl.when(tv_ref[i] == 1)
    def _():
        o_ref[...] = _dot(a_ref[...], wb_ref[...])

    @pl.when(tv_ref[i] == 0)
    def _():
        o_ref[...] = jnp.zeros(o_ref.shape, F32)


def _gmm_down(a, w2, idx, tile_expert, tile_valid, tm):
    r, f = a.shape
    d = w2.shape[3]
    tn = _tile(d, 1024, LANES)
    return pl.pallas_call(
        _gmm_down_kernel,
        grid_spec=pltpu.PrefetchScalarGridSpec(
            num_scalar_prefetch=2, grid=(d // tn, r // tm),
            in_specs=[pl.BlockSpec((tm, f), lambda j, i, te, tv: (i, 0)),
                      pl.BlockSpec((None, None, f, tn), lambda j, i, te, tv: (idx, te[i], 0, j),
                                   pipeline_mode=pl.Buffered(1))],
            out_specs=pl.BlockSpec((tm, tn), lambda j, i, te, tv: (i, j)),
            scratch_shapes=[pltpu.VMEM((f, tn), BF16)]),
        out_shape=jax.ShapeDtypeStruct((r, d), F32),
        compiler_params=_params(2), name="moe_down",
    )(tile_expert, tile_valid, a, w2)


def _combine_kernel(pos_ref, y_hbm, h_ref, tg_ref, gate_ref, *rest, rows, n_lat, seq, n_batch, follow):
    if follow == "norm":
        g_ref, sh_ref, sc_ref, o_ref, xn_ref, buf0, buf1, sems = rest
    else:
        g_ref, o_ref, buf0, buf1, sems = rest
    i = pl.program_id(0)
    n = pl.num_programs(0)
    slot = i % 2
    bufs = [buf0, buf1]

    @pl.when(i == 0)
    def _():
        _row_copies(pos_ref, 0, y_hbm, bufs, sems, 0, rows, TOP_K)

    @pl.when(i + 1 < n)
    def _():
        _row_copies(pos_ref, (i + 1) * rows, y_hbm, bufs, sems, 1 - slot, rows, TOP_K)

    _wait_rows(y_hbm, bufs, sems, slot, rows)
    grp = _group_of_tile(i, rows, n_lat, seq, n_batch)
    gate = gate_ref[pl.ds(grp, 1), :]
    tg = tg_ref[...]
    mix = tg[:, 0:1] * buf0[slot] + tg[:, 1:2] * buf1[slot]
    h_new = h_ref[...] + gate * mix
    if follow == "norm":
        o_ref[...] = h_new
        xn = _rms(h_new, g_ref[...]) * (1.0 + sc_ref[pl.ds(grp, 1), :]) + sh_ref[pl.ds(grp, 1), :]
        xn_ref[...] = xn.astype(BF16)
    else:
        o_ref[...] = _rms(h_new, g_ref[...])


def _combine(h, y, pos, top_g, mod_l, gate_chunk, dims, n_rows, follow, g, mod_next=None):
    n_lat, seq, n_batch = dims
    d = h.shape[1]
    rows = _tile(math.gcd(seq, h.shape[0] - n_lat), GATHER_ROWS, 8)
    kern = functools.partial(_combine_kernel, rows=rows, n_lat=n_lat, seq=seq, n_batch=n_batch, follow=follow)
    row = lambda i, p: (i, 0)
    in_specs = [pl.BlockSpec(memory_space=pl.ANY),
                pl.BlockSpec((rows, d), row),
                pl.BlockSpec((rows, LANES), row),
                pl.BlockSpec((8, d), lambda i, p: (0, gate_chunk)),
                pl.BlockSpec((1, d), lambda i, p: (0, 0))]
    args = [pos, y, h, top_g, mod_l, g.reshape(1, d)]
    if follow == "norm":
        in_specs += [pl.BlockSpec((8, d), lambda i, p: (0, 0)), pl.BlockSpec((8, d), lambda i, p: (0, 1))]
        args += [mod_next, mod_next]
        out_specs = [pl.BlockSpec((rows, d), row), pl.BlockSpec((rows, d), row)]
        out_shape = [jax.ShapeDtypeStruct(h.shape, F32), jax.ShapeDtypeStruct((n_rows, d), BF16)]
        aliases = {2: 0}
    else:
        out_specs = pl.BlockSpec((rows, d), row)
        out_shape = jax.ShapeDtypeStruct((n_rows, d), F32)
        aliases = {}
    return pl.pallas_call(
        kern,
        grid_spec=pltpu.PrefetchScalarGridSpec(
            num_scalar_prefetch=1, grid=(n_rows // rows,), in_specs=in_specs, out_specs=out_specs,
            scratch_shapes=[pltpu.VMEM((2, rows, d), F32), pltpu.VMEM((2, rows, d), F32),
                            pltpu.SemaphoreType.DMA((2, 2))]),
        out_shape=out_shape,
        input_output_aliases=aliases,
        compiler_params=_params(1), name="moe_combine",
    )(*args)


def kernel(x, c, ctx, c_ctx, mod_w, mod_b, norm1_g, norm2_g, w_in, mla_q_norm, mla_kv_norm, mla_w_uq, mla_w_ukv,
           swa_sink, na_rpb, diff_lambda, diff_subln_g, w_branch, w_out, ffn_w1, ffn_w3, ffn_w2, moe_router,
           moe_w1, moe_w3, moe_w2, final_norm_g):
    n_batch, seq, d = x.shape
    n_ctx = ctx.shape[1]
    depth = mod_w.shape[0]
    n_lat = n_batch * seq
    dims = (n_lat, seq, n_batch)

    t = n_lat + n_batch * n_ctx
    h, xn, out = None, None, None
    c8 = jnp.concatenate([c, c_ctx[None, :], jnp.zeros((8 - n_batch - 1, d), F32)], axis=0)
    mod = _mod_all(c8, mod_w, mod_b)

    cols, width_u, in_runs = _in_layout(d)
    w_in_p = _pad_w_in(w_in, width_u, in_runs)
    w_uq = mla_w_uq.reshape(depth, -1, MLA_H, MLA_NOPE + MLA_ROPE)
    w_uq_p = jnp.pad(w_uq, ((0, 0), (0, 0), (0, 0), (0, 2 * LANES - MLA_NOPE - MLA_ROPE)))
    w_uq_p = w_uq_p.reshape(depth, -1, MLA_H * 2 * LANES)
    rope64 = _rope_tables(seq, MLA_ROPE)
    rope128 = _rope_tables(seq, SWA_HD)
    swa_plan = _swa_bias(seq)
    router_t = jnp.swapaxes(moe_router, 1, 2)

    for l in range(depth):
        need_ctx = l < depth - 1
        lam_init = 0.8 - 0.6 * math.exp(-0.3 * l)
        mod_l = mod[l]

        n_rows = t if need_ctx else n_lat
        if h is None:
            h, xn = _join_norm_mod(x.reshape(n_lat, d), ctx.reshape(n_batch * n_ctx, d), norm1_g[l], mod_l, 0, 1, dims)
        elif xn is None:
            xn = _norm_mod(h, norm1_g[l], mod_l, 0, 1, dims)
        u = _project_in(xn, w_in_p, l)

        qa, ka, va = _mla_prep(u, cols, mla_q_norm[l], mla_kv_norm[l], w_uq_p[l], mla_w_ukv[l], rope64, dims)
        o = jnp.zeros((t, N_BRANCH * BRANCH_W), BF16)
        o = _mla_attention(qa, ka, va, o, dims, 0)
        o = _local_attention(u, cols["swa_q"], cols["swa_k"], cols["swa_v"], SWA_H, SWA_KV_H, swa_plan,
                             rope128, swa_sink[l], o, dims, 1, "swa_attention")
        o = _local_attention(u, cols["na_q"], cols["na_k"], cols["na_v"], NA_H, NA_H, _na_bias(na_rpb[l], seq),
                             None, None, o, dims, 2, "na_attention")
        o = _diff_attention(u, cols, rope64, diff_lambda[l], diff_subln_g[l], lam_init, o, dims, 3)
        if need_ctx:
            o = _ctx_attention(u, cols, qa, ka, va, swa_sink[l], diff_lambda[l], diff_subln_g[l], lam_init, o, dims)

        merged = _merge(o, u, cols["gates"], w_branch, l, n_rows)

        i = l // 2
        xn = None
        if l % 2 == 0:
            h, xn2 = _project_out_norm(merged, w_out, l, h, mod_l, norm2_g[l], dims, n_rows)
            a = _swiglu_up(xn2, ffn_w1, ffn_w3, i)
            h = _project_residual(a, ffn_w2, i, h, mod_l, 5, dims)
        else:
            h, xp, top_g, top_i = _project_out_norm(merged, w_out, l, h, mod_l, norm2_g[l], dims, n_rows,
                                                    router=router_t[i])
            pos, src, tile_expert, tile_valid = _route_plan(top_i[:, :TOP_K], MOE_TM)
            xs = _gather_rows(xp, src, tile_valid, MOE_TM)
            a = _gmm_up(xs, moe_w1, moe_w3, i, tile_expert, tile_valid, MOE_TM)
            y = _gmm_down(a, moe_w2, i, tile_expert, tile_valid, MOE_TM)
            if need_ctx:
                h, xn = _combine(h, y, pos, top_g, mod_l, 5, dims, n_rows, "norm", norm1_g[l + 1], mod[l + 1])
            else:
                out = _combine(h, y, pos, top_g, mod_l, 5, dims, n_rows, "final", final_norm_g)

    if out is None:
        out = _final_norm(h, final_norm_g, n_lat)
    return out.reshape(n_batch, seq, d)
```

```python
import functools
import math

import jax
import jax.numpy as jnp
import numpy as np
from jax import lax
from jax.experimental import pallas as pl
from jax.experimental.pallas import tpu as pltpu

F32 = jnp.float32
BF16 = jnp.bfloat16

GRID_W = 64
ROPE_THETA = 10000.0
NORM_EPS = 1e-6
NEG_INF = -1e30
LOG2E = math.log2(math.e)

MLA_H, MLA_NOPE, MLA_ROPE, MLA_V = 4, 128, 64, 128
SWA_H, SWA_KV_H, SWA_HD, SWA_WINDOW = 4, 2, 128, 128
NA_H, NA_HD, NA_KH, NA_KW = 4, 128, 8, 16
DIFF_H, DIFF_DK, DIFF_DV = 4, 64, 128
N_BRANCH, BRANCH_W = 4, 512
N_EXPERTS, TOP_K = 8, 2
LANES = 128

VMEM_LIMIT = 56 * 1024 * 1024


def _params(n_axes):
    return pltpu.CompilerParams(dimension_semantics=("arbitrary",) * n_axes, vmem_limit_bytes=VMEM_LIMIT)


def _tile(n, pref, mult):
    if n <= pref:
        return n
    t = (pref // mult) * mult
    while t > mult and n % t:
        t -= mult
    assert n % t == 0, (n, pref, mult)
    return t


def _dot(a, b):
    return jnp.dot(a, b, preferred_element_type=F32)


def _dot_nt(a, b):
    return lax.dot_general(a, b, (((1,), (1,)), ((), ())), preferred_element_type=F32)


def _silu(x):
    return x * (1.0 / (1.0 + jnp.exp(-x)))


def _sigmoid(x):
    return 0.5 * (1.0 + jnp.tanh(0.5 * x))


def _mod_kernel(c_ref, w_ref, b_ref, o_ref):
    a = _silu(c_ref[...]).astype(BF16)
    o_ref[...] = _dot(a, w_ref[...].astype(BF16)) + b_ref[...]


def _mod_all(c8, mod_w, mod_b):
    n_l, d, n = mod_w.shape
    tn = _tile(n, 1024, LANES)
    return pl.pallas_call(
        _mod_kernel,
        grid=(n_l, n // tn),
        in_specs=[
            pl.BlockSpec((8, d), lambda l, j: (0, 0)),
            pl.BlockSpec((None, d, tn), lambda l, j: (l, 0, j)),
            pl.BlockSpec((None, 1, tn), lambda l, j: (l, 0, j)),
        ],
        out_specs=pl.BlockSpec((None, 8, tn), lambda l, j: (l, 0, j)),
        out_shape=jax.ShapeDtypeStruct((n_l, 8, n), F32),
        compiler_params=_params(2),
        name="mod_all",
    )(c8, mod_w, mod_b.reshape(n_l, 1, n))


def _group_of_tile(i, tm, n_lat, seq, n_batch):
    row0 = i * tm
    return jnp.where(row0 < n_lat, row0 // seq, n_batch)


def _rms(x, g):
    return x * lax.rsqrt(jnp.mean(x * x, axis=-1, keepdims=True) + NORM_EPS) * g


def _pack_bf16_pairs(xn):
    bits = lax.bitcast_convert_type(xn.astype(BF16).astype(F32), jnp.uint32)
    half = bits.shape[1] // 2
    return (bits[:, half:] & jnp.uint32(0xFFFF0000)) | (bits[:, :half] >> 16)


def _route_top2(xn, router_t):
    lane = lax.broadcasted_iota(jnp.int32, (xn.shape[0], LANES), 1).astype(F32)
    logits = jnp.full((xn.shape[0], LANES), -jnp.inf, F32)
    for e in range(N_EXPERTS):
        logit_e = jnp.sum(xn * router_t[e:e + 1, :], axis=-1, keepdims=True)
        logits = jnp.where(lane == float(e), logit_e, logits)
    m1 = jnp.max(logits, axis=-1, keepdims=True)
    i1 = jnp.min(jnp.where(logits == m1, lane, float(LANES)), axis=-1, keepdims=True)
    rest_l = jnp.where(lane == i1, -jnp.inf, logits)
    m2 = jnp.max(rest_l, axis=-1, keepdims=True)
    i2 = jnp.min(jnp.where(rest_l == m2, lane, float(LANES)), axis=-1, keepdims=True)
    e2 = jnp.exp(m2 - m1)
    den = 1.0 + e2
    top_g = jnp.where(lane == 0.0, 1.0 / den, jnp.where(lane == 1.0, e2 / den, 0.0))
    top_i = jnp.where(lane == 0.0, i1, jnp.where(lane == 1.0, i2, 0.0)).astype(jnp.int32)
    return top_g, top_i


def _norm_kernel(h_ref, g_ref, sh_ref, sc_ref, o_ref, *, tm, n_lat, seq, n_batch):
    grp = _group_of_tile(pl.program_id(0), tm, n_lat, seq, n_batch)
    xn = _rms(h_ref[...], g_ref[...]) * (1.0 + sc_ref[pl.ds(grp, 1), :]) + sh_ref[pl.ds(grp, 1), :]
    o_ref[...] = xn.astype(BF16)


def _norm_mod(h, g, mod_l, shift_chunk, scale_chunk, dims):
    n_lat, seq, n_batch = dims
    t, d = h.shape
    tm = _tile(math.gcd(seq, t - n_lat), 512, 8)
    kern = functools.partial(_norm_kernel, tm=tm, n_lat=n_lat, seq=seq, n_batch=n_batch)
    return pl.pallas_call(
        kern, grid=(t // tm,),
        in_specs=[pl.BlockSpec((tm, d), lambda i: (i, 0)),
                  pl.BlockSpec((1, d), lambda i: (0, 0)),
                  pl.BlockSpec((8, d), lambda i: (0, shift_chunk)),
                  pl.BlockSpec((8, d), lambda i: (0, scale_chunk))],
        out_specs=pl.BlockSpec((tm, d), lambda i: (i, 0)),
        out_shape=jax.ShapeDtypeStruct((t, d), BF16),
        compiler_params=_params(1), name="norm_mod",
    )(h, g.reshape(1, d), mod_l, mod_l)


def _join_norm_kernel(x_ref, c_ref, g_ref, sh_ref, sc_ref, h_ref, o_ref, *, tm, n_lat, seq, n_batch):
    i = pl.program_id(0)
    grp = _group_of_tile(i, tm, n_lat, seq, n_batch)
    x = jnp.where(i * tm < n_lat, x_ref[...], c_ref[...])
    h_ref[...] = x
    y = x * lax.rsqrt(jnp.mean(x * x, axis=-1, keepdims=True) + NORM_EPS) * g_ref[...]
    o_ref[...] = (y * (1.0 + sc_ref[pl.ds(grp, 1), :]) + sh_ref[pl.ds(grp, 1), :]).astype(BF16)


def _join_norm_mod(x_lat, x_ctx, g, mod_l, shift_chunk, scale_chunk, dims):
    n_lat, seq, n_batch = dims
    d = x_lat.shape[1]
    t = n_lat + x_ctx.shape[0]
    tm = _tile(math.gcd(seq, t - n_lat), 512, 8)
    lat_tiles = n_lat // tm
    kern = functools.partial(_join_norm_kernel, tm=tm, n_lat=n_lat, seq=seq, n_batch=n_batch)
    return pl.pallas_call(
        kern, grid=(t // tm,),
        in_specs=[pl.BlockSpec((tm, d), lambda i: (jnp.minimum(i, lat_tiles - 1), 0)),
                  pl.BlockSpec((tm, d), lambda i: (jnp.maximum(i - lat_tiles, 0), 0)),
                  pl.BlockSpec((1, d), lambda i: (0, 0)),
                  pl.BlockSpec((8, d), lambda i: (0, shift_chunk)),
                  pl.BlockSpec((8, d), lambda i: (0, scale_chunk))],
        out_specs=[pl.BlockSpec((tm, d), lambda i: (i, 0)), pl.BlockSpec((tm, d), lambda i: (i, 0))],
        out_shape=[jax.ShapeDtypeStruct((t, d), F32), jax.ShapeDtypeStruct((t, d), BF16)],
        compiler_params=_params(1), name="join_norm_mod",
    )(x_lat, x_ctx, g.reshape(1, d), mod_l, mod_l)


def _final_norm_kernel(h_ref, g_ref, o_ref):
    x = h_ref[...]
    o_ref[...] = x * lax.rsqrt(jnp.mean(x * x, axis=-1, keepdims=True) + NORM_EPS) * g_ref[...]


def _final_norm(h, g, n_rows):
    d = h.shape[1]
    tm = _tile(n_rows, 256, 8)
    return pl.pallas_call(
        _final_norm_kernel, grid=(n_rows // tm,),
        in_specs=[pl.BlockSpec((tm, d), lambda i: (i, 0)), pl.BlockSpec((1, d), lambda i: (0, 0))],
        out_specs=pl.BlockSpec((tm, d), lambda i: (i, 0)),
        out_shape=jax.ShapeDtypeStruct((n_rows, d), F32),
        compiler_params=_params(1), name="final_norm",
    )(h, g.reshape(1, d))


def _cast_weights(first, w_refs, wb_refs):
    @pl.when(first)
    def _():
        for w_ref, wb_ref in zip(w_refs, wb_refs):
            wb_ref[...] = w_ref[...].astype(BF16)


IN_TN = 1024
ROW_TILE = 1024


def _in_layout(d_model):
    pieces = [("cq", 512, 512), ("ckv", 256, 256), ("krope", MLA_ROPE, LANES),
              ("swa_q", SWA_H * SWA_HD, LANES), ("swa_k", SWA_KV_H * SWA_HD, LANES),
              ("swa_v", SWA_KV_H * SWA_HD, LANES), ("na_q", NA_H * NA_HD, LANES), ("na_k", NA_H * NA_HD, LANES),
              ("na_v", NA_H * NA_HD, LANES), ("diff_q", DIFF_H * 2 * DIFF_DK, LANES),
              ("diff_k", DIFF_H * 2 * DIFF_DK, LANES), ("diff_v", DIFF_H * DIFF_DV, LANES),
              ("gates", N_BRANCH * d_model, 512)]
    natural, nat = {}, 0
    for name, width, _ in pieces:
        natural[name] = nat
        nat += width
    cols, runs, u = {}, [], 0
    for name, width, align in pieces[-1:] + pieces[:-1]:
        u = -(-u // align) * align
        cols[name] = u
        runs.append((u, natural[name], width))
        u += width
    width_u = -(-u // IN_TN) * IN_TN
    return cols, width_u, tuple(runs)


def _pad_w_in(w_in, width_u, runs):
    assert runs[0][0] == 0
    ends = [r[0] for r in runs[1:]] + [width_u]
    parts = []
    for (u0, n0, w), end in zip(runs, ends):
        piece = w_in[:, :, n0:n0 + w].astype(BF16)
        if end > u0 + w:
            piece = jnp.pad(piece, ((0, 0), (0, 0), (0, end - u0 - w)))
        parts.append(piece)
    return jnp.concatenate(parts, axis=2)


def _proj_kernel(x_ref, w_ref, o_ref):
    o_ref[...] = _dot(x_ref[...], w_ref[...]).astype(o_ref.dtype)


def _project_in(xn, w_p, layer):
    t, d = xn.shape
    n = w_p.shape[2]
    tm = min(ROW_TILE, t)
    return pl.pallas_call(
        _proj_kernel, grid=(n // IN_TN, pl.cdiv(t, tm)),
        in_specs=[pl.BlockSpec((tm, d), lambda j, i: (i, 0)),
                  pl.BlockSpec((None, d, IN_TN), lambda j, i: (layer, 0, j))],
        out_specs=pl.BlockSpec((tm, IN_TN), lambda j, i: (i, j)),
        out_shape=jax.ShapeDtypeStruct((t, n), BF16),
        compiler_params=_params(2), name="project_in",
    )(xn, w_p)


def _proj_res_kernel(x_ref, w_ref, h_ref, gate_ref, o_ref, wb_ref, *, tm, n_lat, seq, n_batch):
    i = pl.program_id(1)
    _cast_weights(i == 0, [w_ref], [wb_ref])
    grp = _group_of_tile(i, tm, n_lat, seq, n_batch)
    gate = gate_ref[pl.ds(grp, 1), :]
    o_ref[...] = h_ref[...] + gate * _dot(x_ref[...], wb_ref[...])


def _project_residual(x, w_l, layer, h, mod_l, gate_chunk, dims):
    n_lat, seq, n_batch = dims
    t, k = x.shape
    n = w_l.shape[2]
    tm = _tile(math.gcd(seq, t - n_lat), 256, 8)
    tn = _tile(n, 1024, LANES)
    kern = functools.partial(_proj_res_kernel, tm=tm, n_lat=n_lat, seq=seq, n_batch=n_batch)
    gate_blk = gate_chunk * (n // tn)
    return pl.pallas_call(
        kern, grid=(n // tn, t // tm),
        in_specs=[pl.BlockSpec((tm, k), lambda j, i: (i, 0)),
                  pl.BlockSpec((None, k, tn), lambda j, i: (layer, 0, j), pipeline_mode=pl.Buffered(1)),
                  pl.BlockSpec((tm, tn), lambda j, i: (i, j)),
                  pl.BlockSpec((8, tn), lambda j, i: (0, gate_blk + j))],
        out_specs=pl.BlockSpec((tm, tn), lambda j, i: (i, j)),
        out_shape=jax.ShapeDtypeStruct(h.shape, F32),
        scratch_shapes=[pltpu.VMEM((k, tn), BF16)],
        input_output_aliases={2: 0},
        compiler_params=_params(2), name="project_residual",
    )(x, w_l, h, mod_l)


def _out_norm_kernel(x_ref, w_ref, h_ref, mod_ref, g_ref, *rest, tm, n_lat, seq, n_batch, chunks, route):
    d = h_ref.shape[1]
    if route:
        r_ref, o_ref, xn_ref, tg_ref, ti_ref, wb_ref = rest
    else:
        o_ref, xn_ref, wb_ref = rest
    i = pl.program_id(0)
    _cast_weights(i == 0, [w_ref], [wb_ref])
    grp = _group_of_tile(i, tm, n_lat, seq, n_batch)
    gate, shift, scale = (mod_ref[pl.ds(grp, 1), c * d:(c + 1) * d] for c in chunks)
    h_new = h_ref[...] + gate * _dot(x_ref[...], wb_ref[...])
    o_ref[...] = h_new
    xn = _rms(h_new, g_ref[...]) * (1.0 + scale) + shift
    if route:
        xn_ref[...] = _pack_bf16_pairs(xn)
        tg_ref[...], ti_ref[...] = _route_top2(xn, r_ref[...])
    else:
        xn_ref[...] = xn.astype(BF16)


def _project_out_norm(x, w_l, layer, h, mod_l, g, dims, n_rows, router=None):
    n_lat, seq, n_batch = dims
    k = x.shape[1]
    d = h.shape[1]
    assert w_l.shape[2] == d
    tm = _tile(math.gcd(seq, h.shape[0] - n_lat), 256, 8)
    route = router is not None
    kern = functools.partial(_out_norm_kernel, tm=tm, n_lat=n_lat, seq=seq, n_batch=n_batch,
                             chunks=(2, 3, 4), route=route)
    row = lambda i: (i, 0)
    fixed = lambda i: (0, 0)
    in_specs = [pl.BlockSpec((tm, k), row),
                pl.BlockSpec((None, k, d), lambda i: (layer, 0, 0), pipeline_mode=pl.Buffered(1)),
                pl.BlockSpec((tm, d), row),
                pl.BlockSpec(mod_l.shape, fixed),
                pl.BlockSpec((1, d), fixed)]
    args = [x, w_l, h, mod_l, g.reshape(1, d)]
    out_specs = [pl.BlockSpec((tm, d), row)]
    out_shape = [jax.ShapeDtypeStruct(h.shape, F32)]
    if route:
        in_specs.append(pl.BlockSpec((N_EXPERTS, d), fixed))
        args.append(router)
        out_specs += [pl.BlockSpec((tm, d // 2), row), pl.BlockSpec((tm, LANES), row), pl.BlockSpec((tm, LANES), row)]
        out_shape += [jax.ShapeDtypeStruct((n_rows, d // 2), jnp.uint32),
                      jax.ShapeDtypeStruct((n_rows, LANES), F32),
                      jax.ShapeDtypeStruct((n_rows, LANES), jnp.int32)]
    else:
        out_specs.append(pl.BlockSpec((tm, d), row))
        out_shape.append(jax.ShapeDtypeStruct((n_rows, d), BF16))
    return pl.pallas_call(
        kern, grid=(n_rows // tm,), in_specs=in_specs, out_specs=out_specs, out_shape=out_shape,
        scratch_shapes=[pltpu.VMEM((k, d), BF16)],
        input_output_aliases={2: 0},
        compiler_params=_params(1), name="project_out_norm",
    )(*args)


def _swiglu_up_kernel(x_ref, w1_ref, w3_ref, o_ref, w1b_ref, w3b_ref):
    _cast_weights(pl.program_id(1) == 0, [w1_ref, w3_ref], [w1b_ref, w3b_ref])
    x = x_ref[...]
    a = _dot(x, w1b_ref[...])
    b = _dot(x, w3b_ref[...])
    o_ref[...] = (_silu(a) * b).astype(BF16)


def _swiglu_up(xn, w1, w3, idx):
    t, d = xn.shape
    f = w1.shape[2]
    tm = min(ROW_TILE, t)
    tn = _tile(f, 512, LANES)
    return pl.pallas_call(
        _swiglu_up_kernel, grid=(f // tn, pl.cdiv(t, tm)),
        in_specs=[pl.BlockSpec((tm, d), lambda j, i: (i, 0)),
                  pl.BlockSpec((None, d, tn), lambda j, i: (idx, 0, j)),
                  pl.BlockSpec((None, d, tn), lambda j, i: (idx, 0, j))],
        out_specs=pl.BlockSpec((tm, tn), lambda j, i: (i, j)),
        out_shape=jax.ShapeDtypeStruct((t, f), BF16),
        scratch_shapes=[pltpu.VMEM((d, tn), BF16), pltpu.VMEM((d, tn), BF16)],
        compiler_params=_params(2), name="swiglu_up",
    )(xn, w1, w3)


def _merge_kernel(o_ref, g0_ref, g1_ref, g2_ref, g3_ref, w_ref, out_ref, wb_ref):
    _cast_weights(pl.program_id(0) == 0, [w_ref], [wb_ref])
    acc = None
    for m, g_ref in enumerate((g0_ref, g1_ref, g2_ref, g3_ref)):
        y = _sigmoid(g_ref[...]).astype(F32) * _dot(o_ref[:, m * BRANCH_W:(m + 1) * BRANCH_W], wb_ref[m])
        acc = y if acc is None else acc + y
    out_ref[...] = acc.astype(BF16)


def _merge(o, u, gate_col, w_branch, layer, n_rows):
    t = n_rows
    d = w_branch.shape[3]
    tm = _tile(math.gcd(t, o.shape[0]), 256, 8)
    assert gate_col % d == 0
    gate_blk = gate_col // d

    def gate_spec(m):
        return pl.BlockSpec((tm, d), lambda i: (i, gate_blk + m))

    return pl.pallas_call(
        _merge_kernel, grid=(t // tm,),
        in_specs=[pl.BlockSpec((tm, N_BRANCH * BRANCH_W), lambda i: (i, 0)),
                  gate_spec(0), gate_spec(1), gate_spec(2), gate_spec(3),
                  pl.BlockSpec((None, N_BRANCH, BRANCH_W, d), lambda i: (layer, 0, 0, 0),
                               pipeline_mode=pl.Buffered(1))],
        out_specs=pl.BlockSpec((tm, d), lambda i: (i, 0)),
        out_shape=jax.ShapeDtypeStruct((t, d), BF16),
        scratch_shapes=[pltpu.VMEM((N_BRANCH, BRANCH_W, d), BF16)],
        compiler_params=_params(1), name="merge",
    )(o, u, u, u, u, w_branch)


def _rope_tables(n, dim):
    pos = jnp.arange(n, dtype=jnp.int32)
    row = (pos // GRID_W).astype(F32)
    col = (pos % GRID_W).astype(F32)
    quarter = dim // 4
    inv_freq = ROPE_THETA ** (-jnp.arange(quarter, dtype=F32) / quarter)
    ang_r = row[:, None] * inv_freq
    ang_c = col[:, None] * inv_freq
    ang = jnp.concatenate([ang_r, ang_r, ang_c, ang_c], axis=-1)
    reps = LANES // dim
    cos = jnp.tile(jnp.cos(ang), (1, reps))
    sin = jnp.tile(jnp.sin(ang), (1, reps))
    lane = jnp.arange(LANES)[None, :]
    first = (lane & quarter) == 0
    return cos, jnp.where(first, -sin, 0.0), jnp.where(first, 0.0, sin)


def _rope(x, cos, sin_a, sin_b, quarter):
    return (x * cos + pltpu.roll(x, LANES - quarter, 1) * sin_a + pltpu.roll(x, quarter, 1) * sin_b)


def _mla_prep_kernel(cq_ref, ckv_ref, kr_ref, qg_ref, kvg_ref, wq_ref, wkv_ref, cos_ref, sa_ref, sb_ref,
                     q_ref, k_ref, v_ref, *, n_lat_tiles):
    is_lat = pl.program_id(0) < n_lat_tiles
    cos = jnp.where(is_lat, cos_ref[...], 1.0)
    sa = jnp.where(is_lat, sa_ref[...], 0.0)
    sb = jnp.where(is_lat, sb_ref[...], 0.0)
    quarter = MLA_ROPE // 4

    def rms(x_ref, g_ref):
        x = x_ref[...].astype(F32)
        return (x * lax.rsqrt(jnp.mean(x * x, axis=-1, keepdims=True) + NORM_EPS) * g_ref[...]).astype(BF16)

    scale = (MLA_NOPE + MLA_ROPE) ** -0.5 * LOG2E
    q = _dot(rms(cq_ref, qg_ref), wq_ref[...].astype(BF16)) * scale
    kv = _dot(rms(ckv_ref, kvg_ref), wkv_ref[...].astype(BF16))
    kr = _rope(kr_ref[...].astype(F32), cos, sa, sb, quarter).astype(BF16)
    for hd in range(MLA_H):
        b = hd * 2 * LANES
        q_ref[:, b:b + LANES] = q[:, b:b + LANES].astype(BF16)
        q_ref[:, b + LANES:b + 2 * LANES] = _rope(q[:, b + LANES:b + 2 * LANES], cos, sa, sb, quarter).astype(BF16)
        k_ref[:, b:b + LANES] = kv[:, b:b + LANES].astype(BF16)
        k_ref[:, b + LANES:b + 2 * LANES] = kr
        v_ref[:, hd * LANES:(hd + 1) * LANES] = kv[:, b + LANES:b + 2 * LANES].astype(BF16)


def _mla_prep(u, cols, q_norm, kv_norm, w_uq_p, w_ukv, rope64, dims):
    n_lat, seq, _ = dims
    t = u.shape[0]
    tm = _tile(math.gcd(seq, t - n_lat), 256, 16)
    n_lat_tiles = n_lat // tm
    seq_tiles = seq // tm
    qw = MLA_H * 2 * LANES

    def tab_spec():
        return pl.BlockSpec((tm, LANES), lambda i: (jnp.where(i < n_lat_tiles, i % seq_tiles, 0), 0))

    kern = functools.partial(_mla_prep_kernel, n_lat_tiles=n_lat_tiles)
    return pl.pallas_call(
        kern, grid=(t // tm,),
        in_specs=[pl.BlockSpec((tm, 512), lambda i: (i, cols["cq"] // 512)),
                  pl.BlockSpec((tm, 256), lambda i: (i, cols["ckv"] // 256)),
                  pl.BlockSpec((tm, LANES), lambda i: (i, cols["krope"] // LANES)),
                  pl.BlockSpec((1, 512), lambda i: (0, 0)),
                  pl.BlockSpec((1, 256), lambda i: (0, 0)),
                  pl.BlockSpec((512, qw), lambda i: (0, 0)),
                  pl.BlockSpec((256, qw), lambda i: (0, 0)),
                  tab_spec(), tab_spec(), tab_spec()],
        out_specs=[pl.BlockSpec((tm, qw), lambda i: (i, 0)),
                   pl.BlockSpec((tm, qw), lambda i: (i, 0)),
                   pl.BlockSpec((tm, MLA_H * MLA_V), lambda i: (i, 0))],
        out_shape=[jax.ShapeDtypeStruct((t, qw), BF16),
                   jax.ShapeDtypeStruct((t, qw), BF16),
                   jax.ShapeDtypeStruct((t, MLA_H * MLA_V), BF16)],
        compiler_params=_params(1), name="mla_prep",
    )(u, u, u, q_norm.reshape(1, -1), kv_norm.reshape(1, -1), w_uq_p, w_ukv, *rope64)


def _softmax_pv(scores, values, sink=None, base2=False):
    ex = jnp.exp2 if base2 else jnp.exp
    m = None
    for s in scores:
        mi = jnp.max(s, axis=-1, keepdims=True)
        m = mi if m is None else jnp.maximum(m, mi)
    if sink is not None:
        m = jnp.maximum(m, sink)
    den = None
    acc = None
    for s, v in zip(scores, values):
        p = ex(s - m)
        li = jnp.sum(p, axis=-1, keepdims=True)
        den = li if den is None else den + li
        o = _dot(p.astype(BF16), v)
        acc = o if acc is None else acc + o
    if sink is not None:
        den = den + ex(sink - m)
    return acc / den


def _diff_lambda(lam_ref, lam_init):
    lp = lam_ref[...]
    return (jnp.exp(jnp.sum(lp[0:1, :] * lp[1:2, :], axis=-1, keepdims=True))
            - jnp.exp(jnp.sum(lp[2:3, :] * lp[3:4, :], axis=-1, keepdims=True)) + lam_init)


def _diff_finish(o1, o2, lam, g, lam_init):
    o = o1 - lam * o2
    y = o * lax.rsqrt(jnp.mean(o * o, axis=-1, keepdims=True) + NORM_EPS) * g
    return y * (1.0 - lam_init)


DENSE_SUB = 256
DENSE_CHUNK = 256


def _transpose_to(dst_ref, src_ref):
    dst_ref[...] = src_ref[...].astype(F32).T.astype(BF16)


def _ref_segment(k_ref, vt_ref):
    return (k_ref.shape[0], lambda c0, w: k_ref[c0:c0 + w, :], lambda c0, w: vt_ref[:, c0:c0 + w], None)


def _scores_pass(q, segments, s_ref, sink=None):
    rows = q.shape[0]
    m_run, off = None, 0
    for n, keys, _, bias in segments:
        for c0 in range(0, n, DENSE_CHUNK):
            w = min(DENSE_CHUNK, n - c0)
            s = _dot_nt(keys(c0, w), q)
            if bias is not None:
                s = s + bias(c0, w)
            s_ref[off + c0:off + c0 + w, :] = s
            part = jnp.max(s.reshape(w // 8, 8, rows), axis=0)
            m_run = part if m_run is None else jnp.maximum(m_run, part)
        off += n
    m = jnp.max(m_run, axis=0, keepdims=True)
    return m if sink is None else jnp.maximum(m, sink)


def _pv_pass(segments, s_ref, m, sink=None):
    rows = m.shape[1]
    l_run, acc, off = None, None, 0
    for n, _, values_t, _ in segments:
        for c0 in range(0, n, DENSE_CHUNK):
            w = min(DENSE_CHUNK, n - c0)
            p = jnp.exp2(s_ref[off + c0:off + c0 + w, :] - m)
            part = jnp.sum(p.reshape(w // 8, 8, rows), axis=0)
            l_run = part if l_run is None else l_run + part
            pv = _dot(values_t(c0, w), p.astype(BF16))
            acc = pv if acc is None else acc + pv
        off += n
    den = jnp.sum(l_run, axis=0, keepdims=True)
    if sink is not None:
        den = den + jnp.exp2(sink - m)
    return (acc * (1.0 / den)).T


def _pipelined(units, first_pass, second_pass):
    state = first_pass(units[0], 0)
    for k, unit in enumerate(units):
        nxt = first_pass(units[k + 1], (k + 1) % 2) if k + 1 < len(units) else None
        second_pass(unit, k % 2, state)
        state = nxt


def _mla_attn_kernel(q_ref, kc_ref, kl_ref, vc_ref, vl_ref, oin_ref, o_ref, vtc_ref, vtl_ref, sa_ref, sb_ref, *, sub):
    del oin_ref
    s_refs = (sa_ref, sb_ref)

    @pl.when(pl.program_id(2) == 0)
    def _():
        _transpose_to(vtc_ref, vc_ref)
        _transpose_to(vtl_ref, vl_ref)

    segments = [_ref_segment(kc_ref, vtc_ref), _ref_segment(kl_ref, vtl_ref)]

    def first_pass(r0, slot):
        return _scores_pass(q_ref[r0:r0 + sub, :], segments, s_refs[slot])

    def second_pass(r0, slot, m):
        o_ref[r0:r0 + sub, :] = _pv_pass(segments, s_refs[slot], m).astype(BF16)

    _pipelined(list(range(0, q_ref.shape[0], sub)), first_pass, second_pass)


def _mla_attention(qa, ka, va, o, dims, branch):
    n_lat, seq, n_batch = dims
    n_ctx = (qa.shape[0] - n_lat) // n_batch
    tq = _tile(seq, 1024, 16)
    sub = _tile(tq, DENSE_SUB, 16)
    nq = seq // tq
    ctx_blk0 = n_lat // n_ctx
    kw = 2 * LANES
    return pl.pallas_call(
        functools.partial(_mla_attn_kernel, sub=sub), grid=(n_batch, MLA_H, nq),
        scratch_shapes=[pltpu.VMEM((MLA_V, n_ctx), BF16), pltpu.VMEM((MLA_V, seq), BF16),
                        pltpu.VMEM((n_ctx + seq, sub), F32), pltpu.VMEM((n_ctx + seq, sub), F32)],
        in_specs=[pl.BlockSpec((tq, kw), lambda b, h, i: (b * nq + i, h)),
                  pl.BlockSpec((n_ctx, kw), lambda b, h, i: (ctx_blk0 + b, h)),
                  pl.BlockSpec((seq, kw), lambda b, h, i: (b, h)),
                  pl.BlockSpec((n_ctx, LANES), lambda b, h, i: (ctx_blk0 + b, h)),
                  pl.BlockSpec((seq, LANES), lambda b, h, i: (b, h)),
                  pl.BlockSpec(memory_space=pl.ANY)],
        out_specs=pl.BlockSpec((tq, LANES), lambda b, h, i: (b * nq + i, branch * 4 + h)),
        out_shape=jax.ShapeDtypeStruct(o.shape, o.dtype),
        input_output_aliases={5: 0},
        compiler_params=_params(3), name="mla_attention",
    )(qa, ka, ka, va, va, o)


def _diff_attn_kernel(q_ref, kc_ref, kl_ref, vc_ref, vl_ref, cosq_ref, saq_ref, sbq_ref, cos_ref, sa_ref, sb_ref,
                      lam_ref, g_ref, oin_ref, o_ref, kr_ref, vtc_ref, vtl_ref, s0_ref, s1_ref, *, lam_init, sub):
    del oin_ref
    quarter = DIFF_DK // 4
    s_refs = (s0_ref, s1_ref)

    @pl.when(pl.program_id(2) == 0)
    def _():
        kr_ref[...] = _rope(kl_ref[...].astype(F32), cos_ref[...], sa_ref[...], sb_ref[...], quarter).astype(BF16)
        _transpose_to(vtc_ref, vc_ref)
        _transpose_to(vtl_ref, vl_ref)

    scale = DIFF_DK ** -0.5 * LOG2E
    segments = [_ref_segment(kc_ref, vtc_ref), _ref_segment(kr_ref, vtl_ref)]
    lam = _diff_lambda(lam_ref, lam_init)

    def first_pass(unit, slot):
        r0, j = unit
        rows = slice(r0, r0 + sub)
        q = _rope(q_ref[rows, :].astype(F32) * scale, cosq_ref[rows, :], saq_ref[rows, :], sbq_ref[rows, :], quarter)
        lane = lax.broadcasted_iota(jnp.int32, q.shape, 1)
        qj = jnp.where((lane >= j * DIFF_DK) & (lane < (j + 1) * DIFF_DK), q, 0.0).astype(BF16)
        return _scores_pass(qj, segments, s_refs[slot])

    first_map = {}

    def second_pass(unit, slot, m):
        r0, j = unit
        o = _pv_pass(segments, s_refs[slot], m)
        if j == 0:
            first_map[r0] = o
        else:
            o_ref[r0:r0 + sub, :] = _diff_finish(first_map.pop(r0), o, lam, g_ref[...], lam_init).astype(BF16)

    _pipelined([(r0, j) for r0 in range(0, q_ref.shape[0], sub) for j in range(2)], first_pass, second_pass)


def _diff_attention(u, cols, rope64, diff_lambda_l, subln_g, lam_init, o, dims, branch):
    n_lat, seq, n_batch = dims
    n_ctx = (u.shape[0] - n_lat) // n_batch
    tq = _tile(seq, 1024, 16)
    sub = _tile(tq, DENSE_SUB, 16)
    nq = seq // tq
    ctx_blk0 = n_lat // n_ctx
    qb, kb, vb = cols["diff_q"] // LANES, cols["diff_k"] // LANES, cols["diff_v"] // LANES
    kern = functools.partial(_diff_attn_kernel, lam_init=lam_init, sub=sub)

    def tab_q():
        return pl.BlockSpec((tq, LANES), lambda b, h, i: (i, 0))

    def tab_k():
        return pl.BlockSpec((seq, LANES), lambda b, h, i: (0, 0))

    return pl.pallas_call(
        kern, grid=(n_batch, DIFF_H, nq),
        in_specs=[pl.BlockSpec((tq, LANES), lambda b, h, i: (b * nq + i, qb + h)),
                  pl.BlockSpec((n_ctx, LANES), lambda b, h, i: (ctx_blk0 + b, kb + h)),
                  pl.BlockSpec((seq, LANES), lambda b, h, i: (b, kb + h)),
                  pl.BlockSpec((n_ctx, LANES), lambda b, h, i: (ctx_blk0 + b, vb + h)),
                  pl.BlockSpec((seq, LANES), lambda b, h, i: (b, vb + h)),
                  tab_q(), tab_q(), tab_q(), tab_k(), tab_k(), tab_k(),
                  pl.BlockSpec((4, DIFF_DK), lambda b, h, i: (0, 0)),
                  pl.BlockSpec((1, DIFF_DV), lambda b, h, i: (0, 0)),
                  pl.BlockSpec(memory_space=pl.ANY)],
        out_specs=pl.BlockSpec((tq, LANES), lambda b, h, i: (b * nq + i, branch * 4 + h)),
        out_shape=jax.ShapeDtypeStruct(o.shape, o.dtype),
        scratch_shapes=[pltpu.VMEM((seq, LANES), BF16),
                        pltpu.VMEM((DIFF_DV, n_ctx), BF16), pltpu.VMEM((DIFF_DV, seq), BF16),
                        pltpu.VMEM((n_ctx + seq, sub), F32), pltpu.VMEM((n_ctx + seq, sub), F32)],
        input_output_aliases={13: 0},
        compiler_params=_params(3), name="diff_attention",
    )(u, u, u, u, u, *rope64, *rope64, diff_lambda_l, subln_g.reshape(1, -1), o)


def _local_attn_kernel(*refs, tq, nk, back, seq, sub, use_rope, use_sink):
    it = iter(refs)
    q_ref, kl_ref, vl_ref, kc_ref, vc_ref, bias_ref = (next(it) for _ in range(6))
    if use_rope:
        cosq_ref, saq_ref, sbq_ref, cos_ref, sa_ref, sb_ref = (next(it) for _ in range(6))
    if use_sink:
        sink_ref = next(it)
    oin_ref, o_ref = next(it), next(it)
    del oin_ref
    if use_rope:
        kr_ref = next(it)
    vtc_ref, vtw_ref, s_ref = next(it), next(it), next(it)
    i = pl.program_id(2)
    scale = LANES ** -0.5 * LOG2E
    quarter = SWA_HD // 4

    @pl.when(i == 0)
    def _():
        _transpose_to(vtc_ref, vc_ref)
        if use_rope:
            kr_ref[...] = _rope(kl_ref[...].astype(F32), cos_ref[...], sa_ref[...], sb_ref[...],
                                quarter).astype(BF16)

    k_src = kr_ref if use_rope else kl_ref
    ks = pl.multiple_of(jnp.clip(i * tq - back, 0, seq - nk), LANES)
    vtw_ref[...] = vl_ref[pl.ds(ks, nk), :].astype(F32).T.astype(BF16)
    sink = sink_ref[pl.program_id(1)] * LOG2E if use_sink else None
    def segments(r0):
        window = (nk, lambda c0, w: k_src[pl.ds(ks + c0, w), :], lambda c0, w: vtw_ref[:, c0:c0 + w],
                  lambda c0, w: bias_ref[c0:c0 + w, r0:r0 + sub])
        return [_ref_segment(kc_ref, vtc_ref), window]

    maxima = []
    for t, r0 in enumerate(range(0, tq, sub)):
        q = q_ref[r0:r0 + sub, :].astype(F32) * scale
        if use_rope:
            q = _rope(q, cosq_ref[r0:r0 + sub, :], saq_ref[r0:r0 + sub, :], sbq_ref[r0:r0 + sub, :], quarter)
        maxima.append(_scores_pass(q.astype(BF16), segments(r0), s_ref.at[t], sink=sink))
    for t, r0 in enumerate(range(0, tq, sub)):
        o_ref[r0:r0 + sub, :] = _pv_pass(segments(r0), s_ref.at[t], maxima[t], sink=sink).astype(BF16)


def _local_attention(u, qcol, kcol, vcol, n_heads, n_kv, bias_plan, rope, sink, o, dims, branch, name):
    bias, tile_map, nk, back = bias_plan

    def bias_tile(i):
        idx = tile_map[-1]
        for t in range(len(tile_map) - 2, -1, -1):
            idx = jnp.where(i == t, tile_map[t], idx)
        return idx

    n_lat, seq, n_batch = dims
    n_ctx = (u.shape[0] - n_lat) // n_batch
    tq = bias.shape[3]
    sub = _tile(tq, DENSE_SUB, LANES)
    nq = seq // tq
    grp = n_heads // n_kv
    ctx_blk0 = n_lat // n_ctx
    qb, kb, vb = qcol // LANES, kcol // LANES, vcol // LANES
    per_head_bias = bias.shape[0] > 1
    use_rope, use_sink = rope is not None, sink is not None
    kern = functools.partial(_local_attn_kernel, tq=tq, nk=nk, back=back, seq=seq, sub=sub,
                             use_rope=use_rope, use_sink=use_sink)
    in_specs = [pl.BlockSpec((tq, LANES), lambda b, h, i: (b * nq + i, qb + h)),
                pl.BlockSpec((seq, LANES), lambda b, h, i: (b, kb + h // grp)),
                pl.BlockSpec((seq, LANES), lambda b, h, i: (b, vb + h // grp)),
                pl.BlockSpec((n_ctx, LANES), lambda b, h, i: (ctx_blk0 + b, kb + h // grp)),
                pl.BlockSpec((n_ctx, LANES), lambda b, h, i: (ctx_blk0 + b, vb + h // grp)),
                pl.BlockSpec((None, None, nk, tq),
                             lambda b, h, i: (h if per_head_bias else 0, bias_tile(i), 0, 0))]
    args = [u, u, u, u, u, bias]
    scratch = []
    if use_rope:
        in_specs += [pl.BlockSpec((tq, LANES), lambda b, h, i: (i, 0))] * 3
        in_specs += [pl.BlockSpec((seq, LANES), lambda b, h, i: (0, 0))] * 3
        args += [*rope, *rope]
        scratch.append(pltpu.VMEM((seq, LANES), BF16))
    scratch += [pltpu.VMEM((LANES, n_ctx), BF16), pltpu.VMEM((LANES, nk), BF16),
                pltpu.VMEM((tq // sub, n_ctx + nk, sub), F32)]
    if use_sink:
        in_specs.append(pl.BlockSpec(memory_space=pltpu.SMEM))
        args.append(sink)
    in_specs.append(pl.BlockSpec(memory_space=pl.ANY))
    args.append(o)
    return pl.pallas_call(
        kern, grid=(n_batch, n_heads, nq), in_specs=in_specs,
        out_specs=pl.BlockSpec((tq, LANES), lambda b, h, i: (b * nq + i, branch * 4 + h)),
        out_shape=jax.ShapeDtypeStruct(o.shape, o.dtype),
        scratch_shapes=scratch,
        input_output_aliases={len(args) - 1: 0},
        compiler_params=_params(3), name=name,
    )(*args)


def _local_tiling(seq, span):
    tq = _tile(seq, 512, 64)
    nk = min(seq, tq + 2 * span)
    return tq, nk


def _dedupe(keys):
    first, uniq, tile_map = {}, [], []
    for i, key in enumerate(keys):
        if key not in first:
            first[key] = len(uniq)
            uniq.append(i)
        tile_map.append(first[key])
    return uniq, tuple(tile_map)


def _swa_bias(seq):
    tq, nk = _local_tiling(seq, SWA_WINDOW)
    nq = seq // tq
    tiles = []
    for i in range(nq):
        ks = min(max(i * tq - SWA_WINDOW, 0), seq - nk)
        qp = i * tq + np.arange(tq)[:, None]
        kp = ks + np.arange(nk)[None, :]
        tiles.append(np.where(np.abs(qp - kp) <= SWA_WINDOW, 0.0, NEG_INF).astype(np.float32).T.copy())
    uniq, tile_map = _dedupe([t.tobytes() for t in tiles])
    return jnp.asarray(np.stack([tiles[i] for i in uniq])[None], F32), tile_map, nk, SWA_WINDOW


def _na_bias_kernel(rpb_ref, out_ref, toe_ref, *, row_offsets):
    n_d, n_c = 2 * NA_KH - 1, 2 * NA_KW - 1
    hd = pl.program_id(0)
    shape = (GRID_W, 2 * GRID_W)
    kc = lax.broadcasted_iota(jnp.int32, shape, 0)
    lane = lax.broadcasted_iota(jnp.int32, shape, 1)
    qc = lane & (GRID_W - 1)
    col_start = jnp.clip(qc - NA_KW // 2, 0, GRID_W - NA_KW)
    valid = (kc >= col_start) & (kc < col_start + NA_KW)
    dc = jnp.clip(kc - qc, -(NA_KW - 1), NA_KW - 1) + (NA_KW - 1)
    for d in range(n_d):
        blk = jnp.zeros(shape, F32)
        for j in range(n_c):
            blk = jnp.where(dc == j, rpb_ref[(hd * n_d + d) * n_c + j] * LOG2E, blk)
        toe_ref[d] = jnp.where(valid, blk, NEG_INF)
    toe_ref[n_d] = jnp.full(shape, NEG_INF, F32)
    left = lane < GRID_W
    for u, offs in enumerate(row_offsets):
        for a in range(0, len(offs), 2):
            for b in range(len(offs[a])):
                out_ref[u, b * GRID_W:(b + 1) * GRID_W, a * GRID_W:(a + 2) * GRID_W] = jnp.where(
                    left, toe_ref[offs[a][b]], toe_ref[offs[a + 1][b]])


def _na_bias(rpb, seq):
    rows = seq // GRID_W
    kh = min(NA_KH, rows)
    back_rows = kh // 2
    tq, nk = _local_tiling(seq, back_rows * GRID_W)
    nq, qr_n, kr_n = seq // tq, tq // GRID_W, nk // GRID_W
    assert qr_n % 2 == 0
    n_heads = rpb.shape[0]
    n_d = 2 * NA_KH - 1
    d_idx = []
    for i in range(nq):
        ks_row = min(max(i * qr_n - back_rows, 0), rows - kr_n)
        qr = i * qr_n + np.arange(qr_n)[:, None]
        kr = ks_row + np.arange(kr_n)[None, :]
        start = np.clip(qr - kh // 2, 0, rows - kh)
        ok = (kr >= start) & (kr < start + kh)
        d_idx.append(np.where(ok, kr - qr + (NA_KH - 1), n_d))
    uniq, tile_map = _dedupe([d.tobytes() for d in d_idx])
    row_offsets = tuple(tuple(tuple(int(v) for v in row) for row in d_idx[i]) for i in uniq)
    bias = pl.pallas_call(
        functools.partial(_na_bias_kernel, row_offsets=row_offsets), grid=(n_heads,),
        in_specs=[pl.BlockSpec(memory_space=pltpu.SMEM)],
        out_specs=pl.BlockSpec((None, len(uniq), nk, tq), lambda hd: (hd, 0, 0, 0)),
        out_shape=jax.ShapeDtypeStruct((n_heads, len(uniq), nk, tq), F32),
        scratch_shapes=[pltpu.VMEM((n_d + 1, GRID_W, 2 * GRID_W), F32)],
        compiler_params=_params(1), name="na_bias",
    )(rpb.reshape(-1))
    return bias, tile_map, nk, back_rows * GRID_W


def _ctx_attn_kernel(u_ref, qa_ref, ka_ref, va_ref, sink_ref, lam_ref, g_ref, oin_ref, o_ref, *, cols, lam_init):
    del oin_ref

    def col(name, hd, width=LANES):
        c0 = cols[name] + hd * width
        return u_ref[:, c0:c0 + width]

    for hd in range(MLA_H):
        q = qa_ref[:, hd * 2 * LANES:(hd + 1) * 2 * LANES]
        k = ka_ref[:, hd * 2 * LANES:(hd + 1) * 2 * LANES]
        v = va_ref[:, hd * LANES:(hd + 1) * LANES]
        o_ref[:, hd * LANES:(hd + 1) * LANES] = _softmax_pv([_dot_nt(q, k)], [v], base2=True).astype(BF16)
    grp = SWA_H // SWA_KV_H
    scale = SWA_HD ** -0.5
    for hd in range(SWA_H):
        q = (col("swa_q", hd).astype(F32) * scale).astype(BF16)
        o = _softmax_pv([_dot_nt(q, col("swa_k", hd // grp))], [col("swa_v", hd // grp)], sink=sink_ref[hd])
        o_ref[:, BRANCH_W + hd * LANES:BRANCH_W + (hd + 1) * LANES] = o.astype(BF16)
    scale = NA_HD ** -0.5
    for hd in range(NA_H):
        q = (col("na_q", hd).astype(F32) * scale).astype(BF16)
        o = _softmax_pv([_dot_nt(q, col("na_k", hd))], [col("na_v", hd)])
        o_ref[:, 2 * BRANCH_W + hd * LANES:2 * BRANCH_W + (hd + 1) * LANES] = o.astype(BF16)
    scale = DIFF_DK ** -0.5
    lam = _diff_lambda(lam_ref, lam_init)
    for hd in range(DIFF_H):
        q = col("diff_q", hd).astype(F32) * scale
        k = col("diff_k", hd)
        v = col("diff_v", hd)
        lane = lax.broadcasted_iota(jnp.int32, q.shape, 1)
        outs = []
        for j in range(2):
            qj = jnp.where((lane >= j * DIFF_DK) & (lane < (j + 1) * DIFF_DK), q, 0.0).astype(BF16)
            outs.append(_softmax_pv([_dot_nt(qj, k)], [v]))
        o = _diff_finish(outs[0], outs[1], lam, g_ref[...], lam_init)
        o_ref[:, 3 * BRANCH_W + hd * LANES:3 * BRANCH_W + (hd + 1) * LANES] = o.astype(BF16)


def _ctx_attention(u, cols, qa, ka, va, sink, diff_lambda_l, subln_g, lam_init, o, dims):
    n_lat, seq, n_batch = dims
    n_ctx = (u.shape[0] - n_lat) // n_batch
    blk0 = n_lat // n_ctx
    kern = functools.partial(_ctx_attn_kernel, cols=cols, lam_init=lam_init)
    return pl.pallas_call(
        kern, grid=(n_batch,),
        in_specs=[pl.BlockSpec((n_ctx, u.shape[1]), lambda b: (blk0 + b, 0)),
                  pl.BlockSpec((n_ctx, qa.shape[1]), lambda b: (blk0 + b, 0)),
                  pl.BlockSpec((n_ctx, ka.shape[1]), lambda b: (blk0 + b, 0)),
                  pl.BlockSpec((n_ctx, va.shape[1]), lambda b: (blk0 + b, 0)),
                  pl.BlockSpec(memory_space=pltpu.SMEM),
                  pl.BlockSpec((4, DIFF_DK), lambda b: (0, 0)),
                  pl.BlockSpec((1, DIFF_DV), lambda b: (0, 0)),
                  pl.BlockSpec(memory_space=pl.ANY)],
        out_specs=pl.BlockSpec((n_ctx, o.shape[1]), lambda b: (blk0 + b, 0)),
        out_shape=jax.ShapeDtypeStruct(o.shape, o.dtype),
        input_output_aliases={7: 0},
        compiler_params=_params(1), name="ctx_attention",
    )(u, qa, ka, va, sink, diff_lambda_l, subln_g.reshape(1, -1), o)


MOE_TM = 512
GATHER_ROWS = 256


def _route_plan(top_i, tm):
    t = top_i.shape[0]
    n_assign = t * TOP_K
    n_tiles = (n_assign + N_EXPERTS * (tm - 1)) // tm
    e_flat = top_i.reshape(-1)
    onehot = (e_flat[:, None] == jnp.arange(N_EXPERTS)[None, :]).astype(jnp.int32)
    csum = jnp.cumsum(onehot, axis=0)
    rank = jnp.sum(onehot * csum, axis=1) - 1
    counts = csum[-1]
    padded = ((counts + tm - 1) // tm) * tm
    ends = jnp.cumsum(padded)
    offs = ends - padded
    pos = (jnp.sum(onehot * offs[None, :], axis=1) + rank).astype(jnp.int32)
    src = jnp.zeros((n_tiles * tm,), jnp.int32).at[pos].set(
        jnp.arange(n_assign, dtype=jnp.int32) // TOP_K, unique_indices=True, mode="promise_in_bounds")
    tile_start = jnp.arange(n_tiles, dtype=jnp.int32) * tm
    tile_expert = jnp.minimum(jnp.sum((tile_start[:, None] >= ends[None, :]).astype(jnp.int32), axis=1),
                              N_EXPERTS - 1).astype(jnp.int32)
    tile_valid = (tile_start < ends[-1]).astype(jnp.int32)
    return pos, src, tile_expert, tile_valid


def _row_copies(idx_ref, base, src_hbm, bufs, sems, slot, n_rows, idx_stride):
    def body(r, carry):
        for s, buf in enumerate(bufs):
            row = idx_ref[(base + r) * idx_stride + s]
            pltpu.make_async_copy(src_hbm.at[pl.ds(row, 1), :], buf.at[slot, pl.ds(r, 1), :],
                                  sems.at[slot, s]).start()
        return carry
    lax.fori_loop(0, n_rows, body, 0, unroll=8)


def _wait_rows(src_hbm, bufs, sems, slot, n_rows):
    for s, buf in enumerate(bufs):
        pltpu.make_async_copy(src_hbm.at[pl.ds(0, n_rows), :], buf.at[slot], sems.at[slot, s]).wait()


def _gather_kernel(src_ref, tv_ref, x_hbm, o_ref, x_ref, buf_ref, sem, *, rows):
    i = pl.program_id(0)

    @pl.when(i == 0)
    def _():
        cp = pltpu.make_async_copy(x_hbm, x_ref, sem)
        cp.start()
        cp.wait()

    @pl.when(tv_ref[i] == 1)
    def _():
        def body(r, carry):
            buf_ref[pl.ds(r, 1), :] = x_ref[pl.ds(src_ref[i * rows + r], 1), :]
            return carry
        lax.fori_loop(0, rows, body, 0, unroll=8)
        w = buf_ref[...]
        half = w.shape[1]
        o_ref[:, :half] = lax.bitcast_convert_type(w << 16, F32).astype(BF16)
        o_ref[:, half:] = lax.bitcast_convert_type(w & jnp.uint32(0xFFFF0000), F32).astype(BF16)

    @pl.when(tv_ref[i] == 0)
    def _():
        o_ref[...] = jnp.zeros(o_ref.shape, BF16)


def _gather_rows(xp, src, tile_valid, rows):
    t, half = xp.shape
    n_rows = src.shape[0]
    kern = functools.partial(_gather_kernel, rows=rows)
    return pl.pallas_call(
        kern,
        grid_spec=pltpu.PrefetchScalarGridSpec(
            num_scalar_prefetch=2, grid=(n_rows // rows,),
            in_specs=[pl.BlockSpec(memory_space=pl.ANY)],
            out_specs=pl.BlockSpec((rows, 2 * half), lambda i, s, v: (i, 0)),
            scratch_shapes=[pltpu.VMEM((t, half), jnp.uint32), pltpu.VMEM((rows, half), jnp.uint32),
                            pltpu.SemaphoreType.DMA(())]),
        out_shape=jax.ShapeDtypeStruct((n_rows, 2 * half), BF16),
        compiler_params=_params(1), name="moe_gather",
    )(src, tile_valid, xp)


def _expert_changed(te_ref, i):
    return jnp.logical_or(i == 0, te_ref[i] != te_ref[jnp.maximum(i - 1, 0)])


def _gmm_up_kernel(te_ref, tv_ref, x_ref, w1_ref, w3_ref, o_ref, w1b_ref, w3b_ref):
    i = pl.program_id(1)
    _cast_weights(_expert_changed(te_ref, i), [w1_ref, w3_ref], [w1b_ref, w3b_ref])

    @pl.when(tv_ref[i] == 1)
    def _():
        x = x_ref[...]
        o_ref[...] = (_silu(_dot(x, w1b_ref[...])) * _dot(x, w3b_ref[...])).astype(BF16)

    @pl.when(tv_ref[i] == 0)
    def _():
        o_ref[...] = jnp.zeros(o_ref.shape, BF16)


def _gmm_up(xs, w1, w3, idx, tile_expert, tile_valid, tm):
    r, d = xs.shape
    f = w1.shape[3]
    tn = _tile(f, 512, LANES)
    return pl.pallas_call(
        _gmm_up_kernel,
        grid_spec=pltpu.PrefetchScalarGridSpec(
            num_scalar_prefetch=2, grid=(f // tn, r // tm),
            in_specs=[pl.BlockSpec((tm, d), lambda j, i, te, tv: (i, 0)),
                      pl.BlockSpec((None, None, d, tn), lambda j, i, te, tv: (idx, te[i], 0, j)),
                      pl.BlockSpec((None, None, d, tn), lambda j, i, te, tv: (idx, te[i], 0, j))],
            out_specs=pl.BlockSpec((tm, tn), lambda j, i, te, tv: (i, j)),
            scratch_shapes=[pltpu.VMEM((d, tn), BF16), pltpu.VMEM((d, tn), BF16)]),
        out_shape=jax.ShapeDtypeStruct((r, f), BF16),
        compiler_params=_params(2), name="moe_up",
    )(tile_expert, tile_valid, xs, w1, w3)


def _gmm_down_kernel(te_ref, tv_ref, a_ref, w_ref, o_ref, wb_ref):
    i = pl.program_id(1)
    _cast_weights(_expert_changed(te_ref, i), [w_ref], [wb_ref])

    @pl.when(tv_ref[i] == 1)
    def _():
        o_ref[...] = _dot(a_ref[...], wb_ref[...])

    @pl.when(tv_ref[i] == 0)
    def _():
        o_ref[...] = jnp.zeros(o_ref.shape, F32)


def _gmm_down(a, w2, idx, tile_expert, tile_valid, tm):
    r, f = a.shape
    d = w2.shape[3]
    tn = _tile(d, 1024, LANES)
    return pl.pallas_call(
        _gmm_down_kernel,
        grid_spec=pltpu.PrefetchScalarGridSpec(
            num_scalar_prefetch=2, grid=(d // tn, r // tm),
            in_specs=[pl.BlockSpec((tm, f), lambda j, i, te, tv: (i, 0)),
                      pl.BlockSpec((None, None, f, tn), lambda j, i, te, tv: (idx, te[i], 0, j),
                                   pipeline_mode=pl.Buffered(1))],
            out_specs=pl.BlockSpec((tm, tn), lambda j, i, te, tv: (i, j)),
            scratch_shapes=[pltpu.VMEM((f, tn), BF16)]),
        out_shape=jax.ShapeDtypeStruct((r, d), F32),
        compiler_params=_params(2), name="moe_down",
    )(tile_expert, tile_valid, a, w2)


def _combine_kernel(pos_ref, y_hbm, h_ref, tg_ref, gate_ref, *rest, rows, n_lat, seq, n_batch, follow):
    if follow == "norm":
        g_ref, sh_ref, sc_ref, o_ref, xn_ref, buf0, buf1, sems = rest
    else:
        g_ref, o_ref, buf0, buf1, sems = rest
    i = pl.program_id(0)
    n = pl.num_programs(0)
    slot = i % 2
    bufs = [buf0, buf1]

    @pl.when(i == 0)
    def _():
        _row_copies(pos_ref, 0, y_hbm, bufs, sems, 0, rows, TOP_K)

    @pl.when(i + 1 < n)
    def _():
        _row_copies(pos_ref, (i + 1) * rows, y_hbm, bufs, sems, 1 - slot, rows, TOP_K)

    _wait_rows(y_hbm, bufs, sems, slot, rows)
    grp = _group_of_tile(i, rows, n_lat, seq, n_batch)
    gate = gate_ref[pl.ds(grp, 1), :]
    tg = tg_ref[...]
    mix = tg[:, 0:1] * buf0[slot] + tg[:, 1:2] * buf1[slot]
    h_new = h_ref[...] + gate * mix
    if follow == "norm":
        o_ref[...] = h_new
        xn = _rms(h_new, g_ref[...]) * (1.0 + sc_ref[pl.ds(grp, 1), :]) + sh_ref[pl.ds(grp, 1), :]
        xn_ref[...] = xn.astype(BF16)
    else:
        o_ref[...] = _rms(h_new, g_ref[...])


def _combine(h, y, pos, top_g, mod_l, gate_chunk, dims, n_rows, follow, g, mod_next=None):
    n_lat, seq, n_batch = dims
    d = h.shape[1]
    rows = _tile(math.gcd(seq, h.shape[0] - n_lat), GATHER_ROWS, 8)
    kern = functools.partial(_combine_kernel, rows=rows, n_lat=n_lat, seq=seq, n_batch=n_batch, follow=follow)
    row = lambda i, p: (i, 0)
    in_specs = [pl.BlockSpec(memory_space=pl.ANY),
                pl.BlockSpec((rows, d), row),
                pl.BlockSpec((rows, LANES), row),
                pl.BlockSpec((8, d), lambda i, p: (0, gate_chunk)),
                pl.BlockSpec((1, d), lambda i, p: (0, 0))]
    args = [pos, y, h, top_g, mod_l, g.reshape(1, d)]
    if follow == "norm":
        in_specs += [pl.BlockSpec((8, d), lambda i, p: (0, 0)), pl.BlockSpec((8, d), lambda i, p: (0, 1))]
        args += [mod_next, mod_next]
        out_specs = [pl.BlockSpec((rows, d), row), pl.BlockSpec((rows, d), row)]
        out_shape = [jax.ShapeDtypeStruct(h.shape, F32), jax.ShapeDtypeStruct((n_rows, d), BF16)]
        aliases = {2: 0}
    else:
        out_specs = pl.BlockSpec((rows, d), row)
        out_shape = jax.ShapeDtypeStruct((n_rows, d), F32)
        aliases = {}
    return pl.pallas_call(
        kern,
        grid_spec=pltpu.PrefetchScalarGridSpec(
            num_scalar_prefetch=1, grid=(n_rows // rows,), in_specs=in_specs, out_specs=out_specs,
            scratch_shapes=[pltpu.VMEM((2, rows, d), F32), pltpu.VMEM((2, rows, d), F32),
                            pltpu.SemaphoreType.DMA((2, 2))]),
        out_shape=out_shape,
        input_output_aliases=aliases,
        compiler_params=_params(1), name="moe_combine",
    )(*args)


def kernel(x, c, ctx, c_ctx, mod_w, mod_b, norm1_g, norm2_g, w_in, mla_q_norm, mla_kv_norm, mla_w_uq, mla_w_ukv,
           swa_sink, na_rpb, diff_lambda, diff_subln_g, w_branch, w_out, ffn_w1, ffn_w3, ffn_w2, moe_router,
           moe_w1, moe_w3, moe_w2, final_norm_g):
    n_batch, seq, d = x.shape
    n_ctx = ctx.shape[1]
    depth = mod_w.shape[0]
    n_lat = n_batch * seq
    dims = (n_lat, seq, n_batch)

    t = n_lat + n_batch * n_ctx
    h, xn, out = None, None, None
    c8 = jnp.concatenate([c, c_ctx[None, :], jnp.zeros((8 - n_batch - 1, d), F32)], axis=0)
    mod = _mod_all(c8, mod_w, mod_b)

    cols, width_u, in_runs = _in_layout(d)
    w_in_p = _pad_w_in(w_in, width_u, in_runs)
    w_uq = mla_w_uq.reshape(depth, -1, MLA_H, MLA_NOPE + MLA_ROPE)
    w_uq_p = jnp.pad(w_uq, ((0, 0), (0, 0), (0, 0), (0, 2 * LANES - MLA_NOPE - MLA_ROPE)))
    w_uq_p = w_uq_p.reshape(depth, -1, MLA_H * 2 * LANES)
    rope64 = _rope_tables(seq, MLA_ROPE)
    rope128 = _rope_tables(seq, SWA_HD)
    swa_plan = _swa_bias(seq)
    router_t = jnp.swapaxes(moe_router, 1, 2)

    for l in range(depth):
        need_ctx = l < depth - 1
        lam_init = 0.8 - 0.6 * math.exp(-0.3 * l)
        mod_l = mod[l]

        n_rows = t if need_ctx else n_lat
        if h is None:
            h, xn = _join_norm_mod(x.reshape(n_lat, d), ctx.reshape(n_batch * n_ctx, d), norm1_g[l], mod_l, 0, 1, dims)
        elif xn is None:
            xn = _norm_mod(h, norm1_g[l], mod_l, 0, 1, dims)
        u = _project_in(xn, w_in_p, l)

        qa, ka, va = _mla_prep(u, cols, mla_q_norm[l], mla_kv_norm[l], w_uq_p[l], mla_w_ukv[l], rope64, dims)
        o = jnp.zeros((t, N_BRANCH * BRANCH_W), BF16)
        o = _mla_attention(qa, ka, va, o, dims, 0)
        o = _local_attention(u, cols["swa_q"], cols["swa_k"], cols["swa_v"], SWA_H, SWA_KV_H, swa_plan,
                             rope128, swa_sink[l], o, dims, 1, "swa_attention")
        o = _local_attention(u, cols["na_q"], cols["na_k"], cols["na_v"], NA_H, NA_H, _na_bias(na_rpb[l], seq),
                             None, None, o, dims, 2, "na_attention")
        o = _diff_attention(u, cols, rope64, diff_lambda[l], diff_subln_g[l], lam_init, o, dims, 3)
        if need_ctx:
            o = _ctx_attention(u, cols, qa, ka, va, swa_sink[l], diff_lambda[l], diff_subln_g[l], lam_init, o, dims)

        merged = _merge(o, u, cols["gates"], w_branch, l, n_rows)

        i = l // 2
        xn = None
        if l % 2 == 0:
            h, xn2 = _project_out_norm(merged, w_out, l, h, mod_l, norm2_g[l], dims, n_rows)
            a = _swiglu_up(xn2, ffn_w1, ffn_w3, i)
            h = _project_residual(a, ffn_w2, i, h, mod_l, 5, dims)
        else:
            h, xp, top_g, top_i = _project_out_norm(merged, w_out, l, h, mod_l, norm2_g[l], dims, n_rows,
                                                    router=router_t[i])
            pos, src, tile_expert, tile_valid = _route_plan(top_i[:, :TOP_K], MOE_TM)
            xs = _gather_rows(xp, src, tile_valid, MOE_TM)
            a = _gmm_up(xs, moe_w1, moe_w3, i, tile_expert, tile_valid, MOE_TM)
            y = _gmm_down(a, moe_w2, i, tile_expert, tile_valid, MOE_TM)
            if need_ctx:
                h, xn = _combine(h, y, pos, top_g, mod_l, 5, dims, n_rows, "norm", norm1_g[l + 1], mod[l + 1])
            else:
                out = _combine(h, y, pos, top_g, mod_l, 5, dims, n_rows, "final", final_norm_g)

    if out is None:
        out = _final_norm(h, final_norm_g, n_lat)
    return out.reshape(n_batch, seq, d)
```
